```python
import math
import jax
import jax.numpy as jnp
from jax import lax
import numpy as np

D_MODEL = 1024
BATCH = 8
SEQ = 4096
DEPTH = 1

MEM_TOKENS = 256
N_DIFF_HEADS = 8
DIFF_HEAD_DIM = 64
DIFF_V_DIM = 128
Q_BLOCK = 128
REL_BUCKETS = 32
REL_MAX_DIST = 128
N_DELTA_HEADS = 8
DELTA_HEAD_DIM = 128
CONV_WIDTH = 4
CHUNK = 64
N_MEM_HEADS = 4
MEM_HEAD_DIM = 256
N_BRANCHES = 3
PEER_HEADS = 8
PEER_KEYS = 128
PEER_EXPERTS = PEER_KEYS * PEER_KEYS
PEER_TOPK = 16
PEER_KEY_DIM = 256
PEER_HALF = PEER_KEY_DIM // 2
PEER_BLOCK = 128
EPS = 1e-6

DIFF_QK_W = N_DIFF_HEADS * 2 * DIFF_HEAD_DIM
DIFF_V_W = N_DIFF_HEADS * DIFF_V_DIM
DELTA_W = N_DELTA_HEADS * DELTA_HEAD_DIM
MEM_W = N_MEM_HEADS * MEM_HEAD_DIM
IN_SPLITS = (DIFF_QK_W, DIFF_QK_W, DIFF_V_W, 3 * DELTA_W, DELTA_W, N_DELTA_HEADS, N_DELTA_HEADS, MEM_W, N_BRANCHES * D_MODEL)
W_IN = 2 * DIFF_QK_W + DIFF_V_W + 4 * DELTA_W + 2 * N_DELTA_HEADS + MEM_W + N_BRANCHES * D_MODEL

kernel_name = 'hybrid_diffattn_deltanet_mem_peer'


def rms_norm(x, g):
    x32 = x.astype(jnp.float32)
    y = x32 * lax.rsqrt(jnp.mean(x32 * x32, axis=-1, keepdims=True) + EPS)
    return (y * g.astype(jnp.float32)).astype(x.dtype)


def l2_normalize(x):
    x32 = x.astype(jnp.float32)
    return x32 * lax.rsqrt(jnp.sum(x32 * x32, axis=-1, keepdims=True) + EPS)


def t5_bucket(dist):
    n = jnp.maximum(dist, 0)
    max_exact = REL_BUCKETS // 2
    nf = jnp.maximum(n, 1).astype(jnp.float32)
    large = max_exact + (jnp.log(nf / max_exact) / math.log(REL_MAX_DIST / max_exact)
                         * (REL_BUCKETS - max_exact)).astype(jnp.int32)
    large = jnp.minimum(large, REL_BUCKETS - 1)
    return jnp.where(n < max_exact, n, large)


def causal_depthwise_conv(x, w):
    c = x.shape[-1]
    return lax.conv_general_dilated(
        x, w.astype(x.dtype)[:, None, :], window_strides=(1,),
        padding=[(CONV_WIDTH - 1, 0)], dimension_numbers=('NWC', 'WIO', 'NWC'),
        feature_group_count=c)


def differential_attention(q, k, v, positions, rel_table, lam, lam_init, q_g, k_g, subln_g):
    b, t = q.shape[0], q.shape[1]
    nb = t // Q_BLOCK
    q = rms_norm(q, q_g)
    k = rms_norm(k, k_g)
    q_blocks = q.reshape(b, nb, Q_BLOCK, N_DIFF_HEADS, 2, DIFF_HEAD_DIM).transpose(1, 0, 3, 4, 2, 5)
    pos_blocks = positions.reshape(b, nb, Q_BLOCK).transpose(1, 0, 2)
    starts = jnp.arange(nb, dtype=jnp.int32) * Q_BLOCK
    k_t = k.transpose(0, 2, 3, 1, 4)
    v_t = v.transpose(0, 2, 1, 3)
    key_idx = jnp.arange(t, dtype=jnp.int32)
    scale = DIFF_HEAD_DIM ** -0.5

    def block(args):
        q_blk, pos_q, start = args
        s = jnp.einsum('bhiqd,bhikd->bhiqk', q_blk, k_t).astype(jnp.float32) * scale
        bucket = t5_bucket(pos_q[:, :, None] - positions[:, None, :])
        bias = jnp.take(rel_table, bucket, axis=0).reshape(b, Q_BLOCK, t, N_DIFF_HEADS, 2)
        s = s + bias.transpose(0, 3, 4, 1, 2).astype(jnp.float32)
        causal = key_idx[None, :] <= (start + jnp.arange(Q_BLOCK, dtype=jnp.int32))[:, None]
        p = jax.nn.softmax(jnp.where(causal, s, -jnp.inf), axis=-1)
        a = p[:, :, 0] - lam * p[:, :, 1]
        return jnp.einsum('bhqk,bhkv->bhqv', a.astype(v_t.dtype), v_t)

    o = lax.map(block, (q_blocks, pos_blocks, starts))
    o = o.transpose(1, 0, 3, 2, 4).reshape(b, t, N_DIFF_HEADS, DIFF_V_DIM)
    o = rms_norm(o, subln_g) * (1.0 - lam_init)
    return o.reshape(b, t, DIFF_V_W)


def chunked_gated_delta_rule(q, k, v, g, beta):
    b, t, h, dk = q.shape
    dv = v.shape[-1]
    n = t // CHUNK

    def chunks(a):
        a = a.reshape((b, n, CHUNK, h) + a.shape[3:])
        return jnp.moveaxis(a, (1, 3), (0, 2))

    qc, kc, vc, gc, bc = chunks(q), chunks(k), chunks(v), chunks(g), chunks(beta)
    gc = jnp.cumsum(gc, axis=-1)
    idx = jnp.arange(CHUNK)
    tril = idx[:, None] >= idx[None, :]
    strict = idx[:, None] > idx[None, :]
    decay = jnp.where(tril, jnp.exp(jnp.where(tril, gc[..., :, None] - gc[..., None, :], 0.0)), 0.0)
    k_beta = kc * bc[..., None]
    a_mat = jnp.where(strict, jnp.einsum('nbhid,nbhjd->nbhij', k_beta, kc) * decay, 0.0)
    lhs = a_mat + jnp.eye(CHUNK, dtype=a_mat.dtype)
    rhs = jnp.concatenate([vc * bc[..., None], k_beta * jnp.exp(gc)[..., None]], axis=-1)
    sol = lax.linalg.triangular_solve(lhs, rhs, left_side=True, lower=True, unit_diagonal=True)
    u, w = sol[..., :dv], sol[..., dv:]
    intra = jnp.where(tril, jnp.einsum('nbhid,nbhjd->nbhij', qc, kc) * decay, 0.0)

    def step(state, inp):
        q_i, k_i, u_i, w_i, g_i, a_i = inp
        v_new = u_i - jnp.einsum('bhck,bhkv->bhcv', w_i, state)
        o_i = (jnp.einsum('bhck,bhkv->bhcv', q_i * jnp.exp(g_i)[..., None], state)
               + jnp.einsum('bhij,bhjv->bhiv', a_i, v_new))
        g_last = g_i[..., -1:]
        k_dec = k_i * jnp.exp(g_last - g_i)[..., None]
        state = state * jnp.exp(g_last)[..., None] + jnp.einsum('bhck,bhcv->bhkv', k_dec, v_new)
        return state, o_i

    state0 = jnp.zeros((b, h, dk, dv), jnp.float32)
    _, o = lax.scan(step, state0, (qc, kc, u, w, gc, intra))
    return jnp.moveaxis(o, (0, 2), (1, 3)).reshape(b, t, h, dv)


def gated_deltanet(qkv, z, b_logit, a_logit, conv_w, a_log, dt_bias, out_g):
    b, t, _ = qkv.shape
    out_dtype = qkv.dtype
    qkv = jax.nn.silu(causal_depthwise_conv(qkv, conv_w)).astype(jnp.float32)
    q, k, v = jnp.split(qkv, 3, axis=-1)
    q = l2_normalize(q.reshape(b, t, N_DELTA_HEADS, DELTA_HEAD_DIM)) * (DELTA_HEAD_DIM ** -0.5)
    k = l2_normalize(k.reshape(b, t, N_DELTA_HEADS, DELTA_HEAD_DIM))
    v = v.reshape(b, t, N_DELTA_HEADS, DELTA_HEAD_DIM)
    beta = jax.nn.sigmoid(b_logit.astype(jnp.float32))
    g = -jnp.exp(a_log.astype(jnp.float32)) * jax.nn.softplus(a_logit.astype(jnp.float32) + dt_bias.astype(jnp.float32))
    o = chunked_gated_delta_rule(q, k, v, g, beta)
    o = rms_norm(o, out_g) * jax.nn.silu(z.astype(jnp.float32).reshape(b, t, N_DELTA_HEADS, DELTA_HEAD_DIM))
    return o.reshape(b, t, DELTA_W).astype(out_dtype)


def memory_attention(q, mem_h, w_mem_kv, q_g, k_g):
    b, t, _ = q.shape
    m = mem_h.shape[1]
    kv = (mem_h @ w_mem_kv).reshape(b, m, 2, N_MEM_HEADS, MEM_HEAD_DIM)
    k = rms_norm(kv[:, :, 0], k_g)
    v = kv[:, :, 1]
    q = rms_norm(q.reshape(b, t, N_MEM_HEADS, MEM_HEAD_DIM), q_g)
    s = jnp.einsum('bthd,bmhd->bhtm', q, k).astype(jnp.float32) * (MEM_HEAD_DIM ** -0.5)
    p = jax.nn.softmax(s, axis=-1)
    o = jnp.einsum('bhtm,bmhd->bthd', p.astype(v.dtype), v)
    return o.reshape(b, t, MEM_W)


def peer_ffn(h, w_query, sub_keys, expert_u, expert_v):
    b, t, d = h.shape
    qry = (h @ w_query).reshape(b, t, PEER_HEADS, 2, PEER_HALF)
    scores = jnp.einsum('bthpd,hpkd->bthpk', qry, sub_keys).astype(jnp.float32)
    s1, i1 = lax.top_k(scores[..., 0, :], PEER_TOPK)
    s2, i2 = lax.top_k(scores[..., 1, :], PEER_TOPK)
    n_cand = PEER_TOPK * PEER_TOPK
    cand_s = (s1[..., :, None] + s2[..., None, :]).reshape(b, t, PEER_HEADS, n_cand)
    cand_i = (i1[..., :, None] * PEER_KEYS + i2[..., None, :]).reshape(b, t, PEER_HEADS, n_cand)
    top_s, top_pos = lax.top_k(cand_s, PEER_TOPK)
    expert_idx = jnp.take_along_axis(cand_i, top_pos, axis=-1)
    gate = jax.nn.softmax(top_s, axis=-1)
    n_blk = (b * t) // PEER_BLOCK
    hb = h.reshape(n_blk, PEER_BLOCK, d)
    ib = expert_idx.reshape(n_blk, PEER_BLOCK, PEER_HEADS * PEER_TOPK)
    gb = gate.reshape(n_blk, PEER_BLOCK, PEER_HEADS * PEER_TOPK).astype(h.dtype)

    def block(args):
        h_blk, i_blk, g_blk = args
        u = jnp.take(expert_u, i_blk, axis=0)
        v = jnp.take(expert_v, i_blk, axis=0)
        act = jax.nn.gelu(jnp.einsum('td,ted->te', h_blk, u), approximate=False)
        return jnp.einsum('te,ted->td', g_blk * act, v)

    return lax.map(block, (hb, ib, gb)).reshape(b, t, d)


def setup_inputs(seed: int = 0) -> dict:
    key = jax.random.key(seed)
    ks = jax.random.split(key, 32)
    f32 = jnp.float32
    L = DEPTH

    def nrm(k, shape, scale):
        return jax.random.normal(k, shape, f32) * scale

    def gain(k, shape):
        return 1.0 + 0.1 * jax.random.normal(k, shape, f32)

    x = nrm(ks[0], (BATCH, SEQ, D_MODEL), 1.0)
    mem = nrm(ks[1], (BATCH, MEM_TOKENS, D_MODEL), 1.0)
    offsets = jax.random.randint(ks[2], (BATCH, 1), 0, 1024, jnp.int32)
    positions = offsets + jnp.arange(SEQ, dtype=jnp.int32)[None, :]
    attn_norm_g = gain(ks[3], (L, D_MODEL))
    mem_norm_g = gain(ks[4], (L, D_MODEL))
    w_in = nrm(ks[5], (L, D_MODEL, W_IN), D_MODEL ** -0.5)
    b_gate = nrm(ks[6], (L, N_BRANCHES * D_MODEL), 0.1)
    diff_q_norm_g = gain(ks[7], (L, DIFF_HEAD_DIM))
    diff_k_norm_g = gain(ks[8], (L, DIFF_HEAD_DIM))
    lambda_q1 = nrm(ks[9], (L, DIFF_HEAD_DIM), 0.1)
    lambda_k1 = nrm(ks[10], (L, DIFF_HEAD_DIM), 0.1)
    lambda_q2 = nrm(ks[11], (L, DIFF_HEAD_DIM), 0.1)
    lambda_k2 = nrm(ks[12], (L, DIFF_HEAD_DIM), 0.1)
    diff_subln_g = gain(ks[13], (L, DIFF_V_DIM))
    rel_bias_table = nrm(ks[14], (REL_BUCKETS, N_DIFF_HEADS * 2), 0.5)
    conv_w = nrm(ks[15], (L, CONV_WIDTH, 3 * DELTA_W), CONV_WIDTH ** -0.5)
    a_log = jnp.log(jax.random.uniform(ks[16], (L, N_DELTA_HEADS), f32, 1.0, 16.0))
    dt = jnp.exp(jax.random.uniform(ks[17], (L, N_DELTA_HEADS), f32, math.log(1e-3), math.log(1e-1)))
    dt_bias = jnp.log(jnp.expm1(dt))
    delta_out_norm_g = gain(ks[18], (L, DELTA_HEAD_DIM))
    w_mem_kv = nrm(ks[19], (L, D_MODEL, 2 * MEM_W), D_MODEL ** -0.5)
    mem_q_norm_g = gain(ks[20], (L, MEM_HEAD_DIM))
    mem_k_norm_g = gain(ks[21], (L, MEM_HEAD_DIM))
    w_br_diff = nrm(ks[22], (L, DIFF_V_W, D_MODEL), DIFF_V_W ** -0.5)
    w_br_delta = nrm(ks[23], (L, DELTA_W, D_MODEL), DELTA_W ** -0.5)
    w_br_mem = nrm(ks[24], (L, MEM_W, D_MODEL), MEM_W ** -0.5)
    w_out = nrm(ks[25], (L, D_MODEL, D_MODEL), D_MODEL ** -0.5)
    ffn_norm_g = gain(ks[26], (L, D_MODEL))
    w_query = nrm(ks[27], (L, D_MODEL, PEER_HEADS * PEER_KEY_DIM), D_MODEL ** -0.5)
    sub_keys = nrm(ks[28], (L, PEER_HEADS, 2, PEER_KEYS, PEER_HALF), PEER_HALF ** -0.5)
    expert_u = nrm(ks[29], (L, PEER_EXPERTS, D_MODEL), D_MODEL ** -0.5)
    expert_v = nrm(ks[30], (L, PEER_EXPERTS, D_MODEL), (PEER_HEADS * PEER_TOPK) ** -0.5)
    return {'x': x, 'mem': mem, 'positions': positions, 'attn_norm_g': attn_norm_g,
            'mem_norm_g': mem_norm_g, 'w_in': w_in, 'b_gate': b_gate,
            'diff_q_norm_g': diff_q_norm_g, 'diff_k_norm_g': diff_k_norm_g,
            'lambda_q1': lambda_q1, 'lambda_k1': lambda_k1, 'lambda_q2': lambda_q2,
            'lambda_k2': lambda_k2, 'diff_subln_g': diff_subln_g, 'rel_bias_table': rel_bias_table,
            'conv_w': conv_w, 'a_log': a_log, 'dt_bias': dt_bias, 'delta_out_norm_g': delta_out_norm_g,
            'w_mem_kv': w_mem_kv, 'mem_q_norm_g': mem_q_norm_g, 'mem_k_norm_g': mem_k_norm_g,
            'w_br_diff': w_br_diff, 'w_br_delta': w_br_delta, 'w_br_mem': w_br_mem, 'w_out': w_out,
            'ffn_norm_g': ffn_norm_g, 'w_query': w_query, 'sub_keys': sub_keys,
            'expert_u': expert_u, 'expert_v': expert_v}


def reference(x, mem, positions, attn_norm_g, mem_norm_g, w_in, b_gate, diff_q_norm_g, diff_k_norm_g,
              lambda_q1, lambda_k1, lambda_q2, lambda_k2, diff_subln_g, rel_bias_table, conv_w, a_log,
              dt_bias, delta_out_norm_g, w_mem_kv, mem_q_norm_g, mem_k_norm_g, w_br_diff, w_br_delta,
              w_br_mem, w_out, ffn_norm_g, w_query, sub_keys, expert_u, expert_v):
    b, t, _ = x.shape
    split_at = np.cumsum(IN_SPLITS)[:-1].tolist()
    for l in range(DEPTH):
        lam_init = 0.8 - 0.6 * math.exp(-0.3 * l)
        h = rms_norm(x, attn_norm_g[l])
        proj = h @ w_in[l]
        dq, dk, dv, lqkv, lz, lb, la, mq, gate_logits = jnp.split(proj, split_at, axis=-1)
        lam = (jnp.exp(jnp.sum(lambda_q1[l].astype(jnp.float32) * lambda_k1[l].astype(jnp.float32)))
               - jnp.exp(jnp.sum(lambda_q2[l].astype(jnp.float32) * lambda_k2[l].astype(jnp.float32)))
               + lam_init)
        y_diff = differential_attention(
            dq.reshape(b, t, N_DIFF_HEADS, 2, DIFF_HEAD_DIM),
            dk.reshape(b, t, N_DIFF_HEADS, 2, DIFF_HEAD_DIM),
            dv.reshape(b, t, N_DIFF_HEADS, DIFF_V_DIM),
            positions, rel_bias_table, lam, lam_init,
            diff_q_norm_g[l], diff_k_norm_g[l], diff_subln_g[l])
        y_delta = gated_deltanet(lqkv, lz, lb, la, conv_w[l], a_log[l], dt_bias[l], delta_out_norm_g[l])
        mem_h = rms_norm(mem, mem_norm_g[l])
        y_mem = memory_attention(mq, mem_h, w_mem_kv[l], mem_q_norm_g[l], mem_k_norm_g[l])
        gates = jax.nn.sigmoid(gate_logits + b_gate[l]).reshape(b, t, N_BRANCHES, D_MODEL)
        merged = (gates[:, :, 0] * (y_diff @ w_br_diff[l])
                  + gates[:, :, 1] * (y_delta @ w_br_delta[l])
                  + gates[:, :, 2] * (y_mem @ w_br_mem[l]))
        x = x + merged @ w_out[l]
        x = x + peer_ffn(rms_norm(x, ffn_norm_g[l]), w_query[l], sub_keys[l], expert_u[l], expert_v[l])
    return x
```

```python
import functools
import math

import jax
import jax.numpy as jnp
from jax import lax
from jax.experimental import pallas as pl
from jax.experimental.pallas import tpu as pltpu

F32 = jnp.float32
BF16 = jnp.bfloat16
EPS = 1e-6
NEG = -1e30

N_DIFF_HEADS = 8
DIFF_HEAD_DIM = 64
N_DELTA_HEADS = 8
DELTA_HEAD_DIM = 128
CONV_WIDTH = 4
CHUNK = 64
N_MEM_HEADS = 4
MEM_HEAD_DIM = 256
REL_BUCKETS = 32
REL_MAX_DIST = 128
PEER_HEADS = 8
PEER_KEYS = 128
PEER_TOPK = 16
PEER_HALF = 128
LANES = 128
SUBLANES = 8
VMEM_LIMIT = 56 * 1024 * 1024


def _cparams(sem):
    return pltpu.CompilerParams(dimension_semantics=sem, vmem_limit_bytes=VMEM_LIMIT)


def _norm_matmul_kernel(*refs, has_norm, has_group):
    it = iter(refs)
    x_ref = next(it)
    g_ref = next(it) if has_norm else None
    w_ref = next(it)
    gm_ref = next(it) if has_group else None
    gain_ref = next(it) if has_group else None
    o_ref = next(it)
    h_ref = next(it)

    @pl.when(pl.program_id(1) == 0)
    def _():
        x = x_ref[...].astype(F32)
        if has_norm:
            ms = jnp.mean(x * x, axis=-1, keepdims=True)
            x = x * lax.rsqrt(ms + EPS) * g_ref[...]
        h_ref[...] = x.astype(BF16)

    y = jnp.dot(h_ref[...], w_ref[...], preferred_element_type=F32)
    if has_group:
        ms = jnp.dot((y * y).astype(BF16), gm_ref[...], preferred_element_type=F32)
        y = y * lax.rsqrt(ms + EPS) * gain_ref[...]
    o_ref[...] = y.astype(o_ref.dtype)


def norm_matmul(x, w, *, norm_g=None, group=None, group_gain=None, out_dtype=BF16, tm=512, tn=None):
    m, k = x.shape
    n = w.shape[1]
    has_norm = norm_g is not None
    has_group = group is not None
    if tn is None:
        tn = 256 if has_group else min(n, 1024)
    tm = min(tm, m)
    assert m % tm == 0 and n % tn == 0
    in_specs = [pl.BlockSpec((tm, k), lambda i, j: (i, 0))]
    args = [x]
    if has_norm:
        in_specs.append(pl.BlockSpec((1, k), lambda i, j: (0, 0)))
        args.append(norm_g.reshape(1, k).astype(F32))
    in_specs.append(pl.BlockSpec((k, tn), lambda i, j: (0, j)))
    args.append(w)
    if has_group:
        gid = jnp.arange(tn) // group
        gm = jnp.where(gid[:, None] == gid[None, :], 1.0 / group, 0.0).astype(BF16)
        in_specs.append(pl.BlockSpec((tn, tn), lambda i, j: (0, 0)))
        args.append(gm)
        in_specs.append(pl.BlockSpec((1, tn), lambda i, j: (0, j)))
        args.append(group_gain.reshape(1, n).astype(F32))
    return pl.pallas_call(
        functools.partial(_norm_matmul_kernel, has_norm=has_norm, has_group=has_group),
        out_shape=jax.ShapeDtypeStruct((m, n), out_dtype),
        grid=(m // tm, n // tn),
        in_specs=in_specs,
        out_specs=pl.BlockSpec((tm, tn), lambda i, j: (i, j)),
        scratch_shapes=[pltpu.VMEM((tm, k), BF16)],
        compiler_params=_cparams(("parallel", "arbitrary")),
        name="norm_matmul",
    )(*args)


def _t5_bucket(n):
    max_exact = REL_BUCKETS // 2
    nf = jnp.maximum(n, 1).astype(F32)
    large = max_exact + (jnp.log(nf / max_exact) / math.log(REL_MAX_DIST / max_exact)
                         * (REL_BUCKETS - max_exact)).astype(jnp.int32)
    large = jnp.minimum(large, REL_BUCKETS - 1)
    return jnp.where(n < max_exact, n, large)


def _rel_bias_kernel(table_ref, o_ref, *, tq):
    hm = pl.program_id(0)
    r = lax.broadcasted_iota(jnp.int32, (tq, 2 * tq), 0)
    c = lax.broadcasted_iota(jnp.int32, (tq, 2 * tq), 1)
    bucket = _t5_bucket(jnp.maximum(r - c + tq, 0))
    acc = jnp.zeros((tq, 2 * tq), F32)
    for b in range(REL_BUCKETS):
        acc = jnp.where(bucket == b, table_ref[b, hm], acc)
    o_ref[0] = acc


def rel_bias_tiles(rel_table, tq):
    nmaps = rel_table.shape[1]
    return pl.pallas_call(
        functools.partial(_rel_bias_kernel, tq=tq),
        out_shape=jax.ShapeDtypeStruct((nmaps, tq, 2 * tq), F32),
        grid=(nmaps,),
        in_specs=[pl.BlockSpec(memory_space=pltpu.SMEM)],
        out_specs=pl.BlockSpec((1, tq, 2 * tq), lambda i: (i, 0, 0)),
        compiler_params=_cparams(("arbitrary",)),
        name="rel_bias_tiles",
    )(rel_table.astype(F32))


def _diff_attn_kernel(scal_ref, q_ref, k_ref, v_ref, bias_ref, subg_ref, o_ref, *, tq, far_bucket_dist):
    del far_bucket_dist
    h = pl.program_id(1)
    i = pl.program_id(2)
    lam = scal_ref[0]
    out_scale = scal_ref[1]
    c1 = scal_ref[2 + 2 * h]
    c2 = scal_ref[3 + 2 * h]

    q = q_ref[...]
    lane = lax.broadcasted_iota(jnp.int32, (tq, LANES), 1)
    zero = jnp.zeros_like(q)
    qq = jnp.concatenate([jnp.where(lane < DIFF_HEAD_DIM, q, zero),
                          jnp.where(lane >= DIFF_HEAD_DIM, q, zero)], axis=0)
    row2 = lax.broadcasted_iota(jnp.int32, (2 * tq, 1), 0)
    cfar = jnp.where(row2 < tq, c1, c2)

    def update(carry, s, vj):
        m, l, acc = carry
        m_new = jnp.maximum(m, jnp.max(s, axis=-1, keepdims=True))
        alpha = jnp.exp(m - m_new)
        p = jnp.exp(s - m_new)
        l = l * alpha + jnp.sum(p, axis=-1, keepdims=True)
        acc = acc * alpha + jnp.dot(p.astype(BF16), vj, preferred_element_type=F32)
        return m_new, l, acc

    def scores(j):
        kj = k_ref[pl.ds(pl.multiple_of(j * tq, tq), tq), :]
        vj = v_ref[pl.ds(pl.multiple_of(j * tq, tq), tq), :]
        s = lax.dot_general(qq, kj, (((1,), (1,)), ((), ())), preferred_element_type=F32)
        return s, vj

    def far_step(j, carry):
        s, vj = scores(j)
        return update(carry, s + cfar, vj)

    init = (jnp.full((2 * tq, 1), NEG, F32), jnp.zeros((2 * tq, 1), F32), jnp.zeros((2 * tq, LANES), F32))
    carry = lax.fori_loop(0, jnp.maximum(i - 1, 0), far_step, init)

    b_prev = jnp.concatenate([bias_ref[0, :, 0:tq], bias_ref[1, :, 0:tq]], axis=0)
    s, vj = scores(jnp.maximum(i - 1, 0))
    s = jnp.where(i >= 1, s + b_prev, NEG)
    carry = update(carry, s, vj)

    b_diag = jnp.concatenate([bias_ref[0, :, tq:2 * tq], bias_ref[1, :, tq:2 * tq]], axis=0)
    s, vj = scores(i)
    rq = lax.broadcasted_iota(jnp.int32, (2 * tq, tq), 0)
    rq = jnp.where(rq >= tq, rq - tq, rq)
    ck = lax.broadcasted_iota(jnp.int32, (2 * tq, tq), 1)
    s = jnp.where(ck <= rq, s + b_diag, NEG)
    m, l, acc = update(carry, s, vj)

    o = acc / l
    o = o[0:tq] - lam * o[tq:2 * tq]
    ms = jnp.mean(o * o, axis=-1, keepdims=True)
    o = o * lax.rsqrt(ms + EPS) * subg_ref[...] * out_scale
    o_ref[...] = o.astype(o_ref.dtype)


def diff_attention(qk, rest, bias_tiles, scal, subln_g, *, b, t, tq, v_col0):
    m = b * t
    nq = t // tq
    nh = N_DIFF_HEADS
    return pl.pallas_call(
        functools.partial(_diff_attn_kernel, tq=tq, far_bucket_dist=None),
        out_shape=jax.ShapeDtypeStruct((m, nh * LANES), BF16),
        grid=(b, nh, nq),
        in_specs=[
            pl.BlockSpec(memory_space=pltpu.SMEM),
            pl.BlockSpec((tq, LANES), lambda bi, h, i: (bi * nq + i, h)),
            pl.BlockSpec((t, LANES), lambda bi, h, i: (bi, nh + h)),
            pl.BlockSpec((t, LANES), lambda bi, h, i: (bi, v_col0 + h)),
            pl.BlockSpec((2, tq, 2 * tq), lambda bi, h, i: (h, 0, 0)),
            pl.BlockSpec((1, LANES), lambda bi, h, i: (0, 0)),
        ],
        out_specs=pl.BlockSpec((tq, LANES), lambda bi, h, i: (bi * nq + i, h)),
        compiler_params=_cparams(("parallel", "parallel", "arbitrary")),
        name="diff_attention",
    )(scal, qk, qk, rest, bias_tiles, subln_g.reshape(1, LANES).astype(F32))


def _sigmoid(x):
    return 1.0 / (1.0 + jnp.exp(-x))


def _softplus(x):
    return jnp.maximum(x, 0.0) + jnp.log(1.0 + jnp.exp(-jnp.abs(x)))


def _delta_kernel(hp_ref, xq_ref, xk_ref, xv_ref, pq_ref, pk_ref, pv_ref, wq_ref, wk_ref, wv_ref,
                  ab_ref, abt_ref, z_ref, og_ref, o_ref, s_ref, *, tt):
    h = pl.program_id(1)
    i = pl.program_id(2)
    nc = tt // CHUNK
    a_log = hp_ref[0, h]
    dt_bias = hp_ref[1, h]

    @pl.when(i == 0)
    def _():
        s_ref[...] = jnp.zeros_like(s_ref)

    def conv_silu(x_ref, p_ref, w_ref):
        prev = jnp.where(i > 0, p_ref[...].astype(F32), 0.0)
        xf = jnp.concatenate([prev, x_ref[...].astype(F32)], axis=0)
        w = w_ref[...]
        base = SUBLANES - (CONV_WIDTH - 1)
        y = xf[base:base + tt] * w[0:1]
        for c in range(1, CONV_WIDTH):
            y = y + xf[base + c:base + c + tt] * w[c:c + 1]
        return y * _sigmoid(y)

    q = conv_silu(xq_ref, pq_ref, wq_ref)
    k = conv_silu(xk_ref, pk_ref, wk_ref)
    v = conv_silu(xv_ref, pv_ref, wv_ref)
    q = q * lax.rsqrt(jnp.sum(q * q, axis=-1, keepdims=True) + EPS) * (DELTA_HEAD_DIM ** -0.5)
    k = k * lax.rsqrt(jnp.sum(k * k, axis=-1, keepdims=True) + EPS)

    ab = ab_ref[...]
    lane = lax.broadcasted_iota(jnp.int32, ab.shape, 1)
    lb_col = jnp.sum(jnp.where(lane == h, ab, 0.0), axis=-1, keepdims=True)
    la_col = jnp.sum(jnp.where(lane == N_DELTA_HEADS + h, ab, 0.0), axis=-1, keepdims=True)
    la_row = abt_ref[pl.ds(N_DELTA_HEADS + h, 1), :]
    neg_a = -jnp.exp(jnp.full((1, 1), a_log, F32))
    beta_col = _sigmoid(lb_col)
    g_col = neg_a * _softplus(la_col + dt_bias)
    g_row = neg_a * _softplus(la_row + dt_bias)

    ri = lax.broadcasted_iota(jnp.int32, (CHUNK, CHUNK), 0)
    ci = lax.broadcasted_iota(jnp.int32, (CHUNK, CHUNK), 1)
    tril = ri >= ci
    strict = ri > ci

    def mm(a, b):
        return jnp.dot(a.astype(BF16), b.astype(BF16), preferred_element_type=F32)

    def mm_nt(a, b):
        return lax.dot_general(a.astype(BF16), b.astype(BF16), (((1,), (1,)), ((), ())),
                               preferred_element_type=F32)

    def mm_tn(a, b):
        return lax.dot_general(a.astype(BF16), b.astype(BF16), (((0,), (0,)), ((), ())),
                               preferred_element_type=F32)

    state = s_ref[...]
    outs = []
    for c in range(nc):
        r0 = c * CHUNK
        qc, kc, vc = q[r0:r0 + CHUNK], k[r0:r0 + CHUNK], v[r0:r0 + CHUNK]
        bc = beta_col[r0:r0 + CHUNK]
        gcol = g_col[r0:r0 + CHUNK]
        grow = g_row[:, r0:r0 + CHUNK]
        gc_col = jnp.sum(jnp.where(tril, grow, 0.0), axis=-1, keepdims=True)
        gc_row = jnp.sum(jnp.where(ri <= ci, gcol, 0.0), axis=0, keepdims=True)
        g_last = jnp.sum(grow, axis=-1, keepdims=True)
        decay = jnp.where(tril, jnp.exp(jnp.where(tril, gc_col - gc_row, 0.0)), 0.0)
        kk = mm_nt(kc, kc)
        x = jnp.where(strict, -(bc * kk * decay), 0.0)
        p = x
        xp = x
        for _ in range(5):
            xp = mm(xp, xp)
            p = p + xp + mm(p, xp)
        rhs_v = bc * vc
        rhs_k = (bc * jnp.exp(gc_col)) * kc
        u = rhs_v + mm(p, rhs_v)
        w = rhs_k + mm(p, rhs_k)
        intra = jnp.where(tril, mm_nt(qc, kc) * decay, 0.0)
        v_new = u - mm(w, state)
        o = mm(qc * jnp.exp(gc_col), state) + mm(intra, v_new)
        k_dec = kc * jnp.exp(g_last - gc_col)
        state = state * jnp.exp(g_last) + mm_tn(k_dec, v_new)
        outs.append(o)
    s_ref[...] = state

    o = jnp.concatenate(outs, axis=0)
    ms = jnp.mean(o * o, axis=-1, keepdims=True)
    z = z_ref[...].astype(F32)
    o = o * lax.rsqrt(ms + EPS) * og_ref[...] * (z * _sigmoid(z))
    o_ref[...] = o.astype(o_ref.dtype)


def gated_deltanet(rest, ab, abt, conv_w, head_params, out_g, *, b, t, tt, qkv_col0, z_col0):
    m = b * t
    nt = t // tt
    nh = N_DELTA_HEADS
    sub = tt // SUBLANES

    def cur(off):
        return pl.BlockSpec((tt, LANES), lambda bi, h, i: (bi * nt + i, off + h))

    def prev(off):
        return pl.BlockSpec((SUBLANES, LANES),
                            lambda bi, h, i: (jnp.maximum((bi * nt + i) * sub - 1, 0), off + h))

    def wspec(off):
        return pl.BlockSpec((CONV_WIDTH, LANES), lambda bi, h, i: (0, off + h))

    return pl.pallas_call(
        functools.partial(_delta_kernel, tt=tt),
        out_shape=jax.ShapeDtypeStruct((m, nh * LANES), BF16),
        grid=(b, nh, nt),
        in_specs=[
            pl.BlockSpec(memory_space=pltpu.SMEM),
            cur(qkv_col0), cur(qkv_col0 + nh), cur(qkv_col0 + 2 * nh),
            prev(qkv_col0), prev(qkv_col0 + nh), prev(qkv_col0 + 2 * nh),
            wspec(0), wspec(nh), wspec(2 * nh),
            pl.BlockSpec((tt, LANES), lambda bi, h, i: (bi * nt + i, 0)),
            pl.BlockSpec((2 * nh, tt), lambda bi, h, i: (0, bi * nt + i)),
            cur(z_col0),
            pl.BlockSpec((1, LANES), lambda bi, h, i: (0, 0)),
        ],
        out_specs=pl.BlockSpec((tt, LANES), lambda bi, h, i: (bi * nt + i, h)),
        scratch_shapes=[pltpu.VMEM((DELTA_HEAD_DIM, DELTA_HEAD_DIM), F32)],
        compiler_params=_cparams(("parallel", "parallel", "arbitrary")),
        name="gated_deltanet",
    )(head_params, rest, rest, rest, rest, rest, rest, conv_w, conv_w, conv_w, ab, abt, rest,
      out_g.reshape(1, LANES).astype(F32))


def _mem_attn_kernel(q_ref, k_ref, v_ref, o_ref):
    outs = []
    for h in range(N_MEM_HEADS):
        sl = slice(h * MEM_HEAD_DIM, (h + 1) * MEM_HEAD_DIM)
        s = lax.dot_general(q_ref[:, sl], k_ref[:, sl], (((1,), (1,)), ((), ())), preferred_element_type=F32)
        s = s - jnp.max(s, axis=-1, keepdims=True)
        p = jnp.exp(s)
        p = p / jnp.sum(p, axis=-1, keepdims=True)
        outs.append(jnp.dot(p.astype(BF16), v_ref[:, sl], preferred_element_type=F32))
    o_ref[...] = jnp.concatenate(outs, axis=-1).astype(o_ref.dtype)


def memory_attention(mq, mk, mv, *, b, t, mtok, tq):
    nq = t // tq
    w = N_MEM_HEADS * MEM_HEAD_DIM
    return pl.pallas_call(
        _mem_attn_kernel,
        out_shape=jax.ShapeDtypeStruct((b * t, w), BF16),
        grid=(b, nq),
        in_specs=[
            pl.BlockSpec((tq, w), lambda bi, i: (bi * nq + i, 0)),
            pl.BlockSpec((mtok, w), lambda bi, i: (bi, 0)),
            pl.BlockSpec((mtok, w), lambda bi, i: (bi, 0)),
        ],
        out_specs=pl.BlockSpec((tq, w), lambda bi, i: (bi * nq + i, 0)),
        compiler_params=_cparams(("parallel", "arbitrary")),
        name="memory_attention",
    )(mq, mk, mv)


def _merge_kernel(x_ref, ya_ref, yb_ref, yc_ref, ga_ref, gb_ref, gc_ref, bg_ref,
                  wa_ref, wb_ref, wc_ref, wo_ref, fg_ref, x1_ref, h2_ref):
    def branch(y_ref, g_ref, w_ref, idx):
        gate = _sigmoid(g_ref[...].astype(F32) + bg_ref[idx:idx + 1, :])
        return gate * jnp.dot(y_ref[...], w_ref[...], preferred_element_type=F32)

    merged = branch(ya_ref, ga_ref, wa_ref, 0) + branch(yb_ref, gb_ref, wb_ref, 1) + branch(yc_ref, gc_ref, wc_ref, 2)
    x1 = x_ref[...] + jnp.dot(merged.astype(BF16), wo_ref[...], preferred_element_type=F32)
    x1_ref[...] = x1
    ms = jnp.mean(x1 * x1, axis=-1, keepdims=True)
    h2_ref[...] = (x1 * lax.rsqrt(ms + EPS) * fg_ref[...]).astype(h2_ref.dtype)


def merge_project(x2d, y_diff, y_delta, y_mem, rest, b_gate, w_a, w_b, w_c, w_o, ffn_g, *, gate_col0, tm):
    m, d = x2d.shape
    row = lambda i: (i, 0)
    const = lambda i: (0, 0)
    assert gate_col0 % d == 0

    def gspec(j):
        return pl.BlockSpec((tm, d), lambda i: (i, gate_col0 // d + j))

    return pl.pallas_call(
        _merge_kernel,
        out_shape=(jax.ShapeDtypeStruct((m, d), F32), jax.ShapeDtypeStruct((m, d), BF16)),
        grid=(m // tm,),
        in_specs=[
            pl.BlockSpec((tm, d), row), pl.BlockSpec((tm, d), row), pl.BlockSpec((tm, d), row),
            pl.BlockSpec((tm, d), row), gspec(0), gspec(1), gspec(2),
            pl.BlockSpec((3, d), const),
            pl.BlockSpec((d, d), const), pl.BlockSpec((d, d), const), pl.BlockSpec((d, d), const),
            pl.BlockSpec((d, d), const), pl.BlockSpec((1, d), const),
        ],
        out_specs=(pl.BlockSpec((tm, d), row), pl.BlockSpec((tm, d), row)),
        compiler_params=_cparams(("parallel",)),
        name="merge_project",
    )(x2d, y_diff, y_delta, y_mem, rest, rest, rest, b_gate.reshape(3, d).astype(F32),
      w_a, w_b, w_c, w_o, ffn_g.reshape(1, d).astype(F32))


def _topk_rows(s, extra, k):
    nrow = s.shape[0]
    rid = lax.broadcasted_iota(jnp.int32, s.shape, 0)
    vals, rows, ext = [], [], []
    for _ in range(k):
        m = jnp.max(s, axis=0, keepdims=True)
        am = jnp.min(jnp.where(s == m, rid, nrow), axis=0, keepdims=True)
        hit = rid == am
        if extra is not None:
            ext.append(jnp.max(jnp.where(hit, extra, -1), axis=0, keepdims=True))
        s = jnp.where(hit, -jnp.inf, s)
        vals.append(m)
        rows.append(am)
    cat = lambda xs: jnp.concatenate(xs, axis=0)
    return cat(vals), cat(rows), (cat(ext) if extra is not None else None)


def _peer_route_kernel(q_ref, keys_ref, idx_ref, gate_ref):
    k = PEER_TOPK
    q = q_ref[...]

    def half(p):
        qp = q[:, p * PEER_HALF:(p + 1) * PEER_HALF]
        st = lax.dot_general(keys_ref[0, p], qp, (((1,), (1,)), ((), ())), preferred_element_type=F32)
        v, r, _ = _topk_rows(st, None, k)
        return v, r

    s1, i1 = half(0)
    s2, i2 = half(1)
    cand_s = jnp.concatenate([s1[a:a + 1] + s2 for a in range(k)], axis=0)
    cand_e = jnp.concatenate([i1[a:a + 1] * PEER_KEYS + i2 for a in range(k)], axis=0)
    top_s, _, top_e = _topk_rows(cand_s, cand_e, k)
    e = jnp.exp(top_s - top_s[0:1])
    gate_ref[0] = e / jnp.sum(e, axis=0, keepdims=True)
    idx_ref[0] = top_e


def peer_route(qry, sub_keys, *, tt):
    m = qry.shape[0]
    nh = PEER_HEADS
    return pl.pallas_call(
        _peer_route_kernel,
        out_shape=(jax.ShapeDtypeStruct((nh, PEER_TOPK, m), jnp.int32),
                   jax.ShapeDtypeStruct((nh, PEER_TOPK, m), F32)),
        grid=(m // tt, nh),
        in_specs=[
            pl.BlockSpec((tt, 2 * PEER_HALF), lambda i, h: (i, h)),
            pl.BlockSpec((1, 2, PEER_KEYS, PEER_HALF), lambda i, h: (h, 0, 0, 0)),
        ],
        out_specs=(pl.BlockSpec((1, PEER_TOPK, tt), lambda i, h: (h, 0, i)),
                   pl.BlockSpec((1, PEER_TOPK, tt), lambda i, h: (h, 0, i))),
        compiler_params=_cparams(("parallel", "arbitrary")),
        name="peer_route",
    )(qry, sub_keys)


def _gelu_exact(x):
    return 0.5 * x * (1.0 + lax.erf(x * (2.0 ** -0.5)))


def _peer_expert_kernel(idx_ref, idxn_ref, gate_ref, h_ref, x_ref, pool_ref, poolt_ref, u_hbm, v_hbm,
                        o_ref, ubuf, vbuf, usem, vsem, *, tb, ne):
    i = pl.program_id(0)
    n = pl.num_programs(0)
    slot = i % 2
    rows = tb * ne
    d_sub = u_hbm.shape[1]

    def issue(iref, s):
        def body(r, carry):
            e = iref[r // ne, r % ne]
            pltpu.make_async_copy(u_hbm.at[e], ubuf.at[s, r], usem.at[s]).start()
            pltpu.make_async_copy(v_hbm.at[e], vbuf.at[s, r], vsem.at[s]).start()
            return carry
        lax.fori_loop(0, rows, body, 0, unroll=8)

    @pl.when(i == 0)
    def _():
        issue(idx_ref, 0)

    @pl.when(i + 1 < n)
    def _():
        issue(idxn_ref, 1 - slot)

    pltpu.make_async_copy(u_hbm.at[pl.ds(0, rows)], ubuf.at[slot], usem.at[slot]).wait()
    pltpu.make_async_copy(v_hbm.at[pl.ds(0, rows)], vbuf.at[slot], vsem.at[slot]).wait()

    sub_id = lax.broadcasted_iota(jnp.int32, (d_sub, ne * d_sub), 0)
    col_id = lax.broadcasted_iota(jnp.int32, (d_sub, ne * d_sub), 1)
    diag = (col_id % d_sub) == sub_id

    drows = []
    for t in range(tb):
        zu = ubuf[slot, pl.ds(t * ne, ne)].reshape(ne * d_sub, LANES).astype(BF16)
        c = lax.dot_general(h_ref[t], zu, (((1,), (1,)), ((), ())), preferred_element_type=F32)
        drows.append(jnp.sum(jnp.where(diag, c, 0.0), axis=0, keepdims=True))
    dall = jnp.concatenate(drows, axis=0)
    d_hi = dall.astype(BF16)
    d_lo = (dall - d_hi.astype(F32)).astype(BF16)
    a = (jnp.dot(d_hi, pool_ref[...], preferred_element_type=F32)
         + jnp.dot(d_lo, pool_ref[...], preferred_element_type=F32))
    w = gate_ref[...] * _gelu_exact(a)
    wrep = jnp.dot(w.astype(BF16), poolt_ref[...], preferred_element_type=F32)

    for t in range(tb):
        wexp = jnp.where(diag, wrep[t:t + 1, :], 0.0).astype(BF16)
        zv = vbuf[slot, pl.ds(t * ne, ne)].reshape(ne * d_sub, LANES).astype(BF16)
        o_ref[t] = x_ref[t] + jnp.dot(wexp, zv, preferred_element_type=F32)


def peer_experts(idx, gates, h3, x3, u3, v3, *, tb):
    m, ne = idx.shape
    d_sub = h3.shape[1]
    n = m // tb
    rows = tb * ne
    cid = jnp.arange(ne * d_sub) // d_sub
    pool = (cid[:, None] == jnp.arange(ne)[None, :]).astype(BF16)
    tok = lambda i: (i, 0)
    tok3 = lambda i: (i, 0, 0)
    return pl.pallas_call(
        functools.partial(_peer_expert_kernel, tb=tb, ne=ne),
        out_shape=jax.ShapeDtypeStruct(x3.shape, F32),
        grid=(n,),
        in_specs=[
            pl.BlockSpec((tb, ne), tok, memory_space=pltpu.SMEM),
            pl.BlockSpec((tb, ne), lambda i: (jnp.minimum(i + 1, n - 1), 0), memory_space=pltpu.SMEM),
            pl.BlockSpec((tb, ne), tok),
            pl.BlockSpec((tb, d_sub, LANES), tok3),
            pl.BlockSpec((tb, d_sub, LANES), tok3),
            pl.BlockSpec((ne * d_sub, ne), lambda i: (0, 0)),
            pl.BlockSpec((ne, ne * d_sub), lambda i: (0, 0)),
            pl.BlockSpec(memory_space=pl.ANY),
            pl.BlockSpec(memory_space=pl.ANY),
        ],
        out_specs=pl.BlockSpec((tb, d_sub, LANES), tok3),
        scratch_shapes=[
            pltpu.VMEM((2, rows, d_sub, LANES), F32),
            pltpu.VMEM((2, rows, d_sub, LANES), F32),
            pltpu.SemaphoreType.DMA((2,)),
            pltpu.SemaphoreType.DMA((2,)),
        ],
        compiler_params=_cparams(("arbitrary",)),
        name="peer_experts",
    )(idx, idx, gates, h3, x3, pool, pool.T, u3, v3)


def kernel(x, mem, positions, attn_norm_g, mem_norm_g, w_in, b_gate, diff_q_norm_g, diff_k_norm_g, lambda_q1, lambda_k1, lambda_q2, lambda_k2, diff_subln_g, rel_bias_table, conv_w, a_log, dt_bias, delta_out_norm_g, w_mem_kv, mem_q_norm_g, mem_k_norm_g, w_br_diff, w_br_delta, w_br_mem, w_out, ffn_norm_g, w_query, sub_keys, expert_u, expert_v):
    del positions
    b, t, d = x.shape
    mtok = mem.shape[1]
    m = b * t
    depth = w_in.shape[0]
    nh = N_DIFF_HEADS
    qk_w = nh * 2 * DIFF_HEAD_DIM
    dv_w = nh * 2 * DIFF_HEAD_DIM
    dl_w = N_DELTA_HEADS * DELTA_HEAD_DIM
    mem_w = N_MEM_HEADS * MEM_HEAD_DIM
    tq = min(128, t)
    tt_delta = min(256, t)

    x2d = x.reshape(m, d)
    for l in range(depth):
        lam_init = 0.8 - 0.6 * math.exp(-0.3 * l)
        wl = w_in[l]
        o = 0
        w_qk = wl[:, o:o + 2 * qk_w]; o += 2 * qk_w
        w_dv = wl[:, o:o + dv_w]; o += dv_w
        w_lqkv = wl[:, o:o + 3 * dl_w]; o += 3 * dl_w
        w_lz = wl[:, o:o + dl_w]; o += dl_w
        w_ab = wl[:, o:o + 2 * N_DELTA_HEADS]; o += 2 * N_DELTA_HEADS
        w_mq = wl[:, o:o + mem_w]; o += mem_w
        w_gate = wl[:, o:o + 3 * d]

        qk_gain = jnp.concatenate([jnp.tile(diff_q_norm_g[l], 2 * nh) * (DIFF_HEAD_DIM ** -0.5),
                                   jnp.tile(diff_k_norm_g[l], 2 * nh)])
        qk = norm_matmul(x2d, w_qk.astype(BF16), norm_g=attn_norm_g[l], group=DIFF_HEAD_DIM, group_gain=qk_gain)
        w_rest = jnp.concatenate([w_dv, w_lqkv, w_lz, w_gate], axis=1).astype(BF16)
        rest = norm_matmul(x2d, w_rest, norm_g=attn_norm_g[l])
        mq_gain = jnp.tile(mem_q_norm_g[l], N_MEM_HEADS) * (MEM_HEAD_DIM ** -0.5)
        mq = norm_matmul(x2d, w_mq.astype(BF16), norm_g=attn_norm_g[l], group=MEM_HEAD_DIM, group_gain=mq_gain)
        w_ab_pad = jnp.pad(w_ab, ((0, 0), (0, LANES - 2 * N_DELTA_HEADS))).astype(BF16)
        ab = norm_matmul(x2d, w_ab_pad, norm_g=attn_norm_g[l], out_dtype=F32, tn=LANES)
        abt = ab[:, :2 * N_DELTA_HEADS].T

        lam = (jnp.exp(jnp.sum(lambda_q1[l].astype(F32) * lambda_k1[l].astype(F32)))
               - jnp.exp(jnp.sum(lambda_q2[l].astype(F32) * lambda_k2[l].astype(F32))) + lam_init)
        bias_tiles = rel_bias_tiles(rel_bias_table, tq)
        scal = jnp.concatenate([jnp.stack([lam, jnp.asarray(1.0 - lam_init, F32)]),
                                rel_bias_table[REL_BUCKETS - 1].astype(F32)])
        y_diff = diff_attention(qk, rest, bias_tiles, scal, diff_subln_g[l], b=b, t=t, tq=tq, v_col0=0)

        head_params = jnp.stack([a_log[l], dt_bias[l]]).astype(F32)
        y_delta = gated_deltanet(rest, ab, abt, conv_w[l].astype(F32), head_params, delta_out_norm_g[l],
                                 b=b, t=t, tt=tt_delta, qkv_col0=dv_w // LANES,
                                 z_col0=(dv_w + 3 * dl_w) // LANES)

        mem2d = mem.reshape(b * mtok, d)
        wkv = w_mem_kv[l].astype(BF16)
        mk = norm_matmul(mem2d, wkv[:, :mem_w], norm_g=mem_norm_g[l], group=MEM_HEAD_DIM,
                         group_gain=jnp.tile(mem_k_norm_g[l], N_MEM_HEADS))
        mv = norm_matmul(mem2d, wkv[:, mem_w:], norm_g=mem_norm_g[l])
        y_mem = memory_attention(mq, mk, mv, b=b, t=t, mtok=mtok, tq=min(512, t))

        x1, h2 = merge_project(x2d, y_diff, y_delta, y_mem, rest, b_gate[l],
                               w_br_diff[l].astype(BF16), w_br_delta[l].astype(BF16), w_br_mem[l].astype(BF16),
                               w_out[l].astype(BF16), ffn_norm_g[l],
                               gate_col0=dv_w + 4 * dl_w, tm=min(512, m))

        qry = norm_matmul(h2, w_query[l].astype(BF16))
        idx_t, gate_t = peer_route(qry, sub_keys[l].astype(BF16), tt=min(512, m))
        ne = PEER_HEADS * PEER_TOPK
        idx = idx_t.reshape(ne, m).T
        gates = gate_t.reshape(ne, m).T
        sub = d // LANES
        x2 = peer_experts(idx, gates, h2.reshape(m, sub, LANES), x1.reshape(m, sub, LANES),
                          expert_u[l].reshape(-1, sub, LANES), expert_v[l].reshape(-1, sub, LANES), tb=8)
        x2d = x2.reshape(m, d)
    return x2d.reshape(b, t, d)
```

```python
import functools
import math

import jax
import jax.numpy as jnp
from jax import lax
from jax.experimental import pallas as pl
from jax.experimental.pallas import tpu as pltpu

F32 = jnp.float32
BF16 = jnp.bfloat16
EPS = 1e-6
NEG = -1e30

N_DIFF_HEADS = 8
DIFF_HEAD_DIM = 64
N_DELTA_HEADS = 8
DELTA_HEAD_DIM = 128
CONV_WIDTH = 4
CHUNK = 64
N_MEM_HEADS = 4
MEM_HEAD_DIM = 256
REL_BUCKETS = 32
REL_MAX_DIST = 128
PEER_HEADS = 8
PEER_KEYS = 128
PEER_TOPK = 16
PEER_HALF = 128
LANES = 128
SUBLANES = 8
VMEM_LIMIT = 56 * 1024 * 1024


def _cparams(sem):
    return pltpu.CompilerParams(dimension_semantics=sem, vmem_limit_bytes=VMEM_LIMIT)


def _norm_matmul_kernel(*refs, has_norm, has_group):
    it = iter(refs)
    x_ref = next(it)
    g_ref = next(it) if has_norm else None
    w_ref = next(it)
    gm_ref = next(it) if has_group else None
    gain_ref = next(it) if has_group else None
    o_ref = next(it)
    h_ref = next(it)

    @pl.when(pl.program_id(1) == 0)
    def _():
        x = x_ref[...].astype(F32)
        if has_norm:
            ms = jnp.mean(x * x, axis=-1, keepdims=True)
            x = x * lax.rsqrt(ms + EPS) * g_ref[...]
        h_ref[...] = x.astype(BF16)

    y = jnp.dot(h_ref[...], w_ref[...], preferred_element_type=F32)
    if has_group:
        ms = jnp.dot((y * y).astype(BF16), gm_ref[...], preferred_element_type=F32)
        y = y * lax.rsqrt(ms + EPS) * gain_ref[...]
    o_ref[...] = y.astype(o_ref.dtype)


def norm_matmul(x, w, *, norm_g=None, group=None, group_gain=None, out_dtype=BF16, tm=512, tn=None):
    m, k = x.shape
    n = w.shape[1]
    has_norm = norm_g is not None
    has_group = group is not None
    if tn is None:
        tn = 256 if has_group else min(n, 1024)
    tm = min(tm, m)
    assert m % tm == 0 and n % tn == 0
    in_specs = [pl.BlockSpec((tm, k), lambda i, j: (i, 0))]
    args = [x]
    if has_norm:
        in_specs.append(pl.BlockSpec((1, k), lambda i, j: (0, 0)))
        args.append(norm_g.reshape(1, k).astype(F32))
    in_specs.append(pl.BlockSpec((k, tn), lambda i, j: (0, j)))
    args.append(w)
    if has_group:
        gid = jnp.arange(tn) // group
        gm = jnp.where(gid[:, None] == gid[None, :], 1.0 / group, 0.0).astype(BF16)
        in_specs.append(pl.BlockSpec((tn, tn), lambda i, j: (0, 0)))
        args.append(gm)
        in_specs.append(pl.BlockSpec((1, tn), lambda i, j: (0, j)))
        args.append(group_gain.reshape(1, n).astype(F32))
    return pl.pallas_call(
        functools.partial(_norm_matmul_kernel, has_norm=has_norm, has_group=has_group),
        out_shape=jax.ShapeDtypeStruct((m, n), out_dtype),
        grid=(m // tm, n // tn),
        in_specs=in_specs,
        out_specs=pl.BlockSpec((tm, tn), lambda i, j: (i, j)),
        scratch_shapes=[pltpu.VMEM((tm, k), BF16)],
        compiler_params=_cparams(("parallel", "arbitrary")),
        name="norm_matmul",
    )(*args)


def _t5_bucket(n):
    max_exact = REL_BUCKETS // 2
    nf = jnp.maximum(n, 1).astype(F32)
    large = max_exact + (jnp.log(nf / max_exact) / math.log(REL_MAX_DIST / max_exact)
                         * (REL_BUCKETS - max_exact)).astype(jnp.int32)
    large = jnp.minimum(large, REL_BUCKETS - 1)
    return jnp.where(n < max_exact, n, large)


def _rel_bias_kernel(table_ref, o_ref, *, tq):
    hm = pl.program_id(0)
    r = lax.broadcasted_iota(jnp.int32, (tq, 2 * tq), 0)
    c = lax.broadcasted_iota(jnp.int32, (tq, 2 * tq), 1)
    bucket = _t5_bucket(jnp.maximum(r - c + tq, 0))
    acc = jnp.zeros((tq, 2 * tq), F32)
    for b in range(REL_BUCKETS):
        acc = jnp.where(bucket == b, table_ref[b, hm], acc)
    o_ref[0] = acc


def rel_bias_tiles(rel_table, tq):
    nmaps = rel_table.shape[1]
    return pl.pallas_call(
        functools.partial(_rel_bias_kernel, tq=tq),
        out_shape=jax.ShapeDtypeStruct((nmaps, tq, 2 * tq), F32),
        grid=(nmaps,),
        in_specs=[pl.BlockSpec(memory_space=pltpu.SMEM)],
        out_specs=pl.BlockSpec((1, tq, 2 * tq), lambda i: (i, 0, 0)),
        compiler_params=_cparams(("arbitrary",)),
        name="rel_bias_tiles",
    )(rel_table.astype(F32))


def _diff_attn_kernel(scal_ref, q_ref, k_ref, v_ref, bias_ref, subg_ref, o_ref, *, tq, far_bucket_dist):
    del far_bucket_dist
    h = pl.program_id(1)
    i = pl.program_id(2)
    lam = scal_ref[0]
    out_scale = scal_ref[1]
    c1 = scal_ref[2 + 2 * h]
    c2 = scal_ref[3 + 2 * h]

    q = q_ref[...]
    lane = lax.broadcasted_iota(jnp.int32, (tq, LANES), 1)
    zero = jnp.zeros_like(q)
    qq = jnp.concatenate([jnp.where(lane < DIFF_HEAD_DIM, q, zero),
                          jnp.where(lane >= DIFF_HEAD_DIM, q, zero)], axis=0)
    row2 = lax.broadcasted_iota(jnp.int32, (2 * tq, 1), 0)
    cfar = jnp.where(row2 < tq, c1, c2)

    def update(carry, s, vj):
        m, l, acc = carry
        m_new = jnp.maximum(m, jnp.max(s, axis=-1, keepdims=True))
        alpha = jnp.exp(m - m_new)
        p = jnp.exp(s - m_new)
        l = l * alpha + jnp.sum(p, axis=-1, keepdims=True)
        acc = acc * alpha + jnp.dot(p.astype(BF16), vj, preferred_element_type=F32)
        return m_new, l, acc

    def scores(start, width):
        kj = k_ref[pl.ds(pl.multiple_of(start, tq), width), :]
        vj = v_ref[pl.ds(pl.multiple_of(start, tq), width), :]
        s = lax.dot_general(qq, kj, (((1,), (1,)), ((), ())), preferred_element_type=F32)
        return s, vj

    def far_step(j, carry):
        s, vj = scores(j * (2 * tq), 2 * tq)
        return update(carry, s + cfar, vj)

    def far_single(carry):
        s, vj = scores((i - 2) * tq, tq)
        return update(carry, s + cfar, vj)

    n_far = jnp.maximum(i - 1, 0)
    init = (jnp.full((2 * tq, 1), NEG, F32), jnp.zeros((2 * tq, 1), F32), jnp.zeros((2 * tq, LANES), F32))
    carry = lax.fori_loop(0, n_far // 2, far_step, init)
    carry = lax.cond(n_far % 2 == 1, far_single, lambda c: c, carry)

    b_prev = jnp.concatenate([bias_ref[0, :, 0:tq], bias_ref[1, :, 0:tq]], axis=0)
    s, vj = scores(jnp.maximum(i - 1, 0) * tq, tq)
    s = jnp.where(i >= 1, s + b_prev, NEG)
    carry = update(carry, s, vj)

    b_diag = jnp.concatenate([bias_ref[0, :, tq:2 * tq], bias_ref[1, :, tq:2 * tq]], axis=0)
    s, vj = scores(i * tq, tq)
    rq = lax.broadcasted_iota(jnp.int32, (2 * tq, tq), 0)
    rq = jnp.where(rq >= tq, rq - tq, rq)
    ck = lax.broadcasted_iota(jnp.int32, (2 * tq, tq), 1)
    s = jnp.where(ck <= rq, s + b_diag, NEG)
    m, l, acc = update(carry, s, vj)

    o = acc / l
    o = o[0:tq] - lam * o[tq:2 * tq]
    ms = jnp.mean(o * o, axis=-1, keepdims=True)
    o = o * lax.rsqrt(ms + EPS) * subg_ref[...] * out_scale
    o_ref[...] = o.astype(o_ref.dtype)


def diff_attention(qk, rest, bias_tiles, scal, subln_g, *, b, t, tq, v_col0):
    m = b * t
    nq = t // tq
    nh = N_DIFF_HEADS
    return pl.pallas_call(
        functools.partial(_diff_attn_kernel, tq=tq, far_bucket_dist=None),
        out_shape=jax.ShapeDtypeStruct((m, nh * LANES), BF16),
        grid=(b, nh, nq),
        in_specs=[
            pl.BlockSpec(memory_space=pltpu.SMEM),
            pl.BlockSpec((tq, LANES), lambda bi, h, i: (bi * nq + i, h)),
            pl.BlockSpec((t, LANES), lambda bi, h, i: (bi, nh + h)),
            pl.BlockSpec((t, LANES), lambda bi, h, i: (bi, v_col0 + h)),
            pl.BlockSpec((2, tq, 2 * tq), lambda bi, h, i: (h, 0, 0)),
            pl.BlockSpec((1, LANES), lambda bi, h, i: (0, 0)),
        ],
        out_specs=pl.BlockSpec((tq, LANES), lambda bi, h, i: (bi * nq + i, h)),
        compiler_params=_cparams(("parallel", "parallel", "arbitrary")),
        name="diff_attention",
    )(scal, qk, qk, rest, bias_tiles, subln_g.reshape(1, LANES).astype(F32))


def _sigmoid(x):
    return 1.0 / (1.0 + jnp.exp(-x))


def _softplus(x):
    return jnp.maximum(x, 0.0) + jnp.log(1.0 + jnp.exp(-jnp.abs(x)))


def _delta_kernel(hp_ref, xq_ref, xk_ref, xv_ref, pq_ref, pk_ref, pv_ref, wq_ref, wk_ref, wv_ref,
                  ab_ref, abt_ref, z_ref, og_ref, o_ref, s_ref, *, tt):
    h = pl.program_id(1)
    i = pl.program_id(2)
    nc = tt // CHUNK
    a_log = hp_ref[0, h]
    dt_bias = hp_ref[1, h]

    @pl.when(i == 0)
    def _():
        s_ref[...] = jnp.zeros_like(s_ref)

    def conv_silu(x_ref, p_ref, w_ref):
        prev = jnp.where(i > 0, p_ref[...].astype(F32), 0.0)
        xf = jnp.concatenate([prev, x_ref[...].astype(F32)], axis=0)
        w = w_ref[...]
        base = SUBLANES - (CONV_WIDTH - 1)
        y = xf[base:base + tt] * w[0:1]
        for c in range(1, CONV_WIDTH):
            y = y + xf[base + c:base + c + tt] * w[c:c + 1]
        return y * _sigmoid(y)

    q = conv_silu(xq_ref, pq_ref, wq_ref)
    k = conv_silu(xk_ref, pk_ref, wk_ref)
    v = conv_silu(xv_ref, pv_ref, wv_ref)
    q = q * lax.rsqrt(jnp.sum(q * q, axis=-1, keepdims=True) + EPS) * (DELTA_HEAD_DIM ** -0.5)
    k = k * lax.rsqrt(jnp.sum(k * k, axis=-1, keepdims=True) + EPS)

    ab = ab_ref[...]
    lane = lax.broadcasted_iota(jnp.int32, ab.shape, 1)
    lb_col = jnp.sum(jnp.where(lane == h, ab, 0.0), axis=-1, keepdims=True)
    la_col = jnp.sum(jnp.where(lane == N_DELTA_HEADS + h, ab, 0.0), axis=-1, keepdims=True)
    la_row = abt_ref[pl.ds(N_DELTA_HEADS + h, 1), :]
    neg_a = -jnp.exp(jnp.full((1, 1), a_log, F32))
    beta_col = _sigmoid(lb_col)
    g_col = neg_a * _softplus(la_col + dt_bias)
    g_row = neg_a * _softplus(la_row + dt_bias)

    ri = lax.broadcasted_iota(jnp.int32, (CHUNK, CHUNK), 0)
    ci = lax.broadcasted_iota(jnp.int32, (CHUNK, CHUNK), 1)
    tril = ri >= ci
    strict = ri > ci

    def mm(a, b):
        return jnp.dot(a.astype(BF16), b.astype(BF16), preferred_element_type=F32)

    def mm_nt(a, b):
        return lax.dot_general(a.astype(BF16), b.astype(BF16), (((1,), (1,)), ((), ())),
                               preferred_element_type=F32)

    def mm_tn(a, b):
        return lax.dot_general(a.astype(BF16), b.astype(BF16), (((0,), (0,)), ((), ())),
                               preferred_element_type=F32)

    state = s_ref[...]
    outs = []
    for c in range(nc):
        r0 = c * CHUNK
        qc, kc, vc = q[r0:r0 + CHUNK], k[r0:r0 + CHUNK], v[r0:r0 + CHUNK]
        bc = beta_col[r0:r0 + CHUNK]
        gcol = g_col[r0:r0 + CHUNK]
        grow = g_row[:, r0:r0 + CHUNK]
        gc_col = jnp.sum(jnp.where(tril, grow, 0.0), axis=-1, keepdims=True)
        gc_row = jnp.sum(jnp.where(ri <= ci, gcol, 0.0), axis=0, keepdims=True)
        g_last = jnp.sum(grow, axis=-1, keepdims=True)
        decay = jnp.where(tril, jnp.exp(jnp.where(tril, gc_col - gc_row, 0.0)), 0.0)
        kk = mm_nt(kc, kc)
        x = jnp.where(strict, -(bc * kk * decay), 0.0)
        p = x
        xp = x
        for _ in range(5):
            xp = mm(xp, xp)
            p = p + xp + mm(p, xp)
        rhs_v = bc * vc
        rhs_k = (bc * jnp.exp(gc_col)) * kc
        u = rhs_v + mm(p, rhs_v)
        w = rhs_k + mm(p, rhs_k)
        intra = jnp.where(tril, mm_nt(qc, kc) * decay, 0.0)
        v_new = u - mm(w, state)
        o = mm(qc * jnp.exp(gc_col), state) + mm(intra, v_new)
        k_dec = kc * jnp.exp(g_last - gc_col)
        state = state * jnp.exp(g_last) + mm_tn(k_dec, v_new)
        outs.append(o)
    s_ref[...] = state

    o = jnp.concatenate(outs, axis=0)
    ms = jnp.mean(o * o, axis=-1, keepdims=True)
    z = z_ref[...].astype(F32)
    o = o * lax.rsqrt(ms + EPS) * og_ref[...] * (z * _sigmoid(z))
    o_ref[...] = o.astype(o_ref.dtype)


def gated_deltanet(rest, ab, abt, conv_w, head_params, out_g, *, b, t, tt, qkv_col0, z_col0):
    m = b * t
    nt = t // tt
    nh = N_DELTA_HEADS
    sub = tt // SUBLANES

    def cur(off):
        return pl.BlockSpec((tt, LANES), lambda bi, h, i: (bi * nt + i, off + h))

    def prev(off):
        return pl.BlockSpec((SUBLANES, LANES),
                            lambda bi, h, i: (jnp.maximum((bi * nt + i) * sub - 1, 0), off + h))

    def wspec(off):
        return pl.BlockSpec((CONV_WIDTH, LANES), lambda bi, h, i: (0, off + h))

    return pl.pallas_call(
        functools.partial(_delta_kernel, tt=tt),
        out_shape=jax.ShapeDtypeStruct((m, nh * LANES), BF16),
        grid=(b, nh, nt),
        in_specs=[
            pl.BlockSpec(memory_space=pltpu.SMEM),
            cur(qkv_col0), cur(qkv_col0 + nh), cur(qkv_col0 + 2 * nh),
            prev(qkv_col0), prev(qkv_col0 + nh), prev(qkv_col0 + 2 * nh),
            wspec(0), wspec(nh), wspec(2 * nh),
            pl.BlockSpec((tt, LANES), lambda bi, h, i: (bi * nt + i, 0)),
            pl.BlockSpec((2 * nh, tt), lambda bi, h, i: (0, bi * nt + i)),
            cur(z_col0),
            pl.BlockSpec((1, LANES), lambda bi, h, i: (0, 0)),
        ],
        out_specs=pl.BlockSpec((tt, LANES), lambda bi, h, i: (bi * nt + i, h)),
        scratch_shapes=[pltpu.VMEM((DELTA_HEAD_DIM, DELTA_HEAD_DIM), F32)],
        compiler_params=_cparams(("parallel", "parallel", "arbitrary")),
        name="gated_deltanet",
    )(head_params, rest, rest, rest, rest, rest, rest, conv_w, conv_w, conv_w, ab, abt, rest,
      out_g.reshape(1, LANES).astype(F32))


def _mem_attn_kernel(q_ref, k_ref, v_ref, o_ref):
    outs = []
    for h in range(N_MEM_HEADS):
        sl = slice(h * MEM_HEAD_DIM, (h + 1) * MEM_HEAD_DIM)
        s = lax.dot_general(q_ref[:, sl], k_ref[:, sl], (((1,), (1,)), ((), ())), preferred_element_type=F32)
        s = s - jnp.max(s, axis=-1, keepdims=True)
        p = jnp.exp(s)
        p = p / jnp.sum(p, axis=-1, keepdims=True)
        outs.append(jnp.dot(p.astype(BF16), v_ref[:, sl], preferred_element_type=F32))
    o_ref[...] = jnp.concatenate(outs, axis=-1).astype(o_ref.dtype)


def memory_attention(mq, mk, mv, *, b, t, mtok, tq):
    nq = t // tq
    w = N_MEM_HEADS * MEM_HEAD_DIM
    return pl.pallas_call(
        _mem_attn_kernel,
        out_shape=jax.ShapeDtypeStruct((b * t, w), BF16),
        grid=(b, nq),
        in_specs=[
            pl.BlockSpec((tq, w), lambda bi, i: (bi * nq + i, 0)),
            pl.BlockSpec((mtok, w), lambda bi, i: (bi, 0)),
            pl.BlockSpec((mtok, w), lambda bi, i: (bi, 0)),
        ],
        out_specs=pl.BlockSpec((tq, w), lambda bi, i: (bi * nq + i, 0)),
        compiler_params=_cparams(("parallel", "arbitrary")),
        name="memory_attention",
    )(mq, mk, mv)


def _merge_kernel(x_ref, ya_ref, yb_ref, yc_ref, ga_ref, gb_ref, gc_ref, bg_ref,
                  wa_ref, wb_ref, wc_ref, wo_ref, fg_ref, x1_ref, h2_ref):
    def branch(y_ref, g_ref, w_ref, idx):
        gate = _sigmoid(g_ref[...].astype(F32) + bg_ref[idx:idx + 1, :])
        return gate * jnp.dot(y_ref[...], w_ref[...], preferred_element_type=F32)

    merged = branch(ya_ref, ga_ref, wa_ref, 0) + branch(yb_ref, gb_ref, wb_ref, 1) + branch(yc_ref, gc_ref, wc_ref, 2)
    x1 = x_ref[...] + jnp.dot(merged.astype(BF16), wo_ref[...], preferred_element_type=F32)
    x1_ref[...] = x1
    ms = jnp.mean(x1 * x1, axis=-1, keepdims=True)
    h2_ref[...] = (x1 * lax.rsqrt(ms + EPS) * fg_ref[...]).astype(h2_ref.dtype)


def merge_project(x2d, y_diff, y_delta, y_mem, rest, b_gate, w_a, w_b, w_c, w_o, ffn_g, *, gate_col0, tm):
    m, d = x2d.shape
    row = lambda i: (i, 0)
    const = lambda i: (0, 0)
    assert gate_col0 % d == 0

    def gspec(j):
        return pl.BlockSpec((tm, d), lambda i: (i, gate_col0 // d + j))

    return pl.pallas_call(
        _merge_kernel,
        out_shape=(jax.ShapeDtypeStruct((m, d), F32), jax.ShapeDtypeStruct((m, d), BF16)),
        grid=(m // tm,),
        in_specs=[
            pl.BlockSpec((tm, d), row), pl.BlockSpec((tm, d), row), pl.BlockSpec((tm, d), row),
            pl.BlockSpec((tm, d), row), gspec(0), gspec(1), gspec(2),
            pl.BlockSpec((3, d), const),
            pl.BlockSpec((d, d), const), pl.BlockSpec((d, d), const), pl.BlockSpec((d, d), const),
            pl.BlockSpec((d, d), const), pl.BlockSpec((1, d), const),
        ],
        out_specs=(pl.BlockSpec((tm, d), row), pl.BlockSpec((tm, d), row)),
        compiler_params=_cparams(("parallel",)),
        name="merge_project",
    )(x2d, y_diff, y_delta, y_mem, rest, rest, rest, b_gate.reshape(3, d).astype(F32),
      w_a, w_b, w_c, w_o, ffn_g.reshape(1, d).astype(F32))


def _topk_rows(s, extra, k):
    nrow = s.shape[0]
    rid = lax.broadcasted_iota(jnp.int32, s.shape, 0)
    vals, rows, ext = [], [], []
    for _ in range(k):
        m = jnp.max(s, axis=0, keepdims=True)
        am = jnp.min(jnp.where(s == m, rid, nrow), axis=0, keepdims=True)
        hit = rid == am
        if extra is not None:
            ext.append(jnp.max(jnp.where(hit, extra, -1), axis=0, keepdims=True))
        s = jnp.where(hit, -jnp.inf, s)
        vals.append(m)
        rows.append(am)
    cat = lambda xs: jnp.concatenate(xs, axis=0)
    return cat(vals), cat(rows), (cat(ext) if extra is not None else None)


def _peer_route_kernel(q_ref, keys_ref, idx_ref, gate_ref):
    k = PEER_TOPK
    q = q_ref[...]

    def half(p):
        qp = q[:, p * PEER_HALF:(p + 1) * PEER_HALF]
        st = lax.dot_general(keys_ref[0, p], qp, (((1,), (1,)), ((), ())), preferred_element_type=F32)
        v, r, _ = _topk_rows(st, None, k)
        return v, r

    s1, i1 = half(0)
    s2, i2 = half(1)
    cand_s = jnp.concatenate([s1[a:a + 1] + s2 for a in range(k)], axis=0)
    cand_e = jnp.concatenate([i1[a:a + 1] * PEER_KEYS + i2 for a in range(k)], axis=0)
    top_s, _, top_e = _topk_rows(cand_s, cand_e, k)
    e = jnp.exp(top_s - top_s[0:1])
    gate_ref[0] = e / jnp.sum(e, axis=0, keepdims=True)
    idx_ref[0] = top_e


def peer_route(qry, sub_keys, *, tt):
    m = qry.shape[0]
    nh = PEER_HEADS
    return pl.pallas_call(
        _peer_route_kernel,
        out_shape=(jax.ShapeDtypeStruct((nh, PEER_TOPK, m), jnp.int32),
                   jax.ShapeDtypeStruct((nh, PEER_TOPK, m), F32)),
        grid=(m // tt, nh),
        in_specs=[
            pl.BlockSpec((tt, 2 * PEER_HALF), lambda i, h: (i, h)),
            pl.BlockSpec((1, 2, PEER_KEYS, PEER_HALF), lambda i, h: (h, 0, 0, 0)),
        ],
        out_specs=(pl.BlockSpec((1, PEER_TOPK, tt), lambda i, h: (h, 0, i)),
                   pl.BlockSpec((1, PEER_TOPK, tt), lambda i, h: (h, 0, i))),
        compiler_params=_cparams(("parallel", "arbitrary")),
        name="peer_route",
    )(qry, sub_keys)


def _gelu_exact(x):
    return 0.5 * x * (1.0 + lax.erf(x * (2.0 ** -0.5)))


def _peer_expert_kernel(idx_ref, idxn_ref, gate_ref, h_ref, x_ref, pool_ref, poolt_ref, uv_hbm,
                        o_ref, uvbuf, sem, *, tb, ne):
    i = pl.program_id(0)
    n = pl.num_programs(0)
    slot = i % 2
    rows = tb * ne
    d_sub = h_ref.shape[1]

    def row_copy(e, s, r, k):
        return pltpu.make_async_copy(uv_hbm.at[e], uvbuf.at[s, r], sem.at[s, k % 2])

    def issue(iref, s):
        def tok(t, carry):
            for j in range(ne):
                row_copy(iref[t, j], s, t * ne + j, j).start(priority=j % 2)
            return carry
        lax.fori_loop(0, tb, tok, 0)

    @pl.when(i == 0)
    def _():
        issue(idx_ref, 0)

    @pl.when(i + 1 < n)
    def _():
        issue(idxn_ref, 1 - slot)

    for k in range(2):
        pltpu.make_async_copy(uv_hbm.at[pl.ds(0, rows // 2)], uvbuf.at[slot, pl.ds(0, rows // 2)],
                              sem.at[slot, k]).wait()

    sub_id = lax.broadcasted_iota(jnp.int32, (d_sub, ne * d_sub), 0)
    col_id = lax.broadcasted_iota(jnp.int32, (d_sub, ne * d_sub), 1)
    diag = (col_id % d_sub) == sub_id

    drows = []
    for t in range(tb):
        zu = uvbuf[slot, pl.ds(t * ne, ne), 0:d_sub, :].reshape(ne * d_sub, LANES).astype(BF16)
        c = lax.dot_general(h_ref[t], zu, (((1,), (1,)), ((), ())), preferred_element_type=F32)
        drows.append(jnp.sum(jnp.where(diag, c, 0.0), axis=0, keepdims=True))
    dall = jnp.concatenate(drows, axis=0)
    d_hi = dall.astype(BF16)
    d_lo = (dall - d_hi.astype(F32)).astype(BF16)
    a = (jnp.dot(d_hi, pool_ref[...], preferred_element_type=F32)
         + jnp.dot(d_lo, pool_ref[...], preferred_element_type=F32))
    w = gate_ref[...] * _gelu_exact(a)
    wrep = jnp.dot(w.astype(BF16), poolt_ref[...], preferred_element_type=F32)

    for t in range(tb):
        wexp = jnp.where(diag, wrep[t:t + 1, :], 0.0).astype(BF16)
        zv = uvbuf[slot, pl.ds(t * ne, ne), d_sub:2 * d_sub, :].reshape(ne * d_sub, LANES).astype(BF16)
        o_ref[t] = x_ref[t] + jnp.dot(wexp, zv, preferred_element_type=F32)


def peer_experts(idx, gates, h3, x3, uv3, *, tb):
    m, ne = idx.shape
    d_sub = h3.shape[1]
    n = m // tb
    rows = tb * ne
    cid = jnp.arange(ne * d_sub) // d_sub
    pool = (cid[:, None] == jnp.arange(ne)[None, :]).astype(BF16)
    tok = lambda i: (i, 0)
    tok3 = lambda i: (i, 0, 0)
    return pl.pallas_call(
        functools.partial(_peer_expert_kernel, tb=tb, ne=ne),
        out_shape=jax.ShapeDtypeStruct(x3.shape, F32),
        grid=(n,),
        in_specs=[
            pl.BlockSpec((tb, ne), tok, memory_space=pltpu.SMEM),
            pl.BlockSpec((tb, ne), lambda i: (jnp.minimum(i + 1, n - 1), 0), memory_space=pltpu.SMEM),
            pl.BlockSpec((tb, ne), tok),
            pl.BlockSpec((tb, d_sub, LANES), tok3),
            pl.BlockSpec((tb, d_sub, LANES), tok3),
            pl.BlockSpec((ne * d_sub, ne), lambda i: (0, 0)),
            pl.BlockSpec((ne, ne * d_sub), lambda i: (0, 0)),
            pl.BlockSpec(memory_space=pl.ANY),
        ],
        out_specs=pl.BlockSpec((tb, d_sub, LANES), tok3),
        scratch_shapes=[
            pltpu.VMEM((2, rows, 2 * d_sub, LANES), F32),
            pltpu.SemaphoreType.DMA((2, 2)),
        ],
        compiler_params=_cparams(("arbitrary",)),
        name="peer_experts",
    )(idx, idx, gates, h3, x3, pool, pool.T, uv3)


def kernel(x, mem, positions, attn_norm_g, mem_norm_g, w_in, b_gate, diff_q_norm_g, diff_k_norm_g, lambda_q1, lambda_k1, lambda_q2, lambda_k2, diff_subln_g, rel_bias_table, conv_w, a_log, dt_bias, delta_out_norm_g, w_mem_kv, mem_q_norm_g, mem_k_norm_g, w_br_diff, w_br_delta, w_br_mem, w_out, ffn_norm_g, w_query, sub_keys, expert_u, expert_v):
    del positions
    b, t, d = x.shape
    mtok = mem.shape[1]
    m = b * t
    depth = w_in.shape[0]
    nh = N_DIFF_HEADS
    qk_w = nh * 2 * DIFF_HEAD_DIM
    dv_w = nh * 2 * DIFF_HEAD_DIM
    dl_w = N_DELTA_HEADS * DELTA_HEAD_DIM
    mem_w = N_MEM_HEADS * MEM_HEAD_DIM
    tq = min(256, t)
    tt_delta = min(256, t)

    x2d = x.reshape(m, d)
    for l in range(depth):
        lam_init = 0.8 - 0.6 * math.exp(-0.3 * l)
        wl = w_in[l]
        o = 0
        w_qk = wl[:, o:o + 2 * qk_w]; o += 2 * qk_w
        w_dv = wl[:, o:o + dv_w]; o += dv_w
        w_lqkv = wl[:, o:o + 3 * dl_w]; o += 3 * dl_w
        w_lz = wl[:, o:o + dl_w]; o += dl_w
        w_ab = wl[:, o:o + 2 * N_DELTA_HEADS]; o += 2 * N_DELTA_HEADS
        w_mq = wl[:, o:o + mem_w]; o += mem_w
        w_gate = wl[:, o:o + 3 * d]

        qk_gain = jnp.concatenate([jnp.tile(diff_q_norm_g[l], 2 * nh) * (DIFF_HEAD_DIM ** -0.5),
                                   jnp.tile(diff_k_norm_g[l], 2 * nh)])
        qk = norm_matmul(x2d, w_qk.astype(BF16), norm_g=attn_norm_g[l], group=DIFF_HEAD_DIM, group_gain=qk_gain)
        w_rest = jnp.concatenate([w_dv, w_lqkv, w_lz, w_gate], axis=1).astype(BF16)
        rest = norm_matmul(x2d, w_rest, norm_g=attn_norm_g[l])
        mq_gain = jnp.tile(mem_q_norm_g[l], N_MEM_HEADS) * (MEM_HEAD_DIM ** -0.5)
        mq = norm_matmul(x2d, w_mq.astype(BF16), norm_g=attn_norm_g[l], group=MEM_HEAD_DIM, group_gain=mq_gain)
        w_ab_pad = jnp.pad(w_ab, ((0, 0), (0, LANES - 2 * N_DELTA_HEADS))).astype(BF16)
        ab = norm_matmul(x2d, w_ab_pad, norm_g=attn_norm_g[l], out_dtype=F32, tn=LANES)
        abt = ab[:, :2 * N_DELTA_HEADS].T

        lam = (jnp.exp(jnp.sum(lambda_q1[l].astype(F32) * lambda_k1[l].astype(F32)))
               - jnp.exp(jnp.sum(lambda_q2[l].astype(F32) * lambda_k2[l].astype(F32))) + lam_init)
        bias_tiles = rel_bias_tiles(rel_bias_table, tq)
        scal = jnp.concatenate([jnp.stack([lam, jnp.asarray(1.0 - lam_init, F32)]),
                                rel_bias_table[REL_BUCKETS - 1].astype(F32)])
        y_diff = diff_attention(qk, rest, bias_tiles, scal, diff_subln_g[l], b=b, t=t, tq=tq, v_col0=0)

        head_params = jnp.stack([a_log[l], dt_bias[l]]).astype(F32)
        y_delta = gated_deltanet(rest, ab, abt, conv_w[l].astype(F32), head_params, delta_out_norm_g[l],
                                 b=b, t=t, tt=tt_delta, qkv_col0=dv_w // LANES,
                                 z_col0=(dv_w + 3 * dl_w) // LANES)

        mem2d = mem.reshape(b * mtok, d)
        wkv = w_mem_kv[l].astype(BF16)
        mk = norm_matmul(mem2d, wkv[:, :mem_w], norm_g=mem_norm_g[l], group=MEM_HEAD_DIM,
                         group_gain=jnp.tile(mem_k_norm_g[l], N_MEM_HEADS))
        mv = norm_matmul(mem2d, wkv[:, mem_w:], norm_g=mem_norm_g[l])
        y_mem = memory_attention(mq, mk, mv, b=b, t=t, mtok=mtok, tq=min(512, t))

        x1, h2 = merge_project(x2d, y_diff, y_delta, y_mem, rest, b_gate[l],
                               w_br_diff[l].astype(BF16), w_br_delta[l].astype(BF16), w_br_mem[l].astype(BF16),
                               w_out[l].astype(BF16), ffn_norm_g[l],
                               gate_col0=dv_w + 4 * dl_w, tm=min(512, m))

        qry = norm_matmul(h2, w_query[l].astype(BF16))
        idx_t, gate_t = peer_route(qry, sub_keys[l].astype(BF16), tt=min(512, m))
        ne = PEER_HEADS * PEER_TOPK
        idx = idx_t.reshape(ne, m).T
        gates = gate_t.reshape(ne, m).T
        sub = d // LANES
        uv3 = jnp.concatenate([expert_u[l].reshape(-1, sub, LANES), expert_v[l].reshape(-1, sub, LANES)], axis=1)
        x2 = peer_experts(idx, gates, h2.reshape(m, sub, LANES), x1.reshape(m, sub, LANES), uv3, tb=8)
        x2d = x2.reshape(m, d)
    return x2d.reshape(b, t, d)
```

```python
import functools
import math

import jax
import jax.numpy as jnp
from jax import lax
from jax.experimental import pallas as pl
from jax.experimental.pallas import tpu as pltpu

F32 = jnp.float32
BF16 = jnp.bfloat16
EPS = 1e-6
NEG = -1e30

N_DIFF_HEADS = 8
DIFF_HEAD_DIM = 64
N_DELTA_HEADS = 8
DELTA_HEAD_DIM = 128
CONV_WIDTH = 4
CHUNK = 64
N_MEM_HEADS = 4
MEM_HEAD_DIM = 256
REL_BUCKETS = 32
REL_MAX_DIST = 128
PEER_HEADS = 8
PEER_KEYS = 128
PEER_TOPK = 16
PEER_HALF = 128
LANES = 128
SUBLANES = 8
VMEM_LIMIT = 56 * 1024 * 1024


def _cparams(sem):
    return pltpu.CompilerParams(dimension_semantics=sem, vmem_limit_bytes=VMEM_LIMIT)


def _norm_matmul_kernel(*refs, has_norm, has_group):
    it = iter(refs)
    x_ref = next(it)
    g_ref = next(it) if has_norm else None
    w_ref = next(it)
    gm_ref = next(it) if has_group else None
    gain_ref = next(it) if has_group else None
    o_ref = next(it)
    h_ref = next(it)

    @pl.when(pl.program_id(1) == 0)
    def _():
        x = x_ref[...].astype(F32)
        if has_norm:
            ms = jnp.mean(x * x, axis=-1, keepdims=True)
            x = x * lax.rsqrt(ms + EPS) * g_ref[...]
        h_ref[...] = x.astype(BF16)

    y = jnp.dot(h_ref[...], w_ref[...], preferred_element_type=F32)
    if has_group:
        ms = jnp.dot((y * y).astype(BF16), gm_ref[...], preferred_element_type=F32)
        y = y * lax.rsqrt(ms + EPS) * gain_ref[...]
    o_ref[...] = y.astype(o_ref.dtype)


def norm_matmul(x, w, *, norm_g=None, group=None, group_gain=None, out_dtype=BF16, tm=512, tn=None):
    m, k = x.shape
    n = w.shape[1]
    has_norm = norm_g is not None
    has_group = group is not None
    if tn is None:
        tn = 256 if has_group else min(n, 1024)
    tm = min(tm, m)
    assert m % tm == 0 and n % tn == 0
    in_specs = [pl.BlockSpec((tm, k), lambda i, j: (i, 0))]
    args = [x]
    if has_norm:
        in_specs.append(pl.BlockSpec((1, k), lambda i, j: (0, 0)))
        args.append(norm_g.reshape(1, k).astype(F32))
    in_specs.append(pl.BlockSpec((k, tn), lambda i, j: (0, j)))
    args.append(w)
    if has_group:
        gid = jnp.arange(tn) // group
        gm = jnp.where(gid[:, None] == gid[None, :], 1.0 / group, 0.0).astype(BF16)
        in_specs.append(pl.BlockSpec((tn, tn), lambda i, j: (0, 0)))
        args.append(gm)
        in_specs.append(pl.BlockSpec((1, tn), lambda i, j: (0, j)))
        args.append(group_gain.reshape(1, n).astype(F32))
    return pl.pallas_call(
        functools.partial(_norm_matmul_kernel, has_norm=has_norm, has_group=has_group),
        out_shape=jax.ShapeDtypeStruct((m, n), out_dtype),
        grid=(m // tm, n // tn),
        in_specs=in_specs,
        out_specs=pl.BlockSpec((tm, tn), lambda i, j: (i, j)),
        scratch_shapes=[pltpu.VMEM((tm, k), BF16)],
        compiler_params=_cparams(("parallel", "arbitrary")),
        name="norm_matmul",
    )(*args)


def _t5_bucket(n):
    max_exact = REL_BUCKETS // 2
    nf = jnp.maximum(n, 1).astype(F32)
    large = max_exact + (jnp.log(nf / max_exact) / math.log(REL_MAX_DIST / max_exact)
                         * (REL_BUCKETS - max_exact)).astype(jnp.int32)
    large = jnp.minimum(large, REL_BUCKETS - 1)
    return jnp.where(n < max_exact, n, large)


def _rel_bias_kernel(table_ref, o_ref, *, tq):
    hm = pl.program_id(0)
    r = lax.broadcasted_iota(jnp.int32, (tq, 2 * tq), 0)
    c = lax.broadcasted_iota(jnp.int32, (tq, 2 * tq), 1)
    bucket = _t5_bucket(jnp.maximum(r - c + tq, 0))
    acc = jnp.zeros((tq, 2 * tq), F32)
    for b in range(REL_BUCKETS):
        acc = jnp.where(bucket == b, table_ref[b, hm], acc)
    o_ref[0] = acc


def rel_bias_tiles(rel_table, tq):
    nmaps = rel_table.shape[1]
    return pl.pallas_call(
        functools.partial(_rel_bias_kernel, tq=tq),
        out_shape=jax.ShapeDtypeStruct((nmaps, tq, 2 * tq), F32),
        grid=(nmaps,),
        in_specs=[pl.BlockSpec(memory_space=pltpu.SMEM)],
        out_specs=pl.BlockSpec((1, tq, 2 * tq), lambda i: (i, 0, 0)),
        compiler_params=_cparams(("arbitrary",)),
        name="rel_bias_tiles",
    )(rel_table.astype(F32))


def _diff_attn_kernel(scal_ref, q_ref, k_ref, v_ref, bias_ref, subg_ref, o_ref, *, tq, far_bucket_dist):
    del far_bucket_dist
    h = pl.program_id(1)
    i = pl.program_id(2)
    lam = scal_ref[0]
    out_scale = scal_ref[1]
    c1 = scal_ref[2 + 2 * h]
    c2 = scal_ref[3 + 2 * h]

    q = q_ref[...]
    lane = lax.broadcasted_iota(jnp.int32, (tq, LANES), 1)
    zero = jnp.zeros_like(q)
    qq = jnp.concatenate([jnp.where(lane < DIFF_HEAD_DIM, q, zero),
                          jnp.where(lane >= DIFF_HEAD_DIM, q, zero)], axis=0)
    row2 = lax.broadcasted_iota(jnp.int32, (2 * tq, 1), 0)
    cfar = jnp.where(row2 < tq, c1, c2)

    def update(carry, s, vj):
        m, l, acc = carry
        m_new = jnp.maximum(m, jnp.max(s, axis=-1, keepdims=True))
        alpha = jnp.exp(m - m_new)
        p = jnp.exp(s - m_new)
        l = l * alpha + jnp.sum(p, axis=-1, keepdims=True)
        acc = acc * alpha + jnp.dot(p.astype(BF16), vj, preferred_element_type=F32)
        return m_new, l, acc

    def scores(start, width):
        kj = k_ref[pl.ds(pl.multiple_of(start, tq), width), :]
        vj = v_ref[pl.ds(pl.multiple_of(start, tq), width), :]
        s = lax.dot_general(qq, kj, (((1,), (1,)), ((), ())), preferred_element_type=F32)
        return s, vj

    def far_step(j, carry):
        s, vj = scores(j * (2 * tq), 2 * tq)
        return update(carry, s + cfar, vj)

    def far_single(carry):
        s, vj = scores((i - 2) * tq, tq)
        return update(carry, s + cfar, vj)

    n_far = jnp.maximum(i - 1, 0)
    init = (jnp.full((2 * tq, 1), NEG, F32), jnp.zeros((2 * tq, 1), F32), jnp.zeros((2 * tq, LANES), F32))
    carry = lax.fori_loop(0, n_far // 2, far_step, init)
    carry = lax.cond(n_far % 2 == 1, far_single, lambda c: c, carry)

    b_prev = jnp.concatenate([bias_ref[0, :, 0:tq], bias_ref[1, :, 0:tq]], axis=0)
    s, vj = scores(jnp.maximum(i - 1, 0) * tq, tq)
    s = jnp.where(i >= 1, s + b_prev, NEG)
    carry = update(carry, s, vj)

    b_diag = jnp.concatenate([bias_ref[0, :, tq:2 * tq], bias_ref[1, :, tq:2 * tq]], axis=0)
    s, vj = scores(i * tq, tq)
    rq = lax.broadcasted_iota(jnp.int32, (2 * tq, tq), 0)
    rq = jnp.where(rq >= tq, rq - tq, rq)
    ck = lax.broadcasted_iota(jnp.int32, (2 * tq, tq), 1)
    s = jnp.where(ck <= rq, s + b_diag, NEG)
    m, l, acc = update(carry, s, vj)

    o = acc / l
    o = o[0:tq] - lam * o[tq:2 * tq]
    ms = jnp.mean(o * o, axis=-1, keepdims=True)
    o = o * lax.rsqrt(ms + EPS) * subg_ref[...] * out_scale
    o_ref[...] = o.astype(o_ref.dtype)


def diff_attention(qk, rest, bias_tiles, scal, subln_g, *, b, t, tq, v_col0):
    m = b * t
    nq = t // tq
    nh = N_DIFF_HEADS
    return pl.pallas_call(
        functools.partial(_diff_attn_kernel, tq=tq, far_bucket_dist=None),
        out_shape=jax.ShapeDtypeStruct((m, nh * LANES), BF16),
        grid=(b, nh, nq),
        in_specs=[
            pl.BlockSpec(memory_space=pltpu.SMEM),
            pl.BlockSpec((tq, LANES), lambda bi, h, i: (bi * nq + i, h)),
            pl.BlockSpec((t, LANES), lambda bi, h, i: (bi, nh + h)),
            pl.BlockSpec((t, LANES), lambda bi, h, i: (bi, v_col0 + h)),
            pl.BlockSpec((2, tq, 2 * tq), lambda bi, h, i: (h, 0, 0)),
            pl.BlockSpec((1, LANES), lambda bi, h, i: (0, 0)),
        ],
        out_specs=pl.BlockSpec((tq, LANES), lambda bi, h, i: (bi * nq + i, h)),
        compiler_params=_cparams(("parallel", "parallel", "arbitrary")),
        name="diff_attention",
    )(scal, qk, qk, rest, bias_tiles, subln_g.reshape(1, LANES).astype(F32))


def _sigmoid(x):
    return 1.0 / (1.0 + jnp.exp(-x))


def _softplus(x):
    return jnp.maximum(x, 0.0) + jnp.log(1.0 + jnp.exp(-jnp.abs(x)))


def _delta_kernel(hp_ref, xq_ref, xk_ref, xv_ref, pq_ref, pk_ref, pv_ref, wq_ref, wk_ref, wv_ref,
                  ab_ref, abt_ref, z_ref, og_ref, o_ref, s_ref, *, tt):
    h = pl.program_id(1)
    i = pl.program_id(2)
    nc = tt // CHUNK
    a_log = hp_ref[0, h]
    dt_bias = hp_ref[1, h]

    @pl.when(i == 0)
    def _():
        s_ref[...] = jnp.zeros_like(s_ref)

    def conv_silu(x_ref, p_ref, w_ref):
        prev = jnp.where(i > 0, p_ref[...].astype(F32), 0.0)
        xf = jnp.concatenate([prev, x_ref[...].astype(F32)], axis=0)
        w = w_ref[...]
        base = SUBLANES - (CONV_WIDTH - 1)
        y = xf[base:base + tt] * w[0:1]
        for c in range(1, CONV_WIDTH):
            y = y + xf[base + c:base + c + tt] * w[c:c + 1]
        return y * _sigmoid(y)

    q = conv_silu(xq_ref, pq_ref, wq_ref)
    k = conv_silu(xk_ref, pk_ref, wk_ref)
    v = conv_silu(xv_ref, pv_ref, wv_ref)
    q = q * lax.rsqrt(jnp.sum(q * q, axis=-1, keepdims=True) + EPS) * (DELTA_HEAD_DIM ** -0.5)
    k = k * lax.rsqrt(jnp.sum(k * k, axis=-1, keepdims=True) + EPS)

    ab = ab_ref[...]
    lane = lax.broadcasted_iota(jnp.int32, ab.shape, 1)
    lb_col = jnp.sum(jnp.where(lane == h, ab, 0.0), axis=-1, keepdims=True)
    la_col = jnp.sum(jnp.where(lane == N_DELTA_HEADS + h, ab, 0.0), axis=-1, keepdims=True)
    la_row = abt_ref[pl.ds(N_DELTA_HEADS + h, 1), :]
    neg_a = -jnp.exp(jnp.full((1, 1), a_log, F32))
    beta_col = _sigmoid(lb_col)
    g_col = neg_a * _softplus(la_col + dt_bias)
    g_row = neg_a * _softplus(la_row + dt_bias)

    ri = lax.broadcasted_iota(jnp.int32, (CHUNK, CHUNK), 0)
    ci = lax.broadcasted_iota(jnp.int32, (CHUNK, CHUNK), 1)
    tril = ri >= ci
    strict = ri > ci

    def mm(a, b):
        return jnp.dot(a.astype(BF16), b.astype(BF16), preferred_element_type=F32)

    def mm_nt(a, b):
        return lax.dot_general(a.astype(BF16), b.astype(BF16), (((1,), (1,)), ((), ())),
                               preferred_element_type=F32)

    def mm_tn(a, b):
        return lax.dot_general(a.astype(BF16), b.astype(BF16), (((0,), (0,)), ((), ())),
                               preferred_element_type=F32)

    state = s_ref[...]
    outs = []
    for c in range(nc):
        r0 = c * CHUNK
        qc, kc, vc = q[r0:r0 + CHUNK], k[r0:r0 + CHUNK], v[r0:r0 + CHUNK]
        bc = beta_col[r0:r0 + CHUNK]
        gcol = g_col[r0:r0 + CHUNK]
        grow = g_row[:, r0:r0 + CHUNK]
        gc_col = jnp.sum(jnp.where(tril, grow, 0.0), axis=-1, keepdims=True)
        gc_row = jnp.sum(jnp.where(ri <= ci, gcol, 0.0), axis=0, keepdims=True)
        g_last = jnp.sum(grow, axis=-1, keepdims=True)
        decay = jnp.where(tril, jnp.exp(jnp.where(tril, gc_col - gc_row, 0.0)), 0.0)
        kk = mm_nt(kc, kc)
        x = jnp.where(strict, -(bc * kk * decay), 0.0)
        p = x
        xp = x
        for _ in range(5):
            xp = mm(xp, xp)
            p = p + xp + mm(p, xp)
        rhs_v = bc * vc
        rhs_k = (bc * jnp.exp(gc_col)) * kc
        u = rhs_v + mm(p, rhs_v)
        w = rhs_k + mm(p, rhs_k)
        intra = jnp.where(tril, mm_nt(qc, kc) * decay, 0.0)
        v_new = u - mm(w, state)
        o = mm(qc * jnp.exp(gc_col), state) + mm(intra, v_new)
        k_dec = kc * jnp.exp(g_last - gc_col)
        state = state * jnp.exp(g_last) + mm_tn(k_dec, v_new)
        outs.append(o)
    s_ref[...] = state

    o = jnp.concatenate(outs, axis=0)
    ms = jnp.mean(o * o, axis=-1, keepdims=True)
    z = z_ref[...].astype(F32)
    o = o * lax.rsqrt(ms + EPS) * og_ref[...] * (z * _sigmoid(z))
    o_ref[...] = o.astype(o_ref.dtype)


def gated_deltanet(rest, ab, abt, conv_w, head_params, out_g, *, b, t, tt, qkv_col0, z_col0):
    m = b * t
    nt = t // tt
    nh = N_DELTA_HEADS
    sub = tt // SUBLANES

    def cur(off):
        return pl.BlockSpec((tt, LANES), lambda bi, h, i: (bi * nt + i, off + h))

    def prev(off):
        return pl.BlockSpec((SUBLANES, LANES),
                            lambda bi, h, i: (jnp.maximum((bi * nt + i) * sub - 1, 0), off + h))

    def wspec(off):
        return pl.BlockSpec((CONV_WIDTH, LANES), lambda bi, h, i: (0, off + h))

    return pl.pallas_call(
        functools.partial(_delta_kernel, tt=tt),
        out_shape=jax.ShapeDtypeStruct((m, nh * LANES), BF16),
        grid=(b, nh, nt),
        in_specs=[
            pl.BlockSpec(memory_space=pltpu.SMEM),
            cur(qkv_col0), cur(qkv_col0 + nh), cur(qkv_col0 + 2 * nh),
            prev(qkv_col0), prev(qkv_col0 + nh), prev(qkv_col0 + 2 * nh),
            wspec(0), wspec(nh), wspec(2 * nh),
            pl.BlockSpec((tt, LANES), lambda bi, h, i: (bi * nt + i, 0)),
            pl.BlockSpec((2 * nh, tt), lambda bi, h, i: (0, bi * nt + i)),
            cur(z_col0),
            pl.BlockSpec((1, LANES), lambda bi, h, i: (0, 0)),
        ],
        out_specs=pl.BlockSpec((tt, LANES), lambda bi, h, i: (bi * nt + i, h)),
        scratch_shapes=[pltpu.VMEM((DELTA_HEAD_DIM, DELTA_HEAD_DIM), F32)],
        compiler_params=_cparams(("parallel", "parallel", "arbitrary")),
        name="gated_deltanet",
    )(head_params, rest, rest, rest, rest, rest, rest, conv_w, conv_w, conv_w, ab, abt, rest,
      out_g.reshape(1, LANES).astype(F32))


def _mem_attn_kernel(q_ref, k_ref, v_ref, o_ref):
    outs = []
    for h in range(N_MEM_HEADS):
        sl = slice(h * MEM_HEAD_DIM, (h + 1) * MEM_HEAD_DIM)
        s = lax.dot_general(q_ref[:, sl], k_ref[:, sl], (((1,), (1,)), ((), ())), preferred_element_type=F32)
        s = s - jnp.max(s, axis=-1, keepdims=True)
        p = jnp.exp(s)
        p = p / jnp.sum(p, axis=-1, keepdims=True)
        outs.append(jnp.dot(p.astype(BF16), v_ref[:, sl], preferred_element_type=F32))
    o_ref[...] = jnp.concatenate(outs, axis=-1).astype(o_ref.dtype)


def memory_attention(mq, mk, mv, *, b, t, mtok, tq):
    nq = t // tq
    w = N_MEM_HEADS * MEM_HEAD_DIM
    return pl.pallas_call(
        _mem_attn_kernel,
        out_shape=jax.ShapeDtypeStruct((b * t, w), BF16),
        grid=(b, nq),
        in_specs=[
            pl.BlockSpec((tq, w), lambda bi, i: (bi * nq + i, 0)),
            pl.BlockSpec((mtok, w), lambda bi, i: (bi, 0)),
            pl.BlockSpec((mtok, w), lambda bi, i: (bi, 0)),
        ],
        out_specs=pl.BlockSpec((tq, w), lambda bi, i: (bi * nq + i, 0)),
        compiler_params=_cparams(("parallel", "arbitrary")),
        name="memory_attention",
    )(mq, mk, mv)


def _merge_kernel(x_ref, ya_ref, yb_ref, yc_ref, ga_ref, gb_ref, gc_ref, bg_ref,
                  wa_ref, wb_ref, wc_ref, wo_ref, fg_ref, x1_ref, h2_ref):
    def branch(y_ref, g_ref, w_ref, idx):
        gate = _sigmoid(g_ref[...].astype(F32) + bg_ref[idx:idx + 1, :])
        return gate * jnp.dot(y_ref[...], w_ref[...], preferred_element_type=F32)

    merged = branch(ya_ref, ga_ref, wa_ref, 0) + branch(yb_ref, gb_ref, wb_ref, 1) + branch(yc_ref, gc_ref, wc_ref, 2)
    x1 = x_ref[...] + jnp.dot(merged.astype(BF16), wo_ref[...], preferred_element_type=F32)
    x1_ref[...] = x1
    ms = jnp.mean(x1 * x1, axis=-1, keepdims=True)
    h2_ref[...] = (x1 * lax.rsqrt(ms + EPS) * fg_ref[...]).astype(h2_ref.dtype)


def merge_project(x2d, y_diff, y_delta, y_mem, rest, b_gate, w_a, w_b, w_c, w_o, ffn_g, *, gate_col0, tm):
    m, d = x2d.shape
    row = lambda i: (i, 0)
    const = lambda i: (0, 0)
    assert gate_col0 % d == 0

    def gspec(j):
        return pl.BlockSpec((tm, d), lambda i: (i, gate_col0 // d + j))

    return pl.pallas_call(
        _merge_kernel,
        out_shape=(jax.ShapeDtypeStruct((m, d), F32), jax.ShapeDtypeStruct((m, d), BF16)),
        grid=(m // tm,),
        in_specs=[
            pl.BlockSpec((tm, d), row), pl.BlockSpec((tm, d), row), pl.BlockSpec((tm, d), row),
            pl.BlockSpec((tm, d), row), gspec(0), gspec(1), gspec(2),
            pl.BlockSpec((3, d), const),
            pl.BlockSpec((d, d), const), pl.BlockSpec((d, d), const), pl.BlockSpec((d, d), const),
            pl.BlockSpec((d, d), const), pl.BlockSpec((1, d), const),
        ],
        out_specs=(pl.BlockSpec((tm, d), row), pl.BlockSpec((tm, d), row)),
        compiler_params=_cparams(("parallel",)),
        name="merge_project",
    )(x2d, y_diff, y_delta, y_mem, rest, rest, rest, b_gate.reshape(3, d).astype(F32),
      w_a, w_b, w_c, w_o, ffn_g.reshape(1, d).astype(F32))


def _topk_rows(s, key, extra, k):
    big = jnp.iinfo(jnp.int32).max
    vals, keys, ext = [], [], []
    for _ in range(k):
        m = jnp.max(s, axis=0, keepdims=True)
        am = jnp.min(jnp.where(s == m, key, big), axis=0, keepdims=True)
        hit = key == am
        if extra is not None:
            ext.append(jnp.max(jnp.where(hit, extra, -1), axis=0, keepdims=True))
        s = jnp.where(hit, -jnp.inf, s)
        vals.append(m)
        keys.append(am)
    cat = lambda xs: jnp.concatenate(xs, axis=0)
    return cat(vals), cat(keys), (cat(ext) if extra is not None else None)


def _peer_route_kernel(q_ref, keys_ref, idx_ref, gate_ref):
    k = PEER_TOPK
    q = q_ref[...]
    tt = q.shape[0]

    def half(p):
        qp = q[:, p * PEER_HALF:(p + 1) * PEER_HALF]
        st = lax.dot_general(keys_ref[0, p], qp, (((1,), (1,)), ((), ())), preferred_element_type=F32)
        v, r, _ = _topk_rows(st, lax.broadcasted_iota(jnp.int32, st.shape, 0), None, k)
        return v, r

    s1, i1 = half(0)
    s2, i2 = half(1)
    g = SUBLANES
    sub = lax.broadcasted_iota(jnp.int32, (g, tt), 0)
    e1 = i1 * PEER_KEYS
    pieces = []
    for b in range(g):
        lim = min(g, k // (b + 1))
        sc = s1[0:g] + s2[b:b + 1]
        if lim < g:
            sc = jnp.where(sub < lim, sc, -jnp.inf)
        pieces.append((sc, sub * k + b, e1[0:g] + i2[b:b + 1]))
    pieces.append((s1[g:k] + s2[0:1], (sub + g) * k, e1[g:k] + i2[0:1]))
    pieces.append((s1[0:1] + s2[g:k], sub + g, e1[0:1] + i2[g:k]))
    cand_s = jnp.concatenate([p[0] for p in pieces], axis=0)
    cand_k = jnp.concatenate([p[1] for p in pieces], axis=0)
    cand_e = jnp.concatenate([p[2] for p in pieces], axis=0)
    top_s, _, top_e = _topk_rows(cand_s, cand_k, cand_e, k)
    e = jnp.exp(top_s - top_s[0:1])
    gate_ref[0] = e / jnp.sum(e, axis=0, keepdims=True)
    idx_ref[0] = top_e


def peer_route(qry, sub_keys, *, tt):
    m = qry.shape[0]
    nh = PEER_HEADS
    return pl.pallas_call(
        _peer_route_kernel,
        out_shape=(jax.ShapeDtypeStruct((nh, PEER_TOPK, m), jnp.int32),
                   jax.ShapeDtypeStruct((nh, PEER_TOPK, m), F32)),
        grid=(m // tt, nh),
        in_specs=[
            pl.BlockSpec((tt, 2 * PEER_HALF), lambda i, h: (i, h)),
            pl.BlockSpec((1, 2, PEER_KEYS, PEER_HALF), lambda i, h: (h, 0, 0, 0)),
        ],
        out_specs=(pl.BlockSpec((1, PEER_TOPK, tt), lambda i, h: (h, 0, i)),
                   pl.BlockSpec((1, PEER_TOPK, tt), lambda i, h: (h, 0, i))),
        compiler_params=_cparams(("parallel", "arbitrary")),
        name="peer_route",
    )(qry, sub_keys)


def _gelu_exact(x):
    return 0.5 * x * (1.0 + lax.erf(x * (2.0 ** -0.5)))


def _peer_expert_kernel(idx_ref, idxn_ref, gate_ref, h_ref, x_ref, pool_ref, poolt_ref, uv_hbm,
                        o_ref, buf_a, buf_b, sem, *, tb, ne):
    i = pl.program_id(0)
    n = pl.num_programs(0)
    rows = tb * ne
    d_sub = h_ref.shape[1]
    n_seg = 2 * tb
    seg = rows // n_seg

    def issue_rows(iref, row0, buf, which, r_lo, r_hi):
        for r in range(r_lo, r_hi):
            e = iref[row0 + r // ne, r % ne]
            pltpu.make_async_copy(uv_hbm.at[e], buf.at[r], sem.at[which, r % 2]).start(priority=r % 2)

    def wait_tile(buf, which):
        for k in range(2):
            pltpu.make_async_copy(uv_hbm.at[pl.ds(0, rows // 2)], buf.at[pl.ds(0, rows // 2)],
                                  sem.at[which, k]).wait()

    @pl.when(i == 0)
    def _():
        def tok(t, carry):
            for j in range(ne):
                pltpu.make_async_copy(uv_hbm.at[idx_ref[t, j]], buf_a.at[t * ne + j],
                                      sem.at[0, j % 2]).start(priority=j % 2)
            return carry
        lax.fori_loop(0, tb, tok, 0)

    sub_id = lax.broadcasted_iota(jnp.int32, (d_sub, ne * d_sub), 0)
    col_id = lax.broadcasted_iota(jnp.int32, (d_sub, ne * d_sub), 1)
    diag = (col_id % d_sub) == sub_id

    def tile(row0, buf, nxt_iref, nxt_row0, nxt_buf, nxt_which):
        drows = []
        for t in range(tb):
            zu = buf[pl.ds(t * ne, ne), 0:d_sub, :].reshape(ne * d_sub, LANES).astype(BF16)
            c = lax.dot_general(h_ref[row0 + t], zu, (((1,), (1,)), ((), ())), preferred_element_type=F32)
            drows.append(jnp.sum(jnp.where(diag, c, 0.0), axis=0, keepdims=True))
            issue_rows(nxt_iref, nxt_row0, nxt_buf, nxt_which, t * seg, (t + 1) * seg)
        dall = jnp.concatenate(drows, axis=0)
        d_hi = dall.astype(BF16)
        d_lo = (dall - d_hi.astype(F32)).astype(BF16)
        a = (jnp.dot(d_hi, pool_ref[...], preferred_element_type=F32)
             + jnp.dot(d_lo, pool_ref[...], preferred_element_type=F32))
        w = gate_ref[pl.ds(row0, tb), :] * _gelu_exact(a)
        wrep = jnp.dot(w.astype(BF16), poolt_ref[...], preferred_element_type=F32)
        for t in range(tb):
            wexp = jnp.where(diag, wrep[t:t + 1, :], 0.0).astype(BF16)
            zv = buf[pl.ds(t * ne, ne), d_sub:2 * d_sub, :].reshape(ne * d_sub, LANES).astype(BF16)
            o_ref[row0 + t] = x_ref[row0 + t] + jnp.dot(wexp, zv, preferred_element_type=F32)
            issue_rows(nxt_iref, nxt_row0, nxt_buf, nxt_which, (tb + t) * seg, (tb + t + 1) * seg)

    wait_tile(buf_a, 0)
    tile(0, buf_a, idx_ref, tb, buf_b, 1)
    wait_tile(buf_b, 1)
    tile(tb, buf_b, idxn_ref, 0, buf_a, 0)

    @pl.when(i == n - 1)
    def _():
        wait_tile(buf_a, 0)


def peer_experts(idx, gates, h3, x3, uv3, *, tb):
    m, ne = idx.shape
    d_sub = h3.shape[1]
    n = m // (2 * tb)
    rows = tb * ne
    cid = jnp.arange(ne * d_sub) // d_sub
    pool = (cid[:, None] == jnp.arange(ne)[None, :]).astype(BF16)
    tok = lambda i: (i, 0)
    tok3 = lambda i: (i, 0, 0)
    return pl.pallas_call(
        functools.partial(_peer_expert_kernel, tb=tb, ne=ne),
        out_shape=jax.ShapeDtypeStruct(x3.shape, F32),
        grid=(n,),
        in_specs=[
            pl.BlockSpec((2 * tb, ne), tok, memory_space=pltpu.SMEM),
            pl.BlockSpec((2 * tb, ne), lambda i: (jnp.minimum(i + 1, n - 1), 0), memory_space=pltpu.SMEM),
            pl.BlockSpec((2 * tb, ne), tok),
            pl.BlockSpec((2 * tb, d_sub, LANES), tok3),
            pl.BlockSpec((2 * tb, d_sub, LANES), tok3),
            pl.BlockSpec((ne * d_sub, ne), lambda i: (0, 0)),
            pl.BlockSpec((ne, ne * d_sub), lambda i: (0, 0)),
            pl.BlockSpec(memory_space=pl.ANY),
        ],
        out_specs=pl.BlockSpec((2 * tb, d_sub, LANES), tok3),
        scratch_shapes=[
            pltpu.VMEM((rows, 2 * d_sub, LANES), F32),
            pltpu.VMEM((rows, 2 * d_sub, LANES), F32),
            pltpu.SemaphoreType.DMA((2, 2)),
        ],
        compiler_params=_cparams(("arbitrary",)),
        name="peer_experts",
    )(idx, idx, gates, h3, x3, pool, pool.T, uv3)


def kernel(x, mem, positions, attn_norm_g, mem_norm_g, w_in, b_gate, diff_q_norm_g, diff_k_norm_g, lambda_q1, lambda_k1, lambda_q2, lambda_k2, diff_subln_g, rel_bias_table, conv_w, a_log, dt_bias, delta_out_norm_g, w_mem_kv, mem_q_norm_g, mem_k_norm_g, w_br_diff, w_br_delta, w_br_mem, w_out, ffn_norm_g, w_query, sub_keys, expert_u, expert_v):
    del positions
    b, t, d = x.shape
    mtok = mem.shape[1]
    m = b * t
    depth = w_in.shape[0]
    nh = N_DIFF_HEADS
    qk_w = nh * 2 * DIFF_HEAD_DIM
    dv_w = nh * 2 * DIFF_HEAD_DIM
    dl_w = N_DELTA_HEADS * DELTA_HEAD_DIM
    mem_w = N_MEM_HEADS * MEM_HEAD_DIM
    tq = min(256, t)
    tt_delta = min(256, t)

    x2d = x.reshape(m, d)
    for l in range(depth):
        lam_init = 0.8 - 0.6 * math.exp(-0.3 * l)
        wl = w_in[l]
        o = 0
        w_qk = wl[:, o:o + 2 * qk_w]; o += 2 * qk_w
        w_dv = wl[:, o:o + dv_w]; o += dv_w
        w_lqkv = wl[:, o:o + 3 * dl_w]; o += 3 * dl_w
        w_lz = wl[:, o:o + dl_w]; o += dl_w
        w_ab = wl[:, o:o + 2 * N_DELTA_HEADS]; o += 2 * N_DELTA_HEADS
        w_mq = wl[:, o:o + mem_w]; o += mem_w
        w_gate = wl[:, o:o + 3 * d]

        qk_gain = jnp.concatenate([jnp.tile(diff_q_norm_g[l], 2 * nh) * (DIFF_HEAD_DIM ** -0.5),
                                   jnp.tile(diff_k_norm_g[l], 2 * nh)])
        qk = norm_matmul(x2d, w_qk.astype(BF16), norm_g=attn_norm_g[l], group=DIFF_HEAD_DIM, group_gain=qk_gain)
        w_rest = jnp.concatenate([w_dv, w_lqkv, w_lz, w_gate], axis=1).astype(BF16)
        rest = norm_matmul(x2d, w_rest, norm_g=attn_norm_g[l])
        mq_gain = jnp.tile(mem_q_norm_g[l], N_MEM_HEADS) * (MEM_HEAD_DIM ** -0.5)
        mq = norm_matmul(x2d, w_mq.astype(BF16), norm_g=attn_norm_g[l], group=MEM_HEAD_DIM, group_gain=mq_gain)
        w_ab_pad = jnp.pad(w_ab, ((0, 0), (0, LANES - 2 * N_DELTA_HEADS))).astype(BF16)
        ab = norm_matmul(x2d, w_ab_pad, norm_g=attn_norm_g[l], out_dtype=F32, tn=LANES)
        abt = ab[:, :2 * N_DELTA_HEADS].T

        lam = (jnp.exp(jnp.sum(lambda_q1[l].astype(F32) * lambda_k1[l].astype(F32)))
               - jnp.exp(jnp.sum(lambda_q2[l].astype(F32) * lambda_k2[l].astype(F32))) + lam_init)
        bias_tiles = rel_bias_tiles(rel_bias_table, tq)
        scal = jnp.concatenate([jnp.stack([lam, jnp.asarray(1.0 - lam_init, F32)]),
                                rel_bias_table[REL_BUCKETS - 1].astype(F32)])
        y_diff = diff_attention(qk, rest, bias_tiles, scal, diff_subln_g[l], b=b, t=t, tq=tq, v_col0=0)

        head_params = jnp.stack([a_log[l], dt_bias[l]]).astype(F32)
        y_delta = gated_deltanet(rest, ab, abt, conv_w[l].astype(F32), head_params, delta_out_norm_g[l],
                                 b=b, t=t, tt=tt_delta, qkv_col0=dv_w // LANES,
                                 z_col0=(dv_w + 3 * dl_w) // LANES)

        mem2d = mem.reshape(b * mtok, d)
        wkv = w_mem_kv[l].astype(BF16)
        mk = norm_matmul(mem2d, wkv[:, :mem_w], norm_g=mem_norm_g[l], group=MEM_HEAD_DIM,
                         group_gain=jnp.tile(mem_k_norm_g[l], N_MEM_HEADS))
        mv = norm_matmul(mem2d, wkv[:, mem_w:], norm_g=mem_norm_g[l])
        y_mem = memory_attention(mq, mk, mv, b=b, t=t, mtok=mtok, tq=min(512, t))

        x1, h2 = merge_project(x2d, y_diff, y_delta, y_mem, rest, b_gate[l],
                               w_br_diff[l].astype(BF16), w_br_delta[l].astype(BF16), w_br_mem[l].astype(BF16),
                               w_out[l].astype(BF16), ffn_norm_g[l],
                               gate_col0=dv_w + 4 * dl_w, tm=min(512, m))

        qry = norm_matmul(h2, w_query[l].astype(BF16))
        idx_t, gate_t = peer_route(qry, sub_keys[l].astype(BF16), tt=min(512, m))
        ne = PEER_HEADS * PEER_TOPK
        idx = idx_t.reshape(ne, m).T
        gates = gate_t.reshape(ne, m).T
        sub = d // LANES
        uv3 = jnp.concatenate([expert_u[l].reshape(-1, sub, LANES), expert_v[l].reshape(-1, sub, LANES)], axis=1)
        x2 = peer_experts(idx, gates, h2.reshape(m, sub, LANES), x1.reshape(m, sub, LANES), uv3, tb=8)
        x2d = x2.reshape(m, d)
    return x2d.reshape(b, t, d)
```

```python
import functools
import math

import jax
import jax.numpy as jnp
from jax import lax
from jax.experimental import pallas as pl
from jax.experimental.pallas import tpu as pltpu

F32 = jnp.float32
BF16 = jnp.bfloat16
EPS = 1e-6
NEG = -1e30

N_DIFF_HEADS = 8
DIFF_HEAD_DIM = 64
N_DELTA_HEADS = 8
DELTA_HEAD_DIM = 128
CONV_WIDTH = 4
CHUNK = 64
N_MEM_HEADS = 4
MEM_HEAD_DIM = 256
REL_BUCKETS = 32
REL_MAX_DIST = 128
PEER_HEADS = 8
PEER_KEYS = 128
PEER_TOPK = 16
PEER_HALF = 128
LANES = 128
SUBLANES = 8
VMEM_LIMIT = 56 * 1024 * 1024


def _cparams(sem):
    return pltpu.CompilerParams(dimension_semantics=sem, vmem_limit_bytes=VMEM_LIMIT)


def _norm_matmul_kernel(*refs, has_norm, has_group):
    it = iter(refs)
    x_ref = next(it)
    g_ref = next(it) if has_norm else None
    w_ref = next(it)
    gm_ref = next(it) if has_group else None
    gain_ref = next(it) if has_group else None
    o_ref = next(it)
    h_ref = next(it)

    @pl.when(pl.program_id(1) == 0)
    def _():
        x = x_ref[...].astype(F32)
        if has_norm:
            ms = jnp.mean(x * x, axis=-1, keepdims=True)
            x = x * lax.rsqrt(ms + EPS) * g_ref[...]
        h_ref[...] = x.astype(BF16)

    y = jnp.dot(h_ref[...], w_ref[...], preferred_element_type=F32)
    if has_group:
        ms = jnp.dot((y * y).astype(BF16), gm_ref[...], preferred_element_type=F32)
        y = y * lax.rsqrt(ms + EPS) * gain_ref[...]
    o_ref[...] = y.astype(o_ref.dtype)


def norm_matmul(x, w, *, norm_g=None, group=None, group_gain=None, out_dtype=BF16, tm=512, tn=None):
    m, k = x.shape
    n = w.shape[1]
    has_norm = norm_g is not None
    has_group = group is not None
    if tn is None:
        tn = 256 if has_group else min(n, 1024)
    tm = min(tm, m)
    assert m % tm == 0 and n % tn == 0
    in_specs = [pl.BlockSpec((tm, k), lambda i, j: (i, 0))]
    args = [x]
    if has_norm:
        in_specs.append(pl.BlockSpec((1, k), lambda i, j: (0, 0)))
        args.append(norm_g.reshape(1, k).astype(F32))
    in_specs.append(pl.BlockSpec((k, tn), lambda i, j: (0, j)))
    args.append(w)
    if has_group:
        gid = jnp.arange(tn) // group
        gm = jnp.where(gid[:, None] == gid[None, :], 1.0 / group, 0.0).astype(BF16)
        in_specs.append(pl.BlockSpec((tn, tn), lambda i, j: (0, 0)))
        args.append(gm)
        in_specs.append(pl.BlockSpec((1, tn), lambda i, j: (0, j)))
        args.append(group_gain.reshape(1, n).astype(F32))
    return pl.pallas_call(
        functools.partial(_norm_matmul_kernel, has_norm=has_norm, has_group=has_group),
        out_shape=jax.ShapeDtypeStruct((m, n), out_dtype),
        grid=(m // tm, n // tn),
        in_specs=in_specs,
        out_specs=pl.BlockSpec((tm, tn), lambda i, j: (i, j)),
        scratch_shapes=[pltpu.VMEM((tm, k), BF16)],
        compiler_params=_cparams(("parallel", "arbitrary")),
        name="norm_matmul",
    )(*args)


def _t5_bucket(n):
    max_exact = REL_BUCKETS // 2
    nf = jnp.maximum(n, 1).astype(F32)
    large = max_exact + (jnp.log(nf / max_exact) / math.log(REL_MAX_DIST / max_exact)
                         * (REL_BUCKETS - max_exact)).astype(jnp.int32)
    large = jnp.minimum(large, REL_BUCKETS - 1)
    return jnp.where(n < max_exact, n, large)


def _rel_bias_kernel(table_ref, o_ref, *, tq):
    hm = pl.program_id(0)
    r = lax.broadcasted_iota(jnp.int32, (tq, 2 * tq), 0)
    c = lax.broadcasted_iota(jnp.int32, (tq, 2 * tq), 1)
    bucket = _t5_bucket(jnp.maximum(r - c + tq, 0))
    acc = jnp.zeros((tq, 2 * tq), F32)
    for b in range(REL_BUCKETS):
        acc = jnp.where(bucket == b, table_ref[b, hm], acc)
    o_ref[0] = acc


def rel_bias_tiles(rel_table, tq):
    nmaps = rel_table.shape[1]
    return pl.pallas_call(
        functools.partial(_rel_bias_kernel, tq=tq),
        out_shape=jax.ShapeDtypeStruct((nmaps, tq, 2 * tq), F32),
        grid=(nmaps,),
        in_specs=[pl.BlockSpec(memory_space=pltpu.SMEM)],
        out_specs=pl.BlockSpec((1, tq, 2 * tq), lambda i: (i, 0, 0)),
        compiler_params=_cparams(("arbitrary",)),
        name="rel_bias_tiles",
    )(rel_table.astype(F32))


def _diff_attn_kernel(scal_ref, q_ref, k_ref, v_ref, bias_ref, subg_ref, o_ref, *, tq, far_bucket_dist):
    del far_bucket_dist
    h = pl.program_id(1)
    i = pl.program_id(2)
    lam = scal_ref[0]
    out_scale = scal_ref[1]
    c1 = scal_ref[2 + 2 * h]
    c2 = scal_ref[3 + 2 * h]

    q = q_ref[...]
    lane = lax.broadcasted_iota(jnp.int32, (tq, LANES), 1)
    zero = jnp.zeros_like(q)
    qq = jnp.concatenate([jnp.where(lane < DIFF_HEAD_DIM, q, zero),
                          jnp.where(lane >= DIFF_HEAD_DIM, q, zero)], axis=0)
    row2 = lax.broadcasted_iota(jnp.int32, (2 * tq, 1), 0)
    cfar = jnp.where(row2 < tq, c1, c2)

    def update(carry, s, vj):
        m, l, acc = carry
        m_new = jnp.maximum(m, jnp.max(s, axis=-1, keepdims=True))
        alpha = jnp.exp(m - m_new)
        p = jnp.exp(s - m_new)
        l = l * alpha + jnp.sum(p, axis=-1, keepdims=True)
        acc = acc * alpha + jnp.dot(p.astype(BF16), vj, preferred_element_type=F32)
        return m_new, l, acc

    def scores(start, width):
        kj = k_ref[pl.ds(pl.multiple_of(start, tq), width), :]
        vj = v_ref[pl.ds(pl.multiple_of(start, tq), width), :]
        s = lax.dot_general(qq, kj, (((1,), (1,)), ((), ())), preferred_element_type=F32)
        return s, vj

    def far_step(j, carry):
        s, vj = scores(j * (2 * tq), 2 * tq)
        return update(carry, s + cfar, vj)

    def far_single(carry):
        s, vj = scores((i - 2) * tq, tq)
        return update(carry, s + cfar, vj)

    n_far = jnp.maximum(i - 1, 0)
    init = (jnp.full((2 * tq, 1), NEG, F32), jnp.zeros((2 * tq, 1), F32), jnp.zeros((2 * tq, LANES), F32))
    carry = lax.fori_loop(0, n_far // 2, far_step, init)
    carry = lax.cond(n_far % 2 == 1, far_single, lambda c: c, carry)

    b_prev = jnp.concatenate([bias_ref[0, :, 0:tq], bias_ref[1, :, 0:tq]], axis=0)
    s, vj = scores(jnp.maximum(i - 1, 0) * tq, tq)
    s = jnp.where(i >= 1, s + b_prev, NEG)
    carry = update(carry, s, vj)

    b_diag = jnp.concatenate([bias_ref[0, :, tq:2 * tq], bias_ref[1, :, tq:2 * tq]], axis=0)
    s, vj = scores(i * tq, tq)
    rq = lax.broadcasted_iota(jnp.int32, (2 * tq, tq), 0)
    rq = jnp.where(rq >= tq, rq - tq, rq)
    ck = lax.broadcasted_iota(jnp.int32, (2 * tq, tq), 1)
    s = jnp.where(ck <= rq, s + b_diag, NEG)
    m, l, acc = update(carry, s, vj)

    o = acc / l
    o = o[0:tq] - lam * o[tq:2 * tq]
    ms = jnp.mean(o * o, axis=-1, keepdims=True)
    o = o * lax.rsqrt(ms + EPS) * subg_ref[...] * out_scale
    o_ref[...] = o.astype(o_ref.dtype)


def diff_attention(qk, rest, bias_tiles, scal, subln_g, *, b, t, tq, v_col0):
    m = b * t
    nq = t // tq
    nh = N_DIFF_HEADS
    return pl.pallas_call(
        functools.partial(_diff_attn_kernel, tq=tq, far_bucket_dist=None),
        out_shape=jax.ShapeDtypeStruct((m, nh * LANES), BF16),
        grid=(b, nh, nq),
        in_specs=[
            pl.BlockSpec(memory_space=pltpu.SMEM),
            pl.BlockSpec((tq, LANES), lambda bi, h, i: (bi * nq + i, h)),
            pl.BlockSpec((t, LANES), lambda bi, h, i: (bi, nh + h)),
            pl.BlockSpec((t, LANES), lambda bi, h, i: (bi, v_col0 + h)),
            pl.BlockSpec((2, tq, 2 * tq), lambda bi, h, i: (h, 0, 0)),
            pl.BlockSpec((1, LANES), lambda bi, h, i: (0, 0)),
        ],
        out_specs=pl.BlockSpec((tq, LANES), lambda bi, h, i: (bi * nq + i, h)),
        compiler_params=_cparams(("parallel", "parallel", "arbitrary")),
        name="diff_attention",
    )(scal, qk, qk, rest, bias_tiles, subln_g.reshape(1, LANES).astype(F32))


def _sigmoid(x):
    return 1.0 / (1.0 + jnp.exp(-x))


def _softplus(x):
    return jnp.maximum(x, 0.0) + jnp.log(1.0 + jnp.exp(-jnp.abs(x)))


def _delta_kernel(hp_ref, xq_ref, xk_ref, xv_ref, pq_ref, pk_ref, pv_ref, wq_ref, wk_ref, wv_ref,
                  ab_ref, abt_ref, z_ref, og_ref, o_ref, s_ref, *, tt):
    h = pl.program_id(1)
    i = pl.program_id(2)
    nc = tt // CHUNK
    a_log = hp_ref[0, h]
    dt_bias = hp_ref[1, h]

    @pl.when(i == 0)
    def _():
        s_ref[...] = jnp.zeros_like(s_ref)

    def conv_silu(x_ref, p_ref, w_ref):
        prev = jnp.where(i > 0, p_ref[...].astype(F32), 0.0)
        xf = jnp.concatenate([prev, x_ref[...].astype(F32)], axis=0)
        w = w_ref[...]
        base = SUBLANES - (CONV_WIDTH - 1)
        y = xf[base:base + tt] * w[0:1]
        for c in range(1, CONV_WIDTH):
            y = y + xf[base + c:base + c + tt] * w[c:c + 1]
        return y * _sigmoid(y)

    q = conv_silu(xq_ref, pq_ref, wq_ref)
    k = conv_silu(xk_ref, pk_ref, wk_ref)
    v = conv_silu(xv_ref, pv_ref, wv_ref)
    q = q * lax.rsqrt(jnp.sum(q * q, axis=-1, keepdims=True) + EPS) * (DELTA_HEAD_DIM ** -0.5)
    k = k * lax.rsqrt(jnp.sum(k * k, axis=-1, keepdims=True) + EPS)

    ab = ab_ref[...]
    lane = lax.broadcasted_iota(jnp.int32, ab.shape, 1)
    lb_col = jnp.sum(jnp.where(lane == h, ab, 0.0), axis=-1, keepdims=True)
    la_col = jnp.sum(jnp.where(lane == N_DELTA_HEADS + h, ab, 0.0), axis=-1, keepdims=True)
    la_row = abt_ref[pl.ds(N_DELTA_HEADS + h, 1), :]
    neg_a = -jnp.exp(jnp.full((1, 1), a_log, F32))
    beta_col = _sigmoid(lb_col)
    g_col = neg_a * _softplus(la_col + dt_bias)
    g_row = neg_a * _softplus(la_row + dt_bias)

    ri = lax.broadcasted_iota(jnp.int32, (CHUNK, CHUNK), 0)
    ci = lax.broadcasted_iota(jnp.int32, (CHUNK, CHUNK), 1)
    tril = ri >= ci
    strict = ri > ci

    def mm(a, b):
        return jnp.dot(a.astype(BF16), b.astype(BF16), preferred_element_type=F32)

    def mm_nt(a, b):
        return lax.dot_general(a.astype(BF16), b.astype(BF16), (((1,), (1,)), ((), ())),
                               preferred_element_type=F32)

    def mm_tn(a, b):
        return lax.dot_general(a.astype(BF16), b.astype(BF16), (((0,), (0,)), ((), ())),
                               preferred_element_type=F32)

    state = s_ref[...]
    outs = []
    for c in range(nc):
        r0 = c * CHUNK
        qc, kc, vc = q[r0:r0 + CHUNK], k[r0:r0 + CHUNK], v[r0:r0 + CHUNK]
        bc = beta_col[r0:r0 + CHUNK]
        gcol = g_col[r0:r0 + CHUNK]
        grow = g_row[:, r0:r0 + CHUNK]
        gc_col = jnp.sum(jnp.where(tril, grow, 0.0), axis=-1, keepdims=True)
        gc_row = jnp.sum(jnp.where(ri <= ci, gcol, 0.0), axis=0, keepdims=True)
        g_last = jnp.sum(grow, axis=-1, keepdims=True)
        decay = jnp.where(tril, jnp.exp(jnp.where(tril, gc_col - gc_row, 0.0)), 0.0)
        kk = mm_nt(kc, kc)
        x = jnp.where(strict, -(bc * kk * decay), 0.0)
        p = x
        xp = x
        for _ in range(5):
            xp = mm(xp, xp)
            p = p + xp + mm(p, xp)
        rhs_v = bc * vc
        rhs_k = (bc * jnp.exp(gc_col)) * kc
        u = rhs_v + mm(p, rhs_v)
        w = rhs_k + mm(p, rhs_k)
        intra = jnp.where(tril, mm_nt(qc, kc) * decay, 0.0)
        v_new = u - mm(w, state)
        o = mm(qc * jnp.exp(gc_col), state) + mm(intra, v_new)
        k_dec = kc * jnp.exp(g_last - gc_col)
        state = state * jnp.exp(g_last) + mm_tn(k_dec, v_new)
        outs.append(o)
    s_ref[...] = state

    o = jnp.concatenate(outs, axis=0)
    ms = jnp.mean(o * o, axis=-1, keepdims=True)
    z = z_ref[...].astype(F32)
    o = o * lax.rsqrt(ms + EPS) * og_ref[...] * (z * _sigmoid(z))
    o_ref[...] = o.astype(o_ref.dtype)


def gated_deltanet(rest, ab, abt, conv_w, head_params, out_g, *, b, t, tt, qkv_col0, z_col0):
    m = b * t
    nt = t // tt
    nh = N_DELTA_HEADS
    sub = tt // SUBLANES

    def cur(off):
        return pl.BlockSpec((tt, LANES), lambda bi, h, i: (bi * nt + i, off + h))

    def prev(off):
        return pl.BlockSpec((SUBLANES, LANES),
                            lambda bi, h, i: (jnp.maximum((bi * nt + i) * sub - 1, 0), off + h))

    def wspec(off):
        return pl.BlockSpec((CONV_WIDTH, LANES), lambda bi, h, i: (0, off + h))

    return pl.pallas_call(
        functools.partial(_delta_kernel, tt=tt),
        out_shape=jax.ShapeDtypeStruct((m, nh * LANES), BF16),
        grid=(b, nh, nt),
        in_specs=[
            pl.BlockSpec(memory_space=pltpu.SMEM),
            cur(qkv_col0), cur(qkv_col0 + nh), cur(qkv_col0 + 2 * nh),
            prev(qkv_col0), prev(qkv_col0 + nh), prev(qkv_col0 + 2 * nh),
            wspec(0), wspec(nh), wspec(2 * nh),
            pl.BlockSpec((tt, LANES), lambda bi, h, i: (bi * nt + i, 0)),
            pl.BlockSpec((2 * nh, tt), lambda bi, h, i: (0, bi * nt + i)),
            cur(z_col0),
            pl.BlockSpec((1, LANES), lambda bi, h, i: (0, 0)),
        ],
        out_specs=pl.BlockSpec((tt, LANES), lambda bi, h, i: (bi * nt + i, h)),
        scratch_shapes=[pltpu.VMEM((DELTA_HEAD_DIM, DELTA_HEAD_DIM), F32)],
        compiler_params=_cparams(("parallel", "parallel", "arbitrary")),
        name="gated_deltanet",
    )(head_params, rest, rest, rest, rest, rest, rest, conv_w, conv_w, conv_w, ab, abt, rest,
      out_g.reshape(1, LANES).astype(F32))


def _mem_attn_kernel(q_ref, k_ref, v_ref, o_ref):
    outs = []
    for h in range(N_MEM_HEADS):
        sl = slice(h * MEM_HEAD_DIM, (h + 1) * MEM_HEAD_DIM)
        s = lax.dot_general(q_ref[:, sl], k_ref[:, sl], (((1,), (1,)), ((), ())), preferred_element_type=F32)
        s = s - jnp.max(s, axis=-1, keepdims=True)
        p = jnp.exp(s)
        p = p / jnp.sum(p, axis=-1, keepdims=True)
        outs.append(jnp.dot(p.astype(BF16), v_ref[:, sl], preferred_element_type=F32))
    o_ref[...] = jnp.concatenate(outs, axis=-1).astype(o_ref.dtype)


def memory_attention(mq, mk, mv, *, b, t, mtok, tq):
    nq = t // tq
    w = N_MEM_HEADS * MEM_HEAD_DIM
    return pl.pallas_call(
        _mem_attn_kernel,
        out_shape=jax.ShapeDtypeStruct((b * t, w), BF16),
        grid=(b, nq),
        in_specs=[
            pl.BlockSpec((tq, w), lambda bi, i: (bi * nq + i, 0)),
            pl.BlockSpec((mtok, w), lambda bi, i: (bi, 0)),
            pl.BlockSpec((mtok, w), lambda bi, i: (bi, 0)),
        ],
        out_specs=pl.BlockSpec((tq, w), lambda bi, i: (bi * nq + i, 0)),
        compiler_params=_cparams(("parallel", "arbitrary")),
        name="memory_attention",
    )(mq, mk, mv)


def _merge_kernel(x_ref, ya_ref, yb_ref, yc_ref, ga_ref, gb_ref, gc_ref, bg_ref,
                  wa_ref, wb_ref, wc_ref, wo_ref, fg_ref, x1_ref, h2_ref):
    def branch(y_ref, g_ref, w_ref, idx):
        gate = _sigmoid(g_ref[...].astype(F32) + bg_ref[idx:idx + 1, :])
        return gate * jnp.dot(y_ref[...], w_ref[...], preferred_element_type=F32)

    merged = branch(ya_ref, ga_ref, wa_ref, 0) + branch(yb_ref, gb_ref, wb_ref, 1) + branch(yc_ref, gc_ref, wc_ref, 2)
    x1 = x_ref[...] + jnp.dot(merged.astype(BF16), wo_ref[...], preferred_element_type=F32)
    x1_ref[...] = x1
    ms = jnp.mean(x1 * x1, axis=-1, keepdims=True)
    h2_ref[...] = (x1 * lax.rsqrt(ms + EPS) * fg_ref[...]).astype(h2_ref.dtype)


def merge_project(x2d, y_diff, y_delta, y_mem, rest, b_gate, w_a, w_b, w_c, w_o, ffn_g, *, gate_col0, tm):
    m, d = x2d.shape
    row = lambda i: (i, 0)
    const = lambda i: (0, 0)
    assert gate_col0 % d == 0

    def gspec(j):
        return pl.BlockSpec((tm, d), lambda i: (i, gate_col0 // d + j))

    return pl.pallas_call(
        _merge_kernel,
        out_shape=(jax.ShapeDtypeStruct((m, d), F32), jax.ShapeDtypeStruct((m, d), BF16)),
        grid=(m // tm,),
        in_specs=[
            pl.BlockSpec((tm, d), row), pl.BlockSpec((tm, d), row), pl.BlockSpec((tm, d), row),
            pl.BlockSpec((tm, d), row), gspec(0), gspec(1), gspec(2),
            pl.BlockSpec((3, d), const),
            pl.BlockSpec((d, d), const), pl.BlockSpec((d, d), const), pl.BlockSpec((d, d), const),
            pl.BlockSpec((d, d), const), pl.BlockSpec((1, d), const),
        ],
        out_specs=(pl.BlockSpec((tm, d), row), pl.BlockSpec((tm, d), row)),
        compiler_params=_cparams(("parallel",)),
        name="merge_project",
    )(x2d, y_diff, y_delta, y_mem, rest, rest, rest, b_gate.reshape(3, d).astype(F32),
      w_a, w_b, w_c, w_o, ffn_g.reshape(1, d).astype(F32))


def _topk_rows(s, key, extra, k):
    big = jnp.iinfo(jnp.int32).max
    vals, keys, ext = [], [], []
    for _ in range(k):
        m = jnp.max(s, axis=0, keepdims=True)
        am = jnp.min(jnp.where(s == m, key, big), axis=0, keepdims=True)
        hit = key == am
        if extra is not None:
            ext.append(jnp.max(jnp.where(hit, extra, -1), axis=0, keepdims=True))
        s = jnp.where(hit, -jnp.inf, s)
        vals.append(m)
        keys.append(am)
    cat = lambda xs: jnp.concatenate(xs, axis=0)
    return cat(vals), cat(keys), (cat(ext) if extra is not None else None)


def _peer_route_kernel(q_ref, keys_ref, idx_ref, gate_ref):
    k = PEER_TOPK
    q = q_ref[...]
    tt = q.shape[0]

    def half(p):
        qp = q[:, p * PEER_HALF:(p + 1) * PEER_HALF]
        st = lax.dot_general(keys_ref[0, p], qp, (((1,), (1,)), ((), ())), preferred_element_type=F32)
        v, r, _ = _topk_rows(st, lax.broadcasted_iota(jnp.int32, st.shape, 0), None, k)
        return v, r

    s1, i1 = half(0)
    s2, i2 = half(1)
    g = SUBLANES
    sub = lax.broadcasted_iota(jnp.int32, (g, tt), 0)
    e1 = i1 * PEER_KEYS
    pieces = []
    for b in range(g):
        lim = min(g, k // (b + 1))
        sc = s1[0:g] + s2[b:b + 1]
        if lim < g:
            sc = jnp.where(sub < lim, sc, -jnp.inf)
        pieces.append((sc, sub * k + b, e1[0:g] + i2[b:b + 1]))
    pieces.append((s1[g:k] + s2[0:1], (sub + g) * k, e1[g:k] + i2[0:1]))
    pieces.append((s1[0:1] + s2[g:k], sub + g, e1[0:1] + i2[g:k]))
    cand_s = jnp.concatenate([p[0] for p in pieces], axis=0)
    cand_k = jnp.concatenate([p[1] for p in pieces], axis=0)
    cand_e = jnp.concatenate([p[2] for p in pieces], axis=0)
    top_s, _, top_e = _topk_rows(cand_s, cand_k, cand_e, k)
    e = jnp.exp(top_s - top_s[0:1])
    gate_ref[0] = e / jnp.sum(e, axis=0, keepdims=True)
    idx_ref[0] = top_e


def peer_route(qry, sub_keys, *, tt):
    m = qry.shape[0]
    nh = PEER_HEADS
    return pl.pallas_call(
        _peer_route_kernel,
        out_shape=(jax.ShapeDtypeStruct((nh, PEER_TOPK, m), jnp.int32),
                   jax.ShapeDtypeStruct((nh, PEER_TOPK, m), F32)),
        grid=(m // tt, nh),
        in_specs=[
            pl.BlockSpec((tt, 2 * PEER_HALF), lambda i, h: (i, h)),
            pl.BlockSpec((1, 2, PEER_KEYS, PEER_HALF), lambda i, h: (h, 0, 0, 0)),
        ],
        out_specs=(pl.BlockSpec((1, PEER_TOPK, tt), lambda i, h: (h, 0, i)),
                   pl.BlockSpec((1, PEER_TOPK, tt), lambda i, h: (h, 0, i))),
        compiler_params=_cparams(("parallel", "arbitrary")),
        name="peer_route",
    )(qry, sub_keys)


def _gelu_exact(x):
    return 0.5 * x * (1.0 + lax.erf(x * (2.0 ** -0.5)))


def _peer_expert_kernel(idx_ref, idxn_ref, gate_ref, h_ref, x_ref, pool_ref, poolt_ref, uv_hbm,
                        o_ref, buf_a, buf_b, sem, *, tb, ne):
    i = pl.program_id(0)
    n = pl.num_programs(0)
    rows = tb * ne
    d_sub = h_ref.shape[1]
    n_seg = 2 * tb
    seg = rows // n_seg

    def issue_rows(iref, row0, buf, which, r_lo, r_hi):
        for r in range(r_lo, r_hi):
            e = iref[row0 + r // ne, r % ne]
            pltpu.make_async_copy(uv_hbm.at[e], buf.at[r], sem.at[which, r % 2]).start(priority=r % 2)

    def wait_tile(buf, which):
        for k in range(2):
            pltpu.make_async_copy(uv_hbm.at[pl.ds(0, rows // 2)], buf.at[pl.ds(0, rows // 2)],
                                  sem.at[which, k]).wait()

    @pl.when(i == 0)
    def _():
        def tok(t, carry):
            for j in range(ne):
                pltpu.make_async_copy(uv_hbm.at[idx_ref[t, j]], buf_a.at[t * ne + j],
                                      sem.at[0, j % 2]).start(priority=j % 2)
            return carry
        lax.fori_loop(0, tb, tok, 0)

    blk = 2 * d_sub
    sub_id = lax.broadcasted_iota(jnp.int32, (d_sub, ne * blk), 0)
    col_id = lax.broadcasted_iota(jnp.int32, (d_sub, ne * blk), 1)
    diag_u = (col_id % blk) == sub_id
    diag_v = (col_id % blk) == sub_id + d_sub

    def tile(row0, buf, nxt_iref, nxt_row0, nxt_buf, nxt_which):
        drows = []
        for t in range(tb):
            z = buf[pl.ds(t * ne, ne)].reshape(ne * blk, LANES)
            c = lax.dot_general(h_ref[row0 + t], z, (((1,), (1,)), ((), ())), preferred_element_type=F32)
            drows.append(jnp.sum(jnp.where(diag_u, c, 0.0), axis=0, keepdims=True))
            issue_rows(nxt_iref, nxt_row0, nxt_buf, nxt_which, t * seg, (t + 1) * seg)
        dall = jnp.concatenate(drows, axis=0)
        d_hi = dall.astype(BF16)
        d_lo = (dall - d_hi.astype(F32)).astype(BF16)
        a = (jnp.dot(d_hi, pool_ref[...], preferred_element_type=F32)
             + jnp.dot(d_lo, pool_ref[...], preferred_element_type=F32))
        w = gate_ref[pl.ds(row0, tb), :] * _gelu_exact(a)
        wrep = jnp.dot(w.astype(BF16), poolt_ref[...], preferred_element_type=F32)
        for t in range(tb):
            wexp = jnp.where(diag_v, wrep[t:t + 1, :], 0.0).astype(BF16)
            z = buf[pl.ds(t * ne, ne)].reshape(ne * blk, LANES)
            o_ref[row0 + t] = x_ref[row0 + t] + jnp.dot(wexp, z, preferred_element_type=F32)
            issue_rows(nxt_iref, nxt_row0, nxt_buf, nxt_which, (tb + t) * seg, (tb + t + 1) * seg)

    wait_tile(buf_a, 0)
    tile(0, buf_a, idx_ref, tb, buf_b, 1)
    wait_tile(buf_b, 1)
    tile(tb, buf_b, idxn_ref, 0, buf_a, 0)

    @pl.when(i == n - 1)
    def _():
        wait_tile(buf_a, 0)


def peer_experts(idx, gates, h3, x3, uv3, *, tb):
    m, ne = idx.shape
    d_sub = h3.shape[1]
    n = m // (2 * tb)
    rows = tb * ne
    blk = 2 * d_sub
    cid = jnp.arange(ne * blk) // blk
    pool = (cid[:, None] == jnp.arange(ne)[None, :]).astype(BF16)
    tok = lambda i: (i, 0)
    tok3 = lambda i: (i, 0, 0)
    return pl.pallas_call(
        functools.partial(_peer_expert_kernel, tb=tb, ne=ne),
        out_shape=jax.ShapeDtypeStruct(x3.shape, F32),
        grid=(n,),
        in_specs=[
            pl.BlockSpec((2 * tb, ne), tok, memory_space=pltpu.SMEM),
            pl.BlockSpec((2 * tb, ne), lambda i: (jnp.minimum(i + 1, n - 1), 0), memory_space=pltpu.SMEM),
            pl.BlockSpec((2 * tb, ne), tok),
            pl.BlockSpec((2 * tb, d_sub, LANES), tok3),
            pl.BlockSpec((2 * tb, d_sub, LANES), tok3),
            pl.BlockSpec((ne * blk, ne), lambda i: (0, 0)),
            pl.BlockSpec((ne, ne * blk), lambda i: (0, 0)),
            pl.BlockSpec(memory_space=pl.ANY),
        ],
        out_specs=pl.BlockSpec((2 * tb, d_sub, LANES), tok3),
        scratch_shapes=[
            pltpu.VMEM((rows, blk, LANES), BF16),
            pltpu.VMEM((rows, blk, LANES), BF16),
            pltpu.SemaphoreType.DMA((2, 2)),
        ],
        compiler_params=_cparams(("arbitrary",)),
        name="peer_experts",
    )(idx, idx, gates, h3, x3, pool, pool.T, uv3)


def kernel(x, mem, positions, attn_norm_g, mem_norm_g, w_in, b_gate, diff_q_norm_g, diff_k_norm_g, lambda_q1, lambda_k1, lambda_q2, lambda_k2, diff_subln_g, rel_bias_table, conv_w, a_log, dt_bias, delta_out_norm_g, w_mem_kv, mem_q_norm_g, mem_k_norm_g, w_br_diff, w_br_delta, w_br_mem, w_out, ffn_norm_g, w_query, sub_keys, expert_u, expert_v):
    del positions
    b, t, d = x.shape
    mtok = mem.shape[1]
    m = b * t
    depth = w_in.shape[0]
    nh = N_DIFF_HEADS
    qk_w = nh * 2 * DIFF_HEAD_DIM
    dv_w = nh * 2 * DIFF_HEAD_DIM
    dl_w = N_DELTA_HEADS * DELTA_HEAD_DIM
    mem_w = N_MEM_HEADS * MEM_HEAD_DIM
    tq = min(256, t)
    tt_delta = min(256, t)

    x2d = x.reshape(m, d)
    for l in range(depth):
        lam_init = 0.8 - 0.6 * math.exp(-0.3 * l)
        wl = w_in[l]
        o = 0
        w_qk = wl[:, o:o + 2 * qk_w]; o += 2 * qk_w
        w_dv = wl[:, o:o + dv_w]; o += dv_w
        w_lqkv = wl[:, o:o + 3 * dl_w]; o += 3 * dl_w
        w_lz = wl[:, o:o + dl_w]; o += dl_w
        w_ab = wl[:, o:o + 2 * N_DELTA_HEADS]; o += 2 * N_DELTA_HEADS
        w_mq = wl[:, o:o + mem_w]; o += mem_w
        w_gate = wl[:, o:o + 3 * d]

        qk_gain = jnp.concatenate([jnp.tile(diff_q_norm_g[l], 2 * nh) * (DIFF_HEAD_DIM ** -0.5),
                                   jnp.tile(diff_k_norm_g[l], 2 * nh)])
        qk = norm_matmul(x2d, w_qk.astype(BF16), norm_g=attn_norm_g[l], group=DIFF_HEAD_DIM, group_gain=qk_gain)
        w_rest = jnp.concatenate([w_dv, w_lqkv, w_lz, w_gate], axis=1).astype(BF16)
        rest = norm_matmul(x2d, w_rest, norm_g=attn_norm_g[l])
        mq_gain = jnp.tile(mem_q_norm_g[l], N_MEM_HEADS) * (MEM_HEAD_DIM ** -0.5)
        mq = norm_matmul(x2d, w_mq.astype(BF16), norm_g=attn_norm_g[l], group=MEM_HEAD_DIM, group_gain=mq_gain)
        w_ab_pad = jnp.pad(w_ab, ((0, 0), (0, LANES - 2 * N_DELTA_HEADS))).astype(BF16)
        ab = norm_matmul(x2d, w_ab_pad, norm_g=attn_norm_g[l], out_dtype=F32, tn=LANES)
        abt = ab[:, :2 * N_DELTA_HEADS].T

        lam = (jnp.exp(jnp.sum(lambda_q1[l].astype(F32) * lambda_k1[l].astype(F32)))
               - jnp.exp(jnp.sum(lambda_q2[l].astype(F32) * lambda_k2[l].astype(F32))) + lam_init)
        bias_tiles = rel_bias_tiles(rel_bias_table, tq)
        scal = jnp.concatenate([jnp.stack([lam, jnp.asarray(1.0 - lam_init, F32)]),
                                rel_bias_table[REL_BUCKETS - 1].astype(F32)])
        y_diff = diff_attention(qk, rest, bias_tiles, scal, diff_subln_g[l], b=b, t=t, tq=tq, v_col0=0)

        head_params = jnp.stack([a_log[l], dt_bias[l]]).astype(F32)
        y_delta = gated_deltanet(rest, ab, abt, conv_w[l].astype(F32), head_params, delta_out_norm_g[l],
                                 b=b, t=t, tt=tt_delta, qkv_col0=dv_w // LANES,
                                 z_col0=(dv_w + 3 * dl_w) // LANES)

        mem2d = mem.reshape(b * mtok, d)
        wkv = w_mem_kv[l].astype(BF16)
        mk = norm_matmul(mem2d, wkv[:, :mem_w], norm_g=mem_norm_g[l], group=MEM_HEAD_DIM,
                         group_gain=jnp.tile(mem_k_norm_g[l], N_MEM_HEADS))
        mv = norm_matmul(mem2d, wkv[:, mem_w:], norm_g=mem_norm_g[l])
        y_mem = memory_attention(mq, mk, mv, b=b, t=t, mtok=mtok, tq=min(512, t))

        x1, h2 = merge_project(x2d, y_diff, y_delta, y_mem, rest, b_gate[l],
                               w_br_diff[l].astype(BF16), w_br_delta[l].astype(BF16), w_br_mem[l].astype(BF16),
                               w_out[l].astype(BF16), ffn_norm_g[l],
                               gate_col0=dv_w + 4 * dl_w, tm=min(512, m))

        qry = norm_matmul(h2, w_query[l].astype(BF16))
        idx_t, gate_t = peer_route(qry, sub_keys[l].astype(BF16), tt=min(512, m))
        ne = PEER_HEADS * PEER_TOPK
        idx = idx_t.reshape(ne, m).T
        gates = gate_t.reshape(ne, m).T
        sub = d // LANES
        uv3 = jnp.concatenate([expert_u[l].reshape(-1, sub, LANES), expert_v[l].reshape(-1, sub, LANES)],
                              axis=1).astype(BF16)
        x2 = peer_experts(idx, gates, h2.reshape(m, sub, LANES), x1.reshape(m, sub, LANES), uv3, tb=8)
        x2d = x2.reshape(m, d)
    return x2d.reshape(b, t, d)
```

```python
import functools
import math

import jax
import jax.numpy as jnp
from jax import lax
from jax.experimental import pallas as pl
from jax.experimental.pallas import tpu as pltpu

F32 = jnp.float32
BF16 = jnp.bfloat16
EPS = 1e-6
NEG = -1e30
LOG2E = math.log2(math.e)

N_DIFF_HEADS = 8
DIFF_HEAD_DIM = 64
N_DELTA_HEADS = 8
DELTA_HEAD_DIM = 128
CONV_WIDTH = 4
CHUNK = 64
N_MEM_HEADS = 4
MEM_HEAD_DIM = 256
REL_BUCKETS = 32
REL_MAX_DIST = 128
PEER_HEADS = 8
PEER_KEYS = 128
PEER_TOPK = 16
PEER_HALF = 128
LANES = 128
SUBLANES = 8
VMEM_LIMIT = 56 * 1024 * 1024


def _cparams(sem):
    return pltpu.CompilerParams(dimension_semantics=sem, vmem_limit_bytes=VMEM_LIMIT)


def _norm_matmul_kernel(*refs, has_norm, has_group):
    it = iter(refs)
    x_ref = next(it)
    g_ref = next(it) if has_norm else None
    w_ref = next(it)
    gm_ref = next(it) if has_group else None
    gain_ref = next(it) if has_group else None
    o_ref = next(it)
    h_ref = next(it)

    @pl.when(pl.program_id(1) == 0)
    def _():
        x = x_ref[...].astype(F32)
        if has_norm:
            ms = jnp.mean(x * x, axis=-1, keepdims=True)
            x = x * lax.rsqrt(ms + EPS) * g_ref[...]
        h_ref[...] = x.astype(BF16)

    y = jnp.dot(h_ref[...], w_ref[...], preferred_element_type=F32)
    if has_group:
        ms = jnp.dot((y * y).astype(BF16), gm_ref[...], preferred_element_type=F32)
        y = y * lax.rsqrt(ms + EPS) * gain_ref[...]
    o_ref[...] = y.astype(o_ref.dtype)


def norm_matmul(x, w, *, norm_g=None, group=None, group_gain=None, out_dtype=BF16, tm=512, tn=None):
    m, k = x.shape
    n = w.shape[1]
    has_norm = norm_g is not None
    has_group = group is not None
    if tn is None:
        tn = 256 if has_group else min(n, 1024)
    tm = min(tm, m)
    assert m % tm == 0 and n % tn == 0
    in_specs = [pl.BlockSpec((tm, k), lambda i, j: (i, 0))]
    args = [x]
    if has_norm:
        in_specs.append(pl.BlockSpec((1, k), lambda i, j: (0, 0)))
        args.append(norm_g.reshape(1, k).astype(F32))
    in_specs.append(pl.BlockSpec((k, tn), lambda i, j: (0, j)))
    args.append(w)
    if has_group:
        gid = jnp.arange(tn) // group
        gm = jnp.where(gid[:, None] == gid[None, :], 1.0 / group, 0.0).astype(BF16)
        in_specs.append(pl.BlockSpec((tn, tn), lambda i, j: (0, 0)))
        args.append(gm)
        in_specs.append(pl.BlockSpec((1, tn), lambda i, j: (0, j)))
        args.append(group_gain.reshape(1, n).astype(F32))
    return pl.pallas_call(
        functools.partial(_norm_matmul_kernel, has_norm=has_norm, has_group=has_group),
        out_shape=jax.ShapeDtypeStruct((m, n), out_dtype),
        grid=(m // tm, n // tn),
        in_specs=in_specs,
        out_specs=pl.BlockSpec((tm, tn), lambda i, j: (i, j)),
        scratch_shapes=[pltpu.VMEM((tm, k), BF16)],
        compiler_params=_cparams(("parallel", "arbitrary")),
        name="norm_matmul",
    )(*args)


def _t5_bucket(n):
    max_exact = REL_BUCKETS // 2
    nf = jnp.maximum(n, 1).astype(F32)
    large = max_exact + (jnp.log(nf / max_exact) / math.log(REL_MAX_DIST / max_exact)
                         * (REL_BUCKETS - max_exact)).astype(jnp.int32)
    large = jnp.minimum(large, REL_BUCKETS - 1)
    return jnp.where(n < max_exact, n, large)


def _rel_bias_kernel(table_ref, o_ref, *, tq):
    hm = pl.program_id(0)
    r = lax.broadcasted_iota(jnp.int32, (tq, 2 * tq), 0)
    c = lax.broadcasted_iota(jnp.int32, (tq, 2 * tq), 1)
    bucket = _t5_bucket(jnp.maximum(r - c + tq, 0))
    acc = jnp.zeros((tq, 2 * tq), F32)
    for b in range(REL_BUCKETS):
        acc = jnp.where(bucket == b, table_ref[b, hm], acc)
    o_ref[0] = acc * LOG2E


def rel_bias_tiles(rel_table, tq):
    nmaps = rel_table.shape[1]
    return pl.pallas_call(
        functools.partial(_rel_bias_kernel, tq=tq),
        out_shape=jax.ShapeDtypeStruct((nmaps, tq, 2 * tq), F32),
        grid=(nmaps,),
        in_specs=[pl.BlockSpec(memory_space=pltpu.SMEM)],
        out_specs=pl.BlockSpec((1, tq, 2 * tq), lambda i: (i, 0, 0)),
        compiler_params=_cparams(("arbitrary",)),
        name="rel_bias_tiles",
    )(rel_table.astype(F32))


def _diff_attn_kernel(scal_ref, q_ref, k_ref, v_ref, bias_ref, subg_ref, o_ref, *, tq, far_bucket_dist):
    del far_bucket_dist
    h = pl.program_id(1)
    i = pl.program_id(2)
    lam = scal_ref[0]
    out_scale = scal_ref[1]
    c1 = scal_ref[2 + 2 * h]
    c2 = scal_ref[3 + 2 * h]

    q = q_ref[...]
    lane = lax.broadcasted_iota(jnp.int32, (tq, LANES), 1)
    zero = jnp.zeros_like(q)
    qq = jnp.concatenate([jnp.where(lane < DIFF_HEAD_DIM, q, zero),
                          jnp.where(lane >= DIFF_HEAD_DIM, q, zero)], axis=0)
    row2 = lax.broadcasted_iota(jnp.int32, (2 * tq, 1), 0)
    cfar = jnp.where(row2 < tq, c1, c2)

    def update(carry, s, vj, shift=None):
        m, l, acc = carry
        rowmax = jnp.max(s, axis=-1, keepdims=True)
        if shift is not None:
            rowmax = rowmax + shift
        m_new = jnp.maximum(m, rowmax)
        alpha = jnp.exp2(m - m_new)
        p = jnp.exp2(s - (m_new if shift is None else m_new - shift))
        l = l * alpha + jnp.sum(p, axis=-1, keepdims=True)
        acc = acc * alpha + jnp.dot(p.astype(BF16), vj, preferred_element_type=F32)
        return m_new, l, acc

    def scores(start, width):
        kj = k_ref[pl.ds(pl.multiple_of(start, tq), width), :]
        vj = v_ref[pl.ds(pl.multiple_of(start, tq), width), :]
        s = lax.dot_general(qq, kj, (((1,), (1,)), ((), ())), preferred_element_type=F32)
        return s, vj

    def far_step(j, carry):
        s, vj = scores(j * (2 * tq), 2 * tq)
        return update(carry, s, vj, cfar)

    def far_single(carry):
        s, vj = scores((i - 2) * tq, tq)
        return update(carry, s, vj, cfar)

    n_far = jnp.maximum(i - 1, 0)
    init = (jnp.full((2 * tq, 1), NEG, F32), jnp.zeros((2 * tq, 1), F32), jnp.zeros((2 * tq, LANES), F32))
    carry = lax.fori_loop(0, n_far // 2, far_step, init)
    carry = lax.cond(n_far % 2 == 1, far_single, lambda c: c, carry)

    b_prev = jnp.concatenate([bias_ref[0, :, 0:tq], bias_ref[1, :, 0:tq]], axis=0)
    s, vj = scores(jnp.maximum(i - 1, 0) * tq, tq)
    s = jnp.where(i >= 1, s + b_prev, NEG)
    carry = update(carry, s, vj)

    b_diag = jnp.concatenate([bias_ref[0, :, tq:2 * tq], bias_ref[1, :, tq:2 * tq]], axis=0)
    s, vj = scores(i * tq, tq)
    rq = lax.broadcasted_iota(jnp.int32, (2 * tq, tq), 0)
    rq = jnp.where(rq >= tq, rq - tq, rq)
    ck = lax.broadcasted_iota(jnp.int32, (2 * tq, tq), 1)
    s = jnp.where(ck <= rq, s + b_diag, NEG)
    m, l, acc = update(carry, s, vj)

    o = acc / l
    o = o[0:tq] - lam * o[tq:2 * tq]
    ms = jnp.mean(o * o, axis=-1, keepdims=True)
    o = o * lax.rsqrt(ms + EPS) * subg_ref[...] * out_scale
    o_ref[...] = o.astype(o_ref.dtype)


def diff_attention(qk, rest, bias_tiles, scal, subln_g, *, b, t, tq, v_col0):
    m = b * t
    nq = t // tq
    nh = N_DIFF_HEADS
    return pl.pallas_call(
        functools.partial(_diff_attn_kernel, tq=tq, far_bucket_dist=None),
        out_shape=jax.ShapeDtypeStruct((m, nh * LANES), BF16),
        grid=(b, nh, nq),
        in_specs=[
            pl.BlockSpec(memory_space=pltpu.SMEM),
            pl.BlockSpec((tq, LANES), lambda bi, h, i: (bi * nq + i, h)),
            pl.BlockSpec((t, LANES), lambda bi, h, i: (bi, nh + h)),
            pl.BlockSpec((t, LANES), lambda bi, h, i: (bi, v_col0 + h)),
            pl.BlockSpec((2, tq, 2 * tq), lambda bi, h, i: (h, 0, 0)),
            pl.BlockSpec((1, LANES), lambda bi, h, i: (0, 0)),
        ],
        out_specs=pl.BlockSpec((tq, LANES), lambda bi, h, i: (bi * nq + i, h)),
        compiler_params=_cparams(("parallel", "parallel", "arbitrary")),
        name="diff_attention",
    )(scal, qk, qk, rest, bias_tiles, subln_g.reshape(1, LANES).astype(F32))


def _sigmoid(x):
    return 1.0 / (1.0 + jnp.exp(-x))


def _softplus(x):
    return jnp.maximum(x, 0.0) + jnp.log(1.0 + jnp.exp(-jnp.abs(x)))


def _delta_kernel(hp_ref, xq_ref, xk_ref, xv_ref, pq_ref, pk_ref, pv_ref, wq_ref, wk_ref, wv_ref,
                  ab_ref, abt_ref, z_ref, og_ref, o_ref, s_ref, *, tt, nhb):
    hb = pl.program_id(1)
    i = pl.program_id(2)
    nc = tt // CHUNK
    dh = DELTA_HEAD_DIM
    heads = range(nhb)

    @pl.when(i == 0)
    def _():
        s_ref[...] = jnp.zeros_like(s_ref)

    def conv_silu(x_ref, p_ref, w_ref):
        prev = jnp.where(i > 0, p_ref[...].astype(F32), 0.0)
        xf = jnp.concatenate([prev, x_ref[...].astype(F32)], axis=0)
        w = w_ref[...]
        base = SUBLANES - (CONV_WIDTH - 1)
        y = xf[base:base + tt] * w[0:1]
        for c in range(1, CONV_WIDTH):
            y = y + xf[base + c:base + c + tt] * w[c:c + 1]
        return y * _sigmoid(y)

    def split(x):
        return [x[:, hd * dh:(hd + 1) * dh] for hd in heads]

    q = split(conv_silu(xq_ref, pq_ref, wq_ref))
    k = split(conv_silu(xk_ref, pk_ref, wk_ref))
    v = split(conv_silu(xv_ref, pv_ref, wv_ref))
    q = [x * lax.rsqrt(jnp.sum(x * x, axis=-1, keepdims=True) + EPS) * (dh ** -0.5) for x in q]
    k = [x * lax.rsqrt(jnp.sum(x * x, axis=-1, keepdims=True) + EPS) for x in k]

    ab = ab_ref[...]
    lane = lax.broadcasted_iota(jnp.int32, ab.shape, 1)
    beta_col, g_col, g_row = [], [], []
    for hd in heads:
        h = hb * nhb + hd
        neg_a = -jnp.exp(jnp.full((1, 1), hp_ref[0, h], F32))
        dt_bias = hp_ref[1, h]
        lb_col = jnp.sum(jnp.where(lane == h, ab, 0.0), axis=-1, keepdims=True)
        la_col = jnp.sum(jnp.where(lane == N_DELTA_HEADS + h, ab, 0.0), axis=-1, keepdims=True)
        la_row = abt_ref[pl.ds(N_DELTA_HEADS + h, 1), :]
        beta_col.append(_sigmoid(lb_col))
        g_col.append(neg_a * _softplus(la_col + dt_bias))
        g_row.append(neg_a * _softplus(la_row + dt_bias))

    ri = lax.broadcasted_iota(jnp.int32, (tt, tt), 0)
    ci = lax.broadcasted_iota(jnp.int32, (tt, tt), 1)
    same = (ri // CHUNK) == (ci // CHUNK)
    tril = same & (ri >= ci)
    strict = same & (ri > ci)
    triu = same & (ri <= ci)

    def mm(a, b):
        return jnp.dot(a.astype(BF16), b.astype(BF16), preferred_element_type=F32)

    def mm_nt(a, b):
        return lax.dot_general(a.astype(BF16), b.astype(BF16), (((1,), (1,)), ((), ())),
                               preferred_element_type=F32)

    def mm_tn(a, b):
        return lax.dot_general(a.astype(BF16), b.astype(BF16), (((0,), (0,)), ((), ())),
                               preferred_element_type=F32)

    gc_col = [jnp.sum(jnp.where(tril, g, 0.0), axis=-1, keepdims=True) for g in g_row]
    gc_row = [jnp.sum(jnp.where(triu, g, 0.0), axis=0, keepdims=True) for g in g_col]
    decay = [jnp.where(tril, jnp.exp(jnp.where(tril, a - b, 0.0)), 0.0) for a, b in zip(gc_col, gc_row)]
    e_gc = [jnp.exp(g) for g in gc_col]
    kk = [mm_nt(x, x) for x in k]
    qk = [mm_nt(a, b) for a, b in zip(q, k)]
    xm = [jnp.where(strict, -(b * kx * d), 0.0) for b, kx, d in zip(beta_col, kk, decay)]
    intra = [jnp.where(tril, a * d, 0.0) for a, d in zip(qk, decay)]
    r = [jnp.concatenate([b * vx, (b * e) * kx], axis=-1) for b, vx, e, kx in zip(beta_col, v, e_gc, k)]
    n_sq = CHUNK.bit_length() - 1
    for lvl in range(n_sq):
        r = [x + mm(m_, x) for m_, x in zip(xm, r)]
        if lvl + 1 < n_sq:
            xm = [mm(m_, m_) for m_ in xm]
    u = [x[:, 0:dh] for x in r]
    w = [x[:, dh:2 * dh] for x in r]
    qg = [a * e for a, e in zip(q, e_gc)]

    state = [s_ref[hd] for hd in heads]
    o_state = [[] for _ in heads]
    v_new = [[] for _ in heads]
    for c in range(nc):
        r0 = c * CHUNK
        sl = slice(r0, r0 + CHUNK)
        g_last = [g[r0 + CHUNK - 1:r0 + CHUNK] for g in gc_col]
        vn = [u[hd][sl] - mm(w[hd][sl], state[hd]) for hd in heads]
        os_ = [mm(qg[hd][sl], state[hd]) for hd in heads]
        k_dec = [k[hd][sl] * jnp.exp(g_last[hd] - gc_col[hd][sl]) for hd in heads]
        state = [state[hd] * jnp.exp(g_last[hd]) + mm_tn(k_dec[hd], vn[hd]) for hd in heads]
        for hd in heads:
            v_new[hd].append(vn[hd])
            o_state[hd].append(os_[hd])
    for hd in heads:
        s_ref[hd] = state[hd]

    outs = []
    for hd in heads:
        o = jnp.concatenate(o_state[hd], axis=0) + mm(intra[hd], jnp.concatenate(v_new[hd], axis=0))
        ms = jnp.mean(o * o, axis=-1, keepdims=True)
        outs.append(o * lax.rsqrt(ms + EPS) * og_ref[...])
    z = z_ref[...].astype(F32)
    o_ref[...] = (jnp.concatenate(outs, axis=-1) * (z * _sigmoid(z))).astype(o_ref.dtype)


def gated_deltanet(rest, ab, abt, conv_w, head_params, out_g, *, b, t, tt, nhb, qkv_col0, z_col0):
    m = b * t
    nt = t // tt
    nh = N_DELTA_HEADS
    sub = tt // SUBLANES
    wl = nhb * LANES
    assert nh % nhb == 0 and qkv_col0 % nhb == 0 and z_col0 % nhb == 0

    def cur(off):
        return pl.BlockSpec((tt, wl), lambda bi, h, i: (bi * nt + i, off // nhb + h))

    def prev(off):
        return pl.BlockSpec((SUBLANES, wl),
                            lambda bi, h, i: (jnp.maximum((bi * nt + i) * sub - 1, 0), off // nhb + h))

    def wspec(off):
        return pl.BlockSpec((CONV_WIDTH, wl), lambda bi, h, i: (0, off // nhb + h))

    return pl.pallas_call(
        functools.partial(_delta_kernel, tt=tt, nhb=nhb),
        out_shape=jax.ShapeDtypeStruct((m, nh * LANES), BF16),
        grid=(b, nh // nhb, nt),
        in_specs=[
            pl.BlockSpec(memory_space=pltpu.SMEM),
            cur(qkv_col0), cur(qkv_col0 + nh), cur(qkv_col0 + 2 * nh),
            prev(qkv_col0), prev(qkv_col0 + nh), prev(qkv_col0 + 2 * nh),
            wspec(0), wspec(nh), wspec(2 * nh),
            pl.BlockSpec((tt, LANES), lambda bi, h, i: (bi * nt + i, 0)),
            pl.BlockSpec((2 * nh, tt), lambda bi, h, i: (0, bi * nt + i)),
            cur(z_col0),
            pl.BlockSpec((1, LANES), lambda bi, h, i: (0, 0)),
        ],
        out_specs=pl.BlockSpec((tt, wl), lambda bi, h, i: (bi * nt + i, h)),
        scratch_shapes=[pltpu.VMEM((nhb, DELTA_HEAD_DIM, DELTA_HEAD_DIM), F32)],
        compiler_params=_cparams(("parallel", "parallel", "arbitrary")),
        name="gated_deltanet",
    )(head_params, rest, rest, rest, rest, rest, rest, conv_w, conv_w, conv_w, ab, abt, rest,
      out_g.reshape(1, LANES).astype(F32))


def _mem_attn_kernel(q_ref, k_ref, v_ref, o_ref):
    outs = []
    for h in range(N_MEM_HEADS):
        sl = slice(h * MEM_HEAD_DIM, (h + 1) * MEM_HEAD_DIM)
        s = lax.dot_general(q_ref[:, sl], k_ref[:, sl], (((1,), (1,)), ((), ())), preferred_element_type=F32)
        s = s - jnp.max(s, axis=-1, keepdims=True)
        p = jnp.exp(s)
        p = p / jnp.sum(p, axis=-1, keepdims=True)
        outs.append(jnp.dot(p.astype(BF16), v_ref[:, sl], preferred_element_type=F32))
    o_ref[...] = jnp.concatenate(outs, axis=-1).astype(o_ref.dtype)


def memory_attention(mq, mk, mv, *, b, t, mtok, tq):
    nq = t // tq
    w = N_MEM_HEADS * MEM_HEAD_DIM
    return pl.pallas_call(
        _mem_attn_kernel,
        out_shape=jax.ShapeDtypeStruct((b * t, w), BF16),
        grid=(b, nq),
        in_specs=[
            pl.BlockSpec((tq, w), lambda bi, i: (bi * nq + i, 0)),
            pl.BlockSpec((mtok, w), lambda bi, i: (bi, 0)),
            pl.BlockSpec((mtok, w), lambda bi, i: (bi, 0)),
        ],
        out_specs=pl.BlockSpec((tq, w), lambda bi, i: (bi * nq + i, 0)),
        compiler_params=_cparams(("parallel", "arbitrary")),
        name="memory_attention",
    )(mq, mk, mv)


def _merge_kernel(x_ref, ya_ref, yb_ref, yc_ref, ga_ref, gb_ref, gc_ref, bg_ref,
                  wa_ref, wb_ref, wc_ref, wo_ref, fg_ref, x1_ref, h2_ref):
    def branch(y_ref, g_ref, w_ref, idx):
        gate = _sigmoid(g_ref[...].astype(F32) + bg_ref[idx:idx + 1, :])
        return gate * jnp.dot(y_ref[...], w_ref[...], preferred_element_type=F32)

    merged = branch(ya_ref, ga_ref, wa_ref, 0) + branch(yb_ref, gb_ref, wb_ref, 1) + branch(yc_ref, gc_ref, wc_ref, 2)
    x1 = x_ref[...] + jnp.dot(merged.astype(BF16), wo_ref[...], preferred_element_type=F32)
    x1_ref[...] = x1
    ms = jnp.mean(x1 * x1, axis=-1, keepdims=True)
    h2_ref[...] = (x1 * lax.rsqrt(ms + EPS) * fg_ref[...]).astype(h2_ref.dtype)


def merge_project(x2d, y_diff, y_delta, y_mem, rest, b_gate, w_a, w_b, w_c, w_o, ffn_g, *, gate_col0, tm):
    m, d = x2d.shape
    row = lambda i: (i, 0)
    const = lambda i: (0, 0)
    assert gate_col0 % d == 0

    def gspec(j):
        return pl.BlockSpec((tm, d), lambda i: (i, gate_col0 // d + j))

    return pl.pallas_call(
        _merge_kernel,
        out_shape=(jax.ShapeDtypeStruct((m, d), F32), jax.ShapeDtypeStruct((m, d), BF16)),
        grid=(m // tm,),
        in_specs=[
            pl.BlockSpec((tm, d), row), pl.BlockSpec((tm, d), row), pl.BlockSpec((tm, d), row),
            pl.BlockSpec((tm, d), row), gspec(0), gspec(1), gspec(2),
            pl.BlockSpec((3, d), const),
            pl.BlockSpec((d, d), const), pl.BlockSpec((d, d), const), pl.BlockSpec((d, d), const),
            pl.BlockSpec((d, d), const), pl.BlockSpec((1, d), const),
        ],
        out_specs=(pl.BlockSpec((tm, d), row), pl.BlockSpec((tm, d), row)),
        compiler_params=_cparams(("parallel",)),
        name="merge_project",
    )(x2d, y_diff, y_delta, y_mem, rest, rest, rest, b_gate.reshape(3, d).astype(F32),
      w_a, w_b, w_c, w_o, ffn_g.reshape(1, d).astype(F32))


def _topk_rows(s, key, extra, k):
    big = jnp.iinfo(jnp.int32).max
    vals, keys, ext = [], [], []
    for _ in range(k):
        m = jnp.max(s, axis=0, keepdims=True)
        am = jnp.min(jnp.where(s == m, key, big), axis=0, keepdims=True)
        hit = key == am
        if extra is not None:
            ext.append(jnp.max(jnp.where(hit, extra, -1), axis=0, keepdims=True))
        s = jnp.where(hit, -jnp.inf, s)
        vals.append(m)
        keys.append(am)
    cat = lambda xs: jnp.concatenate(xs, axis=0)
    return cat(vals), cat(keys), (cat(ext) if extra is not None else None)


def _peer_route_kernel(q_ref, keys_ref, idx_ref, gate_ref):
    k = PEER_TOPK
    q = q_ref[...]
    tt = q.shape[0]

    def half(p):
        qp = q[:, p * PEER_HALF:(p + 1) * PEER_HALF]
        st = lax.dot_general(keys_ref[0, p], qp, (((1,), (1,)), ((), ())), preferred_element_type=F32)
        v, r, _ = _topk_rows(st, lax.broadcasted_iota(jnp.int32, st.shape, 0), None, k)
        return v, r

    s1, i1 = half(0)
    s2, i2 = half(1)
    g = SUBLANES
    sub = lax.broadcasted_iota(jnp.int32, (g, tt), 0)
    e1 = i1 * PEER_KEYS
    pieces = []
    for b in range(g):
        lim = min(g, k // (b + 1))
        sc = s1[0:g] + s2[b:b + 1]
        if lim < g:
            sc = jnp.where(sub < lim, sc, -jnp.inf)
        pieces.append((sc, sub * k + b, e1[0:g] + i2[b:b + 1]))
    pieces.append((s1[g:k] + s2[0:1], (sub + g) * k, e1[g:k] + i2[0:1]))
    pieces.append((s1[0:1] + s2[g:k], sub + g, e1[0:1] + i2[g:k]))
    cand_s = jnp.concatenate([p[0] for p in pieces], axis=0)
    cand_k = jnp.concatenate([p[1] for p in pieces], axis=0)
    cand_e = jnp.concatenate([p[2] for p in pieces], axis=0)
    top_s, _, top_e = _topk_rows(cand_s, cand_k, cand_e, k)
    e = jnp.exp(top_s - top_s[0:1])
    gate_ref[0] = e / jnp.sum(e, axis=0, keepdims=True)
    idx_ref[0] = top_e


def peer_route(qry, sub_keys, *, tt):
    m = qry.shape[0]
    nh = PEER_HEADS
    return pl.pallas_call(
        _peer_route_kernel,
        out_shape=(jax.ShapeDtypeStruct((nh, PEER_TOPK, m), jnp.int32),
                   jax.ShapeDtypeStruct((nh, PEER_TOPK, m), F32)),
        grid=(m // tt, nh),
        in_specs=[
            pl.BlockSpec((tt, 2 * PEER_HALF), lambda i, h: (i, h)),
            pl.BlockSpec((1, 2, PEER_KEYS, PEER_HALF), lambda i, h: (h, 0, 0, 0)),
        ],
        out_specs=(pl.BlockSpec((1, PEER_TOPK, tt), lambda i, h: (h, 0, i)),
                   pl.BlockSpec((1, PEER_TOPK, tt), lambda i, h: (h, 0, i))),
        compiler_params=_cparams(("parallel", "arbitrary")),
        name="peer_route",
    )(qry, sub_keys)


def _gelu_exact(x):
    return 0.5 * x * (1.0 + lax.erf(x * (2.0 ** -0.5)))


def _peer_expert_kernel(idx_ref, idxn_ref, gate_ref, h_ref, x_ref, pool_ref, poolt_ref, uv_hbm,
                        o_ref, buf_a, buf_b, sem, *, tb, ne):
    i = pl.program_id(0)
    n = pl.num_programs(0)
    rows = tb * ne
    d_sub = h_ref.shape[1]
    n_seg = 2 * tb
    seg = rows // n_seg

    def issue_rows(iref, row0, buf, which, r_lo, r_hi):
        for r in range(r_lo, r_hi):
            e = iref[row0 + r // ne, r % ne]
            pltpu.make_async_copy(uv_hbm.at[e], buf.at[r], sem.at[which, r % 2]).start(priority=r % 2)

    def wait_tile(buf, which):
        for k in range(2):
            pltpu.make_async_copy(uv_hbm.at[pl.ds(0, rows // 2)], buf.at[pl.ds(0, rows // 2)],
                                  sem.at[which, k]).wait()

    @pl.when(i == 0)
    def _():
        def tok(t, carry):
            for j in range(ne):
                pltpu.make_async_copy(uv_hbm.at[idx_ref[t, j]], buf_a.at[t * ne + j],
                                      sem.at[0, j % 2]).start(priority=j % 2)
            return carry
        lax.fori_loop(0, tb, tok, 0)

    blk = 2 * d_sub
    sub_id = lax.broadcasted_iota(jnp.int32, (d_sub, ne * blk), 0)
    col_id = lax.broadcasted_iota(jnp.int32, (d_sub, ne * blk), 1)
    diag_u = (col_id % blk) == sub_id
    diag_v = (col_id % blk) == sub_id + d_sub

    def tile(row0, buf, nxt_iref, nxt_row0, nxt_buf, nxt_which):
        drows = []
        for t in range(tb):
            z = buf[pl.ds(t * ne, ne)].reshape(ne * blk, LANES)
            c = lax.dot_general(h_ref[row0 + t], z, (((1,), (1,)), ((), ())), preferred_element_type=F32)
            drows.append(jnp.sum(jnp.where(diag_u, c, 0.0), axis=0, keepdims=True))
            issue_rows(nxt_iref, nxt_row0, nxt_buf, nxt_which, t * seg, (t + 1) * seg)
        dall = jnp.concatenate(drows, axis=0)
        d_hi = dall.astype(BF16)
        d_lo = (dall - d_hi.astype(F32)).astype(BF16)
        a = (jnp.dot(d_hi, pool_ref[...], preferred_element_type=F32)
             + jnp.dot(d_lo, pool_ref[...], preferred_element_type=F32))
        w = gate_ref[pl.ds(row0, tb), :] * _gelu_exact(a)
        wrep = jnp.dot(w.astype(BF16), poolt_ref[...], preferred_element_type=F32)
        for t in range(tb):
            wexp = jnp.where(diag_v, wrep[t:t + 1, :], 0.0).astype(BF16)
            z = buf[pl.ds(t * ne, ne)].reshape(ne * blk, LANES)
            o_ref[row0 + t] = x_ref[row0 + t] + jnp.dot(wexp, z, preferred_element_type=F32)
            issue_rows(nxt_iref, nxt_row0, nxt_buf, nxt_which, (tb + t) * seg, (tb + t + 1) * seg)

    wait_tile(buf_a, 0)
    tile(0, buf_a, idx_ref, tb, buf_b, 1)
    wait_tile(buf_b, 1)
    tile(tb, buf_b, idxn_ref, 0, buf_a, 0)

    @pl.when(i == n - 1)
    def _():
        wait_tile(buf_a, 0)


def peer_experts(idx, gates, h3, x3, uv3, *, tb):
    m, ne = idx.shape
    d_sub = h3.shape[1]
    n = m // (2 * tb)
    rows = tb * ne
    blk = 2 * d_sub
    cid = jnp.arange(ne * blk) // blk
    pool = (cid[:, None] == jnp.arange(ne)[None, :]).astype(BF16)
    tok = lambda i: (i, 0)
    tok3 = lambda i: (i, 0, 0)
    return pl.pallas_call(
        functools.partial(_peer_expert_kernel, tb=tb, ne=ne),
        out_shape=jax.ShapeDtypeStruct(x3.shape, F32),
        grid=(n,),
        in_specs=[
            pl.BlockSpec((2 * tb, ne), tok, memory_space=pltpu.SMEM),
            pl.BlockSpec((2 * tb, ne), lambda i: (jnp.minimum(i + 1, n - 1), 0), memory_space=pltpu.SMEM),
            pl.BlockSpec((2 * tb, ne), tok),
            pl.BlockSpec((2 * tb, d_sub, LANES), tok3),
            pl.BlockSpec((2 * tb, d_sub, LANES), tok3),
            pl.BlockSpec((ne * blk, ne), lambda i: (0, 0)),
            pl.BlockSpec((ne, ne * blk), lambda i: (0, 0)),
            pl.BlockSpec(memory_space=pl.ANY),
        ],
        out_specs=pl.BlockSpec((2 * tb, d_sub, LANES), tok3),
        scratch_shapes=[
            pltpu.VMEM((rows, blk, LANES), BF16),
            pltpu.VMEM((rows, blk, LANES), BF16),
            pltpu.SemaphoreType.DMA((2, 2)),
        ],
        compiler_params=_cparams(("arbitrary",)),
        name="peer_experts",
    )(idx, idx, gates, h3, x3, pool, pool.T, uv3)


def kernel(x, mem, positions, attn_norm_g, mem_norm_g, w_in, b_gate, diff_q_norm_g, diff_k_norm_g, lambda_q1, lambda_k1, lambda_q2, lambda_k2, diff_subln_g, rel_bias_table, conv_w, a_log, dt_bias, delta_out_norm_g, w_mem_kv, mem_q_norm_g, mem_k_norm_g, w_br_diff, w_br_delta, w_br_mem, w_out, ffn_norm_g, w_query, sub_keys, expert_u, expert_v):
    del positions
    b, t, d = x.shape
    mtok = mem.shape[1]
    m = b * t
    depth = w_in.shape[0]
    nh = N_DIFF_HEADS
    qk_w = nh * 2 * DIFF_HEAD_DIM
    dv_w = nh * 2 * DIFF_HEAD_DIM
    dl_w = N_DELTA_HEADS * DELTA_HEAD_DIM
    mem_w = N_MEM_HEADS * MEM_HEAD_DIM
    tq = min(256, t)
    tt_delta = min(256, t)

    x2d = x.reshape(m, d)
    for l in range(depth):
        lam_init = 0.8 - 0.6 * math.exp(-0.3 * l)
        wl = w_in[l]
        o = 0
        w_qk = wl[:, o:o + 2 * qk_w]; o += 2 * qk_w
        w_dv = wl[:, o:o + dv_w]; o += dv_w
        w_lqkv = wl[:, o:o + 3 * dl_w]; o += 3 * dl_w
        w_lz = wl[:, o:o + dl_w]; o += dl_w
        w_ab = wl[:, o:o + 2 * N_DELTA_HEADS]; o += 2 * N_DELTA_HEADS
        w_mq = wl[:, o:o + mem_w]; o += mem_w
        w_gate = wl[:, o:o + 3 * d]

        qk_gain = jnp.concatenate([jnp.tile(diff_q_norm_g[l], 2 * nh) * (DIFF_HEAD_DIM ** -0.5 * LOG2E),
                                   jnp.tile(diff_k_norm_g[l], 2 * nh)])
        qk = norm_matmul(x2d, w_qk.astype(BF16), norm_g=attn_norm_g[l], group=DIFF_HEAD_DIM, group_gain=qk_gain)
        w_rest = jnp.concatenate([w_dv, w_lqkv, w_lz, w_gate], axis=1).astype(BF16)
        rest = norm_matmul(x2d, w_rest, norm_g=attn_norm_g[l])
        mq_gain = jnp.tile(mem_q_norm_g[l], N_MEM_HEADS) * (MEM_HEAD_DIM ** -0.5)
        mq = norm_matmul(x2d, w_mq.astype(BF16), norm_g=attn_norm_g[l], group=MEM_HEAD_DIM, group_gain=mq_gain)
        w_ab_pad = jnp.pad(w_ab, ((0, 0), (0, LANES - 2 * N_DELTA_HEADS))).astype(BF16)
        ab = norm_matmul(x2d, w_ab_pad, norm_g=attn_norm_g[l], out_dtype=F32, tn=LANES)
        abt = ab[:, :2 * N_DELTA_HEADS].T

        lam = (jnp.exp(jnp.sum(lambda_q1[l].astype(F32) * lambda_k1[l].astype(F32)))
               - jnp.exp(jnp.sum(lambda_q2[l].astype(F32) * lambda_k2[l].astype(F32))) + lam_init)
        bias_tiles = rel_bias_tiles(rel_bias_table, tq)
        scal = jnp.concatenate([jnp.stack([lam, jnp.asarray(1.0 - lam_init, F32)]),
                                rel_bias_table[REL_BUCKETS - 1].astype(F32) * LOG2E])
        y_diff = diff_attention(qk, rest, bias_tiles, scal, diff_subln_g[l], b=b, t=t, tq=tq, v_col0=0)

        head_params = jnp.stack([a_log[l], dt_bias[l]]).astype(F32)
        y_delta = gated_deltanet(rest, ab, abt, conv_w[l].astype(F32), head_params, delta_out_norm_g[l],
                                 b=b, t=t, tt=tt_delta, nhb=8, qkv_col0=dv_w // LANES,
                                 z_col0=(dv_w + 3 * dl_w) // LANES)

        mem2d = mem.reshape(b * mtok, d)
        wkv = w_mem_kv[l].astype(BF16)
        mk = norm_matmul(mem2d, wkv[:, :mem_w], norm_g=mem_norm_g[l], group=MEM_HEAD_DIM,
                         group_gain=jnp.tile(mem_k_norm_g[l], N_MEM_HEADS))
        mv = norm_matmul(mem2d, wkv[:, mem_w:], norm_g=mem_norm_g[l])
        y_mem = memory_attention(mq, mk, mv, b=b, t=t, mtok=mtok, tq=min(512, t))

        x1, h2 = merge_project(x2d, y_diff, y_delta, y_mem, rest, b_gate[l],
                               w_br_diff[l].astype(BF16), w_br_delta[l].astype(BF16), w_br_mem[l].astype(BF16),
                               w_out[l].astype(BF16), ffn_norm_g[l],
                               gate_col0=dv_w + 4 * dl_w, tm=min(512, m))

        qry = norm_matmul(h2, w_query[l].astype(BF16))
        idx_t, gate_t = peer_route(qry, sub_keys[l].astype(BF16), tt=min(512, m))
        ne = PEER_HEADS * PEER_TOPK
        idx = idx_t.reshape(ne, m).T
        gates = gate_t.reshape(ne, m).T
        sub = d // LANES
        uv3 = jnp.concatenate([expert_u[l].reshape(-1, sub, LANES), expert_v[l].reshape(-1, sub, LANES)],
                              axis=1).astype(BF16)
        x2 = peer_experts(idx, gates, h2.reshape(m, sub, LANES), x1.reshape(m, sub, LANES), uv3, tb=8)
        x2d = x2.reshape(m, d)
    return x2d.reshape(b, t, d)
```

```python
import functools
import math

import jax
import jax.numpy as jnp
from jax import lax
from jax.experimental import pallas as pl
from jax.experimental.pallas import tpu as pltpu

F32 = jnp.float32
BF16 = jnp.bfloat16
EPS = 1e-6
NEG = -1e30
LOG2E = math.log2(math.e)

N_DIFF_HEADS = 8
DIFF_HEAD_DIM = 64
N_DELTA_HEADS = 8
DELTA_HEAD_DIM = 128
CONV_WIDTH = 4
CHUNK = 64
N_MEM_HEADS = 4
MEM_HEAD_DIM = 256
REL_BUCKETS = 32
REL_MAX_DIST = 128
PEER_HEADS = 8
PEER_KEYS = 128
PEER_TOPK = 16
PEER_HALF = 128
LANES = 128
SUBLANES = 8
VMEM_LIMIT = 56 * 1024 * 1024


def _cparams(sem):
    return pltpu.CompilerParams(dimension_semantics=sem, vmem_limit_bytes=VMEM_LIMIT)


def _norm_matmul_kernel(*refs, has_norm, has_group):
    it = iter(refs)
    x_ref = next(it)
    g_ref = next(it) if has_norm else None
    w_ref = next(it)
    gm_ref = next(it) if has_group else None
    gain_ref = next(it) if has_group else None
    o_ref = next(it)
    h_ref = next(it)

    @pl.when(pl.program_id(1) == 0)
    def _():
        x = x_ref[...].astype(F32)
        if has_norm:
            ms = jnp.mean(x * x, axis=-1, keepdims=True)
            x = x * lax.rsqrt(ms + EPS) * g_ref[...]
        h_ref[...] = x.astype(BF16)

    y = jnp.dot(h_ref[...], w_ref[...], preferred_element_type=F32)
    if has_group:
        ms = jnp.dot((y * y).astype(BF16), gm_ref[...], preferred_element_type=F32)
        y = y * lax.rsqrt(ms + EPS) * gain_ref[...]
    o_ref[...] = y.astype(o_ref.dtype)


def norm_matmul(x, w, *, norm_g=None, group=None, group_gain=None, out_dtype=BF16, tm=512, tn=None):
    m, k = x.shape
    n = w.shape[1]
    has_norm = norm_g is not None
    has_group = group is not None
    if tn is None:
        tn = 256 if has_group else min(n, 1024)
    tm = min(tm, m)
    assert m % tm == 0 and n % tn == 0
    in_specs = [pl.BlockSpec((tm, k), lambda i, j: (i, 0))]
    args = [x]
    if has_norm:
        in_specs.append(pl.BlockSpec((1, k), lambda i, j: (0, 0)))
        args.append(norm_g.reshape(1, k).astype(F32))
    in_specs.append(pl.BlockSpec((k, tn), lambda i, j: (0, j)))
    args.append(w)
    if has_group:
        gid = jnp.arange(tn) // group
        gm = jnp.where(gid[:, None] == gid[None, :], 1.0 / group, 0.0).astype(BF16)
        in_specs.append(pl.BlockSpec((tn, tn), lambda i, j: (0, 0)))
        args.append(gm)
        in_specs.append(pl.BlockSpec((1, tn), lambda i, j: (0, j)))
        args.append(group_gain.reshape(1, n).astype(F32))
    return pl.pallas_call(
        functools.partial(_norm_matmul_kernel, has_norm=has_norm, has_group=has_group),
        out_shape=jax.ShapeDtypeStruct((m, n), out_dtype),
        grid=(m // tm, n // tn),
        in_specs=in_specs,
        out_specs=pl.BlockSpec((tm, tn), lambda i, j: (i, j)),
        scratch_shapes=[pltpu.VMEM((tm, k), BF16)],
        compiler_params=_cparams(("parallel", "arbitrary")),
        name="norm_matmul",
    )(*args)


def _t5_bucket(n):
    max_exact = REL_BUCKETS // 2
    nf = jnp.maximum(n, 1).astype(F32)
    large = max_exact + (jnp.log(nf / max_exact) / math.log(REL_MAX_DIST / max_exact)
                         * (REL_BUCKETS - max_exact)).astype(jnp.int32)
    large = jnp.minimum(large, REL_BUCKETS - 1)
    return jnp.where(n < max_exact, n, large)


def _rel_bias_kernel(table_ref, o_ref, *, tq):
    hm = pl.program_id(0)
    r = lax.broadcasted_iota(jnp.int32, (tq, 2 * tq), 0)
    c = lax.broadcasted_iota(jnp.int32, (tq, 2 * tq), 1)
    bucket = _t5_bucket(jnp.maximum(r - c + tq, 0))
    acc = jnp.zeros((tq, 2 * tq), F32)
    for b in range(REL_BUCKETS):
        acc = jnp.where(bucket == b, table_ref[b, hm], acc)
    o_ref[0] = acc * LOG2E


def rel_bias_tiles(rel_table, tq):
    nmaps = rel_table.shape[1]
    return pl.pallas_call(
        functools.partial(_rel_bias_kernel, tq=tq),
        out_shape=jax.ShapeDtypeStruct((nmaps, tq, 2 * tq), F32),
        grid=(nmaps,),
        in_specs=[pl.BlockSpec(memory_space=pltpu.SMEM)],
        out_specs=pl.BlockSpec((1, tq, 2 * tq), lambda i: (i, 0, 0)),
        compiler_params=_cparams(("arbitrary",)),
        name="rel_bias_tiles",
    )(rel_table.astype(F32))


def _diff_attn_kernel(scal_ref, q_ref, k_ref, v_ref, bias_ref, subg_ref, o_ref, *, tq, far_bucket_dist):
    del far_bucket_dist
    h = pl.program_id(1)
    i = pl.program_id(2)
    lam = scal_ref[0]
    out_scale = scal_ref[1]
    c1 = scal_ref[2 + 2 * h]
    c2 = scal_ref[3 + 2 * h]

    q = q_ref[...]
    lane = lax.broadcasted_iota(jnp.int32, (tq, LANES), 1)
    zero = jnp.zeros_like(q)
    qq = jnp.concatenate([jnp.where(lane < DIFF_HEAD_DIM, q, zero),
                          jnp.where(lane >= DIFF_HEAD_DIM, q, zero)], axis=0)
    row2 = lax.broadcasted_iota(jnp.int32, (2 * tq, 1), 0)
    cfar = jnp.where(row2 < tq, c1, c2)

    def update(carry, s, vj, shift=None):
        m, l, acc = carry
        rowmax = jnp.max(s, axis=-1, keepdims=True)
        if shift is not None:
            rowmax = rowmax + shift
        m_new = jnp.maximum(m, rowmax)
        alpha = jnp.exp2(m - m_new)
        p = jnp.exp2(s - (m_new if shift is None else m_new - shift))
        l = l * alpha + jnp.sum(p, axis=-1, keepdims=True)
        acc = acc * alpha + jnp.dot(p.astype(BF16), vj, preferred_element_type=F32)
        return m_new, l, acc

    def scores(start, width):
        kj = k_ref[pl.ds(pl.multiple_of(start, tq), width), :]
        vj = v_ref[pl.ds(pl.multiple_of(start, tq), width), :]
        s = lax.dot_general(qq, kj, (((1,), (1,)), ((), ())), preferred_element_type=F32)
        return s, vj

    def far_step(j, carry):
        s, vj = scores(j * (2 * tq), 2 * tq)
        return update(carry, s, vj, cfar)

    def far_single(carry):
        s, vj = scores((i - 2) * tq, tq)
        return update(carry, s, vj, cfar)

    n_far = jnp.maximum(i - 1, 0)
    init = (jnp.full((2 * tq, 1), NEG, F32), jnp.zeros((2 * tq, 1), F32), jnp.zeros((2 * tq, LANES), F32))
    carry = lax.fori_loop(0, n_far // 2, far_step, init)
    carry = lax.cond(n_far % 2 == 1, far_single, lambda c: c, carry)

    b_prev = jnp.concatenate([bias_ref[0, :, 0:tq], bias_ref[1, :, 0:tq]], axis=0)
    s, vj = scores(jnp.maximum(i - 1, 0) * tq, tq)
    s = jnp.where(i >= 1, s + b_prev, NEG)
    carry = update(carry, s, vj)

    b_diag = jnp.concatenate([bias_ref[0, :, tq:2 * tq], bias_ref[1, :, tq:2 * tq]], axis=0)
    s, vj = scores(i * tq, tq)
    rq = lax.broadcasted_iota(jnp.int32, (2 * tq, tq), 0)
    rq = jnp.where(rq >= tq, rq - tq, rq)
    ck = lax.broadcasted_iota(jnp.int32, (2 * tq, tq), 1)
    s = jnp.where(ck <= rq, s + b_diag, NEG)
    m, l, acc = update(carry, s, vj)

    o = acc / l
    o = o[0:tq] - lam * o[tq:2 * tq]
    ms = jnp.mean(o * o, axis=-1, keepdims=True)
    o = o * lax.rsqrt(ms + EPS) * subg_ref[...] * out_scale
    o_ref[...] = o.astype(o_ref.dtype)


def diff_attention(qk, rest, bias_tiles, scal, subln_g, *, b, t, tq, v_col0):
    m = b * t
    nq = t // tq
    nh = N_DIFF_HEADS
    return pl.pallas_call(
        functools.partial(_diff_attn_kernel, tq=tq, far_bucket_dist=None),
        out_shape=jax.ShapeDtypeStruct((m, nh * LANES), BF16),
        grid=(b, nh, nq),
        in_specs=[
            pl.BlockSpec(memory_space=pltpu.SMEM),
            pl.BlockSpec((tq, LANES), lambda bi, h, i: (bi * nq + i, h)),
            pl.BlockSpec((t, LANES), lambda bi, h, i: (bi, nh + h)),
            pl.BlockSpec((t, LANES), lambda bi, h, i: (bi, v_col0 + h)),
            pl.BlockSpec((2, tq, 2 * tq), lambda bi, h, i: (h, 0, 0)),
            pl.BlockSpec((1, LANES), lambda bi, h, i: (0, 0)),
        ],
        out_specs=pl.BlockSpec((tq, LANES), lambda bi, h, i: (bi * nq + i, h)),
        compiler_params=_cparams(("parallel", "parallel", "arbitrary")),
        name="diff_attention",
    )(scal, qk, qk, rest, bias_tiles, subln_g.reshape(1, LANES).astype(F32))


def _sigmoid(x):
    return 1.0 / (1.0 + jnp.exp(-x))


def _softplus(x):
    return jnp.maximum(x, 0.0) + jnp.log(1.0 + jnp.exp(-jnp.abs(x)))


def _delta_kernel(hp_ref, xq_ref, xk_ref, xv_ref, pq_ref, pk_ref, pv_ref, wq_ref, wk_ref, wv_ref,
                  ab_ref, abt_ref, z_ref, og_ref, o_ref, s_ref, *, tt, nhb):
    hb = pl.program_id(1)
    i = pl.program_id(2)
    nc = tt // CHUNK
    dh = DELTA_HEAD_DIM
    heads = range(nhb)

    @pl.when(i == 0)
    def _():
        s_ref[...] = jnp.zeros_like(s_ref)

    def conv_silu(x_ref, p_ref, w_ref):
        prev = jnp.where(i > 0, p_ref[...].astype(F32), 0.0)
        xf = jnp.concatenate([prev, x_ref[...].astype(F32)], axis=0)
        w = w_ref[...]
        base = SUBLANES - (CONV_WIDTH - 1)
        y = xf[base:base + tt] * w[0:1]
        for c in range(1, CONV_WIDTH):
            y = y + xf[base + c:base + c + tt] * w[c:c + 1]
        return y * _sigmoid(y)

    def split(x):
        return [x[:, hd * dh:(hd + 1) * dh] for hd in heads]

    q = split(conv_silu(xq_ref, pq_ref, wq_ref))
    k = split(conv_silu(xk_ref, pk_ref, wk_ref))
    v = split(conv_silu(xv_ref, pv_ref, wv_ref))
    q = [x * lax.rsqrt(jnp.sum(x * x, axis=-1, keepdims=True) + EPS) * (dh ** -0.5) for x in q]
    k = [x * lax.rsqrt(jnp.sum(x * x, axis=-1, keepdims=True) + EPS) for x in k]

    ab = ab_ref[...]
    lane = lax.broadcasted_iota(jnp.int32, ab.shape, 1)
    beta_col, g_col, g_row = [], [], []
    for hd in heads:
        h = hb * nhb + hd
        neg_a = -jnp.exp(jnp.full((1, 1), hp_ref[0, h], F32))
        dt_bias = hp_ref[1, h]
        lb_col = jnp.sum(jnp.where(lane == h, ab, 0.0), axis=-1, keepdims=True)
        la_col = jnp.sum(jnp.where(lane == N_DELTA_HEADS + h, ab, 0.0), axis=-1, keepdims=True)
        la_row = abt_ref[pl.ds(N_DELTA_HEADS + h, 1), :]
        beta_col.append(_sigmoid(lb_col))
        g_col.append(neg_a * _softplus(la_col + dt_bias))
        g_row.append(neg_a * _softplus(la_row + dt_bias))

    ri = lax.broadcasted_iota(jnp.int32, (tt, tt), 0)
    ci = lax.broadcasted_iota(jnp.int32, (tt, tt), 1)
    same = (ri // CHUNK) == (ci // CHUNK)
    tril = same & (ri >= ci)
    strict = same & (ri > ci)
    triu = same & (ri <= ci)

    def mm(a, b):
        return jnp.dot(a.astype(BF16), b.astype(BF16), preferred_element_type=F32)

    def mm_nt(a, b):
        return lax.dot_general(a.astype(BF16), b.astype(BF16), (((1,), (1,)), ((), ())),
                               preferred_element_type=F32)

    def mm_tn(a, b):
        return lax.dot_general(a.astype(BF16), b.astype(BF16), (((0,), (0,)), ((), ())),
                               preferred_element_type=F32)

    gc_col = [jnp.sum(jnp.where(tril, g, 0.0), axis=-1, keepdims=True) for g in g_row]
    gc_row = [jnp.sum(jnp.where(triu, g, 0.0), axis=0, keepdims=True) for g in g_col]
    decay = [jnp.where(tril, jnp.exp(jnp.where(tril, a - b, 0.0)), 0.0) for a, b in zip(gc_col, gc_row)]
    e_gc = [jnp.exp(g) for g in gc_col]
    kk = [mm_nt(x, x) for x in k]
    qk = [mm_nt(a, b) for a, b in zip(q, k)]
    xm = [jnp.where(strict, -(b * kx * d), 0.0) for b, kx, d in zip(beta_col, kk, decay)]
    intra = [jnp.where(tril, a * d, 0.0) for a, d in zip(qk, decay)]
    r = [jnp.concatenate([b * vx, (b * e) * kx], axis=-1) for b, vx, e, kx in zip(beta_col, v, e_gc, k)]
    n_sq = CHUNK.bit_length() - 1
    for lvl in range(n_sq):
        r = [x + mm(m_, x) for m_, x in zip(xm, r)]
        if lvl + 1 < n_sq:
            xm = [mm(m_, m_) for m_ in xm]
    u = [x[:, 0:dh] for x in r]
    w = [x[:, dh:2 * dh] for x in r]
    qg = [a * e for a, e in zip(q, e_gc)]

    state = [s_ref[hd] for hd in heads]
    o_state = [[] for _ in heads]
    v_new = [[] for _ in heads]
    for c in range(nc):
        r0 = c * CHUNK
        sl = slice(r0, r0 + CHUNK)
        g_last = [g[r0 + CHUNK - 1:r0 + CHUNK] for g in gc_col]
        vn = [u[hd][sl] - mm(w[hd][sl], state[hd]) for hd in heads]
        os_ = [mm(qg[hd][sl], state[hd]) for hd in heads]
        k_dec = [k[hd][sl] * jnp.exp(g_last[hd] - gc_col[hd][sl]) for hd in heads]
        state = [state[hd] * jnp.exp(g_last[hd]) + mm_tn(k_dec[hd], vn[hd]) for hd in heads]
        for hd in heads:
            v_new[hd].append(vn[hd])
            o_state[hd].append(os_[hd])
    for hd in heads:
        s_ref[hd] = state[hd]

    outs = []
    for hd in heads:
        o = jnp.concatenate(o_state[hd], axis=0) + mm(intra[hd], jnp.concatenate(v_new[hd], axis=0))
        ms = jnp.mean(o * o, axis=-1, keepdims=True)
        outs.append(o * lax.rsqrt(ms + EPS) * og_ref[...])
    z = z_ref[...].astype(F32)
    o_ref[...] = (jnp.concatenate(outs, axis=-1) * (z * _sigmoid(z))).astype(o_ref.dtype)


def gated_deltanet(rest, ab, abt, conv_w, head_params, out_g, *, b, t, tt, nhb, qkv_col0, z_col0):
    m = b * t
    nt = t // tt
    nh = N_DELTA_HEADS
    sub = tt // SUBLANES
    wl = nhb * LANES
    assert nh % nhb == 0 and qkv_col0 % nhb == 0 and z_col0 % nhb == 0

    def cur(off):
        return pl.BlockSpec((tt, wl), lambda bi, h, i: (bi * nt + i, off // nhb + h))

    def prev(off):
        return pl.BlockSpec((SUBLANES, wl),
                            lambda bi, h, i: (jnp.maximum((bi * nt + i) * sub - 1, 0), off // nhb + h))

    def wspec(off):
        return pl.BlockSpec((CONV_WIDTH, wl), lambda bi, h, i: (0, off // nhb + h))

    return pl.pallas_call(
        functools.partial(_delta_kernel, tt=tt, nhb=nhb),
        out_shape=jax.ShapeDtypeStruct((m, nh * LANES), BF16),
        grid=(b, nh // nhb, nt),
        in_specs=[
            pl.BlockSpec(memory_space=pltpu.SMEM),
            cur(qkv_col0), cur(qkv_col0 + nh), cur(qkv_col0 + 2 * nh),
            prev(qkv_col0), prev(qkv_col0 + nh), prev(qkv_col0 + 2 * nh),
            wspec(0), wspec(nh), wspec(2 * nh),
            pl.BlockSpec((tt, LANES), lambda bi, h, i: (bi * nt + i, 0)),
            pl.BlockSpec((2 * nh, tt), lambda bi, h, i: (0, bi * nt + i)),
            cur(z_col0),
            pl.BlockSpec((1, LANES), lambda bi, h, i: (0, 0)),
        ],
        out_specs=pl.BlockSpec((tt, wl), lambda bi, h, i: (bi * nt + i, h)),
        scratch_shapes=[pltpu.VMEM((nhb, DELTA_HEAD_DIM, DELTA_HEAD_DIM), F32)],
        compiler_params=_cparams(("parallel", "parallel", "arbitrary")),
        name="gated_deltanet",
    )(head_params, rest, rest, rest, rest, rest, rest, conv_w, conv_w, conv_w, ab, abt, rest,
      out_g.reshape(1, LANES).astype(F32))


def _mem_attn_kernel(q_ref, k_ref, v_ref, o_ref):
    outs = []
    for h in range(N_MEM_HEADS):
        sl = slice(h * MEM_HEAD_DIM, (h + 1) * MEM_HEAD_DIM)
        s = lax.dot_general(q_ref[:, sl], k_ref[:, sl], (((1,), (1,)), ((), ())), preferred_element_type=F32)
        s = s - jnp.max(s, axis=-1, keepdims=True)
        p = jnp.exp(s)
        p = p / jnp.sum(p, axis=-1, keepdims=True)
        outs.append(jnp.dot(p.astype(BF16), v_ref[:, sl], preferred_element_type=F32))
    o_ref[...] = jnp.concatenate(outs, axis=-1).astype(o_ref.dtype)


def memory_attention(mq, mk, mv, *, b, t, mtok, tq):
    nq = t // tq
    w = N_MEM_HEADS * MEM_HEAD_DIM
    return pl.pallas_call(
        _mem_attn_kernel,
        out_shape=jax.ShapeDtypeStruct((b * t, w), BF16),
        grid=(b, nq),
        in_specs=[
            pl.BlockSpec((tq, w), lambda bi, i: (bi * nq + i, 0)),
            pl.BlockSpec((mtok, w), lambda bi, i: (bi, 0)),
            pl.BlockSpec((mtok, w), lambda bi, i: (bi, 0)),
        ],
        out_specs=pl.BlockSpec((tq, w), lambda bi, i: (bi * nq + i, 0)),
        compiler_params=_cparams(("parallel", "arbitrary")),
        name="memory_attention",
    )(mq, mk, mv)


def _merge_kernel(x_ref, ya_ref, yb_ref, yc_ref, ga_ref, gb_ref, gc_ref, bg_ref,
                  wa_ref, wb_ref, wc_ref, wo_ref, fg_ref, x1_ref, h2_ref):
    def branch(y_ref, g_ref, w_ref, idx):
        gate = _sigmoid(g_ref[...].astype(F32) + bg_ref[idx:idx + 1, :])
        return gate * jnp.dot(y_ref[...], w_ref[...], preferred_element_type=F32)

    merged = branch(ya_ref, ga_ref, wa_ref, 0) + branch(yb_ref, gb_ref, wb_ref, 1) + branch(yc_ref, gc_ref, wc_ref, 2)
    x1 = x_ref[...] + jnp.dot(merged.astype(BF16), wo_ref[...], preferred_element_type=F32)
    x1_ref[...] = x1
    ms = jnp.mean(x1 * x1, axis=-1, keepdims=True)
    h2_ref[...] = (x1 * lax.rsqrt(ms + EPS) * fg_ref[...]).astype(h2_ref.dtype)


def merge_project(x2d, y_diff, y_delta, y_mem, rest, b_gate, w_a, w_b, w_c, w_o, ffn_g, *, gate_col0, tm):
    m, d = x2d.shape
    row = lambda i: (i, 0)
    const = lambda i: (0, 0)
    assert gate_col0 % d == 0

    def gspec(j):
        return pl.BlockSpec((tm, d), lambda i: (i, gate_col0 // d + j))

    return pl.pallas_call(
        _merge_kernel,
        out_shape=(jax.ShapeDtypeStruct((m, d), F32), jax.ShapeDtypeStruct((m, d), BF16)),
        grid=(m // tm,),
        in_specs=[
            pl.BlockSpec((tm, d), row), pl.BlockSpec((tm, d), row), pl.BlockSpec((tm, d), row),
            pl.BlockSpec((tm, d), row), gspec(0), gspec(1), gspec(2),
            pl.BlockSpec((3, d), const),
            pl.BlockSpec((d, d), const), pl.BlockSpec((d, d), const), pl.BlockSpec((d, d), const),
            pl.BlockSpec((d, d), const), pl.BlockSpec((1, d), const),
        ],
        out_specs=(pl.BlockSpec((tm, d), row), pl.BlockSpec((tm, d), row)),
        compiler_params=_cparams(("parallel",)),
        name="merge_project",
    )(x2d, y_diff, y_delta, y_mem, rest, rest, rest, b_gate.reshape(3, d).astype(F32),
      w_a, w_b, w_c, w_o, ffn_g.reshape(1, d).astype(F32))


def _topk_rows(s, key, extra, k):
    big = jnp.iinfo(jnp.int32).max
    vals, keys, ext = [], [], []
    for _ in range(k):
        m = jnp.max(s, axis=0, keepdims=True)
        am = jnp.min(jnp.where(s == m, key, big), axis=0, keepdims=True)
        hit = key == am
        if extra is not None:
            ext.append(jnp.max(jnp.where(hit, extra, -1), axis=0, keepdims=True))
        s = jnp.where(hit, -jnp.inf, s)
        vals.append(m)
        keys.append(am)
    cat = lambda xs: jnp.concatenate(xs, axis=0)
    return cat(vals), cat(keys), (cat(ext) if extra is not None else None)


def _peer_route_kernel(q_ref, keys_ref, idx_ref, gate_ref):
    k = PEER_TOPK
    q = q_ref[...]
    tt = q.shape[0]

    def half(p):
        qp = q[:, p * PEER_HALF:(p + 1) * PEER_HALF]
        st = lax.dot_general(keys_ref[0, p], qp, (((1,), (1,)), ((), ())), preferred_element_type=F32)
        v, r, _ = _topk_rows(st, lax.broadcasted_iota(jnp.int32, st.shape, 0), None, k)
        return v, r

    s1, i1 = half(0)
    s2, i2 = half(1)
    g = SUBLANES
    sub = lax.broadcasted_iota(jnp.int32, (g, tt), 0)
    e1 = i1 * PEER_KEYS
    pieces = []
    for b in range(g):
        lim = min(g, k // (b + 1))
        sc = s1[0:g] + s2[b:b + 1]
        if lim < g:
            sc = jnp.where(sub < lim, sc, -jnp.inf)
        pieces.append((sc, sub * k + b, e1[0:g] + i2[b:b + 1]))
    pieces.append((s1[g:k] + s2[0:1], (sub + g) * k, e1[g:k] + i2[0:1]))
    pieces.append((s1[0:1] + s2[g:k], sub + g, e1[0:1] + i2[g:k]))
    cand_s = jnp.concatenate([p[0] for p in pieces], axis=0)
    cand_k = jnp.concatenate([p[1] for p in pieces], axis=0)
    cand_e = jnp.concatenate([p[2] for p in pieces], axis=0)
    top_s, _, top_e = _topk_rows(cand_s, cand_k, cand_e, k)
    e = jnp.exp(top_s - top_s[0:1])
    gate_ref[0] = e / jnp.sum(e, axis=0, keepdims=True)
    idx_ref[0] = top_e


def peer_route(qry, sub_keys, *, tt):
    m = qry.shape[0]
    nh = PEER_HEADS
    return pl.pallas_call(
        _peer_route_kernel,
        out_shape=(jax.ShapeDtypeStruct((nh, PEER_TOPK, m), jnp.int32),
                   jax.ShapeDtypeStruct((nh, PEER_TOPK, m), F32)),
        grid=(m // tt, nh),
        in_specs=[
            pl.BlockSpec((tt, 2 * PEER_HALF), lambda i, h: (i, h)),
            pl.BlockSpec((1, 2, PEER_KEYS, PEER_HALF), lambda i, h: (h, 0, 0, 0)),
        ],
        out_specs=(pl.BlockSpec((1, PEER_TOPK, tt), lambda i, h: (h, 0, i)),
                   pl.BlockSpec((1, PEER_TOPK, tt), lambda i, h: (h, 0, i))),
        compiler_params=_cparams(("parallel", "arbitrary")),
        name="peer_route",
    )(qry, sub_keys)


def _gelu_exact(x):
    return 0.5 * x * (1.0 + lax.erf(x * (2.0 ** -0.5)))


def _peer_expert_kernel(idx_ref, idxn_ref, gate_ref, h_ref, x_ref, pool_ref, poolt_ref, uv_hbm,
                        o_ref, *scratch, tb, ne, nbuf, dist):
    bufs, sem = scratch[:nbuf], scratch[nbuf]
    i = pl.program_id(0)
    n = pl.num_programs(0)
    rows = tb * ne
    d_sub = h_ref.shape[1]
    n_seg = 2 * tb
    seg = rows // n_seg

    def issue_rows(iref, row0, buf, which, r_lo, r_hi):
        for r in range(r_lo, r_hi):
            e = iref[row0 + r // ne, r % ne]
            pltpu.make_async_copy(uv_hbm.at[e], buf.at[r], sem.at[which, r % 2]).start(priority=r % 2)

    def wait_tile(buf, which):
        for k in range(2):
            pltpu.make_async_copy(uv_hbm.at[pl.ds(0, rows // 2)], buf.at[pl.ds(0, rows // 2)],
                                  sem.at[which, k]).wait()

    @pl.when(i == 0)
    def _():
        for a in range(dist):
            def tok(t, carry, a=a):
                for j in range(ne):
                    pltpu.make_async_copy(uv_hbm.at[idx_ref[a * tb + t, j]], bufs[a].at[t * ne + j],
                                          sem.at[a, j % 2]).start(priority=j % 2)
                return carry
            lax.fori_loop(0, tb, tok, 0)

    blk = 2 * d_sub
    sub_id = lax.broadcasted_iota(jnp.int32, (d_sub, ne * blk), 0)
    col_id = lax.broadcasted_iota(jnp.int32, (d_sub, ne * blk), 1)
    diag_u = (col_id % blk) == sub_id
    diag_v = (col_id % blk) == sub_id + d_sub

    def tile(row0, buf, nxt_iref, nxt_row0, nxt_buf, nxt_which):
        drows = []
        for t in range(tb):
            z = buf[pl.ds(t * ne, ne)].reshape(ne * blk, LANES)
            c = lax.dot_general(h_ref[row0 + t], z, (((1,), (1,)), ((), ())), preferred_element_type=F32)
            drows.append(jnp.sum(jnp.where(diag_u, c, 0.0), axis=0, keepdims=True))
            issue_rows(nxt_iref, nxt_row0, nxt_buf, nxt_which, t * seg, (t + 1) * seg)
        dall = jnp.concatenate(drows, axis=0)
        d_hi = dall.astype(BF16)
        d_lo = (dall - d_hi.astype(F32)).astype(BF16)
        a = (jnp.dot(d_hi, pool_ref[...], preferred_element_type=F32)
             + jnp.dot(d_lo, pool_ref[...], preferred_element_type=F32))
        w = gate_ref[pl.ds(row0, tb), :] * _gelu_exact(a)
        wrep = jnp.dot(w.astype(BF16), poolt_ref[...], preferred_element_type=F32)
        for t in range(tb):
            wexp = jnp.where(diag_v, wrep[t:t + 1, :], 0.0).astype(BF16)
            z = buf[pl.ds(t * ne, ne)].reshape(ne * blk, LANES)
            o_ref[row0 + t] = x_ref[row0 + t] + jnp.dot(wexp, z, preferred_element_type=F32)
            issue_rows(nxt_iref, nxt_row0, nxt_buf, nxt_which, (tb + t) * seg, (tb + t + 1) * seg)

    for a in range(nbuf):
        wait_tile(bufs[a], a)
        nxt = a + dist
        if nxt < nbuf:
            tile(a * tb, bufs[a], idx_ref, nxt * tb, bufs[nxt], nxt)
        else:
            tile(a * tb, bufs[a], idxn_ref, (nxt - nbuf) * tb, bufs[nxt - nbuf], nxt - nbuf)

    @pl.when(i == n - 1)
    def _():
        for a in range(dist):
            wait_tile(bufs[a], a)


def peer_experts(idx, gates, h3, x3, uv3, *, tb):
    m, ne = idx.shape
    d_sub = h3.shape[1]
    nbuf, dist = 4, 2
    ts = nbuf * tb
    n = m // ts
    rows = tb * ne
    blk = 2 * d_sub
    cid = jnp.arange(ne * blk) // blk
    pool = (cid[:, None] == jnp.arange(ne)[None, :]).astype(BF16)
    tok = lambda i: (i, 0)
    tok3 = lambda i: (i, 0, 0)
    return pl.pallas_call(
        functools.partial(_peer_expert_kernel, tb=tb, ne=ne, nbuf=nbuf, dist=dist),
        out_shape=jax.ShapeDtypeStruct(x3.shape, F32),
        grid=(n,),
        in_specs=[
            pl.BlockSpec((ts, ne), tok, memory_space=pltpu.SMEM),
            pl.BlockSpec((ts, ne), lambda i: (jnp.minimum(i + 1, n - 1), 0), memory_space=pltpu.SMEM),
            pl.BlockSpec((ts, ne), tok),
            pl.BlockSpec((ts, d_sub, LANES), tok3),
            pl.BlockSpec((ts, d_sub, LANES), tok3),
            pl.BlockSpec((ne * blk, ne), lambda i: (0, 0)),
            pl.BlockSpec((ne, ne * blk), lambda i: (0, 0)),
            pl.BlockSpec(memory_space=pl.ANY),
        ],
        out_specs=pl.BlockSpec((ts, d_sub, LANES), tok3),
        scratch_shapes=[pltpu.VMEM((rows, blk, LANES), BF16) for _ in range(nbuf)]
        + [pltpu.SemaphoreType.DMA((nbuf, 2))],
        compiler_params=_cparams(("arbitrary",)),
        name="peer_experts",
    )(idx, idx, gates, h3, x3, pool, pool.T, uv3)


def kernel(x, mem, positions, attn_norm_g, mem_norm_g, w_in, b_gate, diff_q_norm_g, diff_k_norm_g, lambda_q1, lambda_k1, lambda_q2, lambda_k2, diff_subln_g, rel_bias_table, conv_w, a_log, dt_bias, delta_out_norm_g, w_mem_kv, mem_q_norm_g, mem_k_norm_g, w_br_diff, w_br_delta, w_br_mem, w_out, ffn_norm_g, w_query, sub_keys, expert_u, expert_v):
    del positions
    b, t, d = x.shape
    mtok = mem.shape[1]
    m = b * t
    depth = w_in.shape[0]
    nh = N_DIFF_HEADS
    qk_w = nh * 2 * DIFF_HEAD_DIM
    dv_w = nh * 2 * DIFF_HEAD_DIM
    dl_w = N_DELTA_HEADS * DELTA_HEAD_DIM
    mem_w = N_MEM_HEADS * MEM_HEAD_DIM
    tq = min(256, t)
    tt_delta = min(256, t)

    x2d = x.reshape(m, d)
    for l in range(depth):
        lam_init = 0.8 - 0.6 * math.exp(-0.3 * l)
        wl = w_in[l]
        o = 0
        w_qk = wl[:, o:o + 2 * qk_w]; o += 2 * qk_w
        w_dv = wl[:, o:o + dv_w]; o += dv_w
        w_lqkv = wl[:, o:o + 3 * dl_w]; o += 3 * dl_w
        w_lz = wl[:, o:o + dl_w]; o += dl_w
        w_ab = wl[:, o:o + 2 * N_DELTA_HEADS]; o += 2 * N_DELTA_HEADS
        w_mq = wl[:, o:o + mem_w]; o += mem_w
        w_gate = wl[:, o:o + 3 * d]

        qk_gain = jnp.concatenate([jnp.tile(diff_q_norm_g[l], 2 * nh) * (DIFF_HEAD_DIM ** -0.5 * LOG2E),
                                   jnp.tile(diff_k_norm_g[l], 2 * nh)])
        qk = norm_matmul(x2d, w_qk.astype(BF16), norm_g=attn_norm_g[l], group=DIFF_HEAD_DIM, group_gain=qk_gain)
        w_rest = jnp.concatenate([w_dv, w_lqkv, w_lz, w_gate], axis=1).astype(BF16)
        rest = norm_matmul(x2d, w_rest, norm_g=attn_norm_g[l])
        mq_gain = jnp.tile(mem_q_norm_g[l], N_MEM_HEADS) * (MEM_HEAD_DIM ** -0.5)
        mq = norm_matmul(x2d, w_mq.astype(BF16), norm_g=attn_norm_g[l], group=MEM_HEAD_DIM, group_gain=mq_gain)
        w_ab_pad = jnp.pad(w_ab, ((0, 0), (0, LANES - 2 * N_DELTA_HEADS))).astype(BF16)
        ab = norm_matmul(x2d, w_ab_pad, norm_g=attn_norm_g[l], out_dtype=F32, tn=LANES)
        abt = ab[:, :2 * N_DELTA_HEADS].T

        lam = (jnp.exp(jnp.sum(lambda_q1[l].astype(F32) * lambda_k1[l].astype(F32)))
               - jnp.exp(jnp.sum(lambda_q2[l].astype(F32) * lambda_k2[l].astype(F32))) + lam_init)
        bias_tiles = rel_bias_tiles(rel_bias_table, tq)
        scal = jnp.concatenate([jnp.stack([lam, jnp.asarray(1.0 - lam_init, F32)]),
                                rel_bias_table[REL_BUCKETS - 1].astype(F32) * LOG2E])
        y_diff = diff_attention(qk, rest, bias_tiles, scal, diff_subln_g[l], b=b, t=t, tq=tq, v_col0=0)

        head_params = jnp.stack([a_log[l], dt_bias[l]]).astype(F32)
        y_delta = gated_deltanet(rest, ab, abt, conv_w[l].astype(F32), head_params, delta_out_norm_g[l],
                                 b=b, t=t, tt=tt_delta, nhb=8, qkv_col0=dv_w // LANES,
                                 z_col0=(dv_w + 3 * dl_w) // LANES)

        mem2d = mem.reshape(b * mtok, d)
        wkv = w_mem_kv[l].astype(BF16)
        mk = norm_matmul(mem2d, wkv[:, :mem_w], norm_g=mem_norm_g[l], group=MEM_HEAD_DIM,
                         group_gain=jnp.tile(mem_k_norm_g[l], N_MEM_HEADS))
        mv = norm_matmul(mem2d, wkv[:, mem_w:], norm_g=mem_norm_g[l])
        y_mem = memory_attention(mq, mk, mv, b=b, t=t, mtok=mtok, tq=min(512, t))

        x1, h2 = merge_project(x2d, y_diff, y_delta, y_mem, rest, b_gate[l],
                               w_br_diff[l].astype(BF16), w_br_delta[l].astype(BF16), w_br_mem[l].astype(BF16),
                               w_out[l].astype(BF16), ffn_norm_g[l],
                               gate_col0=dv_w + 4 * dl_w, tm=min(512, m))

        qry = norm_matmul(h2, w_query[l].astype(BF16))
        idx_t, gate_t = peer_route(qry, sub_keys[l].astype(BF16), tt=min(512, m))
        ne = PEER_HEADS * PEER_TOPK
        idx = idx_t.reshape(ne, m).T
        gates = gate_t.reshape(ne, m).T
        sub = d // LANES
        uv3 = jnp.concatenate([expert_u[l].reshape(-1, sub, LANES), expert_v[l].reshape(-1, sub, LANES)],
                              axis=1).astype(BF16)
        x2 = peer_experts(idx, gates, h2.reshape(m, sub, LANES), x1.reshape(m, sub, LANES), uv3, tb=8)
        x2d = x2.reshape(m, d)
    return x2d.reshape(b, t, d)
```

```python
import functools
import math

import jax
import jax.numpy as jnp
from jax import lax
from jax.experimental import pallas as pl
from jax.experimental.pallas import tpu as pltpu

F32 = jnp.float32
BF16 = jnp.bfloat16
EPS = 1e-6
NEG = -1e30
LOG2E = math.log2(math.e)

N_DIFF_HEADS = 8
DIFF_HEAD_DIM = 64
N_DELTA_HEADS = 8
DELTA_HEAD_DIM = 128
CONV_WIDTH = 4
CHUNK = 64
N_MEM_HEADS = 4
MEM_HEAD_DIM = 256
REL_BUCKETS = 32
REL_MAX_DIST = 128
PEER_HEADS = 8
PEER_KEYS = 128
PEER_TOPK = 16
PEER_HALF = 128
LANES = 128
SUBLANES = 8
VMEM_LIMIT = 56 * 1024 * 1024


def _cparams(sem):
    return pltpu.CompilerParams(dimension_semantics=sem, vmem_limit_bytes=VMEM_LIMIT)


def _norm_matmul_kernel(*refs, has_norm, has_group):
    it = iter(refs)
    x_ref = next(it)
    g_ref = next(it) if has_norm else None
    w_ref = next(it)
    gm_ref = next(it) if has_group else None
    gain_ref = next(it) if has_group else None
    o_ref = next(it)
    h_ref = next(it)

    @pl.when(pl.program_id(1) == 0)
    def _():
        x = x_ref[...].astype(F32)
        if has_norm:
            ms = jnp.mean(x * x, axis=-1, keepdims=True)
            x = x * lax.rsqrt(ms + EPS) * g_ref[...]
        h_ref[...] = x.astype(BF16)

    y = jnp.dot(h_ref[...], w_ref[...], preferred_element_type=F32)
    if has_group:
        ms = jnp.dot((y * y).astype(BF16), gm_ref[...], preferred_element_type=F32)
        y = y * lax.rsqrt(ms + EPS) * gain_ref[...]
    o_ref[...] = y.astype(o_ref.dtype)


def norm_matmul(x, w, *, norm_g=None, group=None, group_gain=None, out_dtype=BF16, tm=1024, tn=None):
    m, k = x.shape
    n = w.shape[1]
    has_norm = norm_g is not None
    has_group = group is not None
    if tn is None:
        tn = 512 if has_group else min(n, 2048)
    tm = min(tm, m)
    assert m % tm == 0 and n % tn == 0
    in_specs = [pl.BlockSpec((tm, k), lambda i, j: (i, 0))]
    args = [x]
    if has_norm:
        in_specs.append(pl.BlockSpec((1, k), lambda i, j: (0, 0)))
        args.append(norm_g.reshape(1, k).astype(F32))
    in_specs.append(pl.BlockSpec((k, tn), lambda i, j: (0, j)))
    args.append(w)
    if has_group:
        gid = jnp.arange(tn) // group
        gm = jnp.where(gid[:, None] == gid[None, :], 1.0 / group, 0.0).astype(BF16)
        in_specs.append(pl.BlockSpec((tn, tn), lambda i, j: (0, 0)))
        args.append(gm)
        in_specs.append(pl.BlockSpec((1, tn), lambda i, j: (0, j)))
        args.append(group_gain.reshape(1, n).astype(F32))
    return pl.pallas_call(
        functools.partial(_norm_matmul_kernel, has_norm=has_norm, has_group=has_group),
        out_shape=jax.ShapeDtypeStruct((m, n), out_dtype),
        grid=(m // tm, n // tn),
        in_specs=in_specs,
        out_specs=pl.BlockSpec((tm, tn), lambda i, j: (i, j)),
        scratch_shapes=[pltpu.VMEM((tm, k), BF16)],
        compiler_params=_cparams(("parallel", "arbitrary")),
        name="norm_matmul",
    )(*args)


def _t5_bucket(n):
    max_exact = REL_BUCKETS // 2
    nf = jnp.maximum(n, 1).astype(F32)
    large = max_exact + (jnp.log(nf / max_exact) / math.log(REL_MAX_DIST / max_exact)
                         * (REL_BUCKETS - max_exact)).astype(jnp.int32)
    large = jnp.minimum(large, REL_BUCKETS - 1)
    return jnp.where(n < max_exact, n, large)


def _rel_bias_kernel(table_ref, o_ref, *, tq):
    hm = pl.program_id(0)
    r = lax.broadcasted_iota(jnp.int32, (tq, 2 * tq), 0)
    c = lax.broadcasted_iota(jnp.int32, (tq, 2 * tq), 1)
    bucket = _t5_bucket(jnp.maximum(r - c + tq, 0))
    acc = jnp.zeros((tq, 2 * tq), F32)
    for b in range(REL_BUCKETS):
        acc = jnp.where(bucket == b, table_ref[b, hm], acc)
    o_ref[0] = acc * LOG2E


def rel_bias_tiles(rel_table, tq):
    nmaps = rel_table.shape[1]
    return pl.pallas_call(
        functools.partial(_rel_bias_kernel, tq=tq),
        out_shape=jax.ShapeDtypeStruct((nmaps, tq, 2 * tq), F32),
        grid=(nmaps,),
        in_specs=[pl.BlockSpec(memory_space=pltpu.SMEM)],
        out_specs=pl.BlockSpec((1, tq, 2 * tq), lambda i: (i, 0, 0)),
        compiler_params=_cparams(("arbitrary",)),
        name="rel_bias_tiles",
    )(rel_table.astype(F32))


def _diff_attn_kernel(scal_ref, q_ref, k_ref, v_ref, bias_ref, subg_ref, o_ref, *, tq):
    h = pl.program_id(1)
    i = pl.program_id(2)
    lam = scal_ref[0]
    out_scale = scal_ref[1]
    c1 = scal_ref[2 + 2 * h]
    c2 = scal_ref[3 + 2 * h]

    q = q_ref[...]
    lane = lax.broadcasted_iota(jnp.int32, (tq, LANES), 1)
    zero = jnp.zeros_like(q)
    qq = jnp.concatenate([jnp.where(lane < DIFF_HEAD_DIM, q, zero),
                          jnp.where(lane >= DIFF_HEAD_DIM, q, zero)], axis=0)
    row2 = lax.broadcasted_iota(jnp.int32, (2 * tq, 1), 0)
    cfar = jnp.where(row2 < tq, c1, c2)

    def update(carry, s, vj, shift=None):
        m, l, acc = carry
        rowmax = jnp.max(s, axis=-1, keepdims=True)
        if shift is not None:
            rowmax = rowmax + shift
        m_new = jnp.maximum(m, rowmax)
        alpha = jnp.exp2(m - m_new)
        p = jnp.exp2(s - (m_new if shift is None else m_new - shift))
        l = l * alpha + jnp.sum(p, axis=-1, keepdims=True)
        acc = acc * alpha + jnp.dot(p.astype(BF16), vj, preferred_element_type=F32)
        return m_new, l, acc

    def scores(start, width):
        kj = k_ref[pl.ds(pl.multiple_of(start, tq), width), :]
        vj = v_ref[pl.ds(pl.multiple_of(start, tq), width), :]
        s = lax.dot_general(qq, kj, (((1,), (1,)), ((), ())), preferred_element_type=F32)
        return s, vj

    def far_step(j, carry):
        s, vj = scores(j * (2 * tq), 2 * tq)
        return update(carry, s, vj, cfar)

    def far_single(carry):
        s, vj = scores((i - 2) * tq, tq)
        return update(carry, s, vj, cfar)

    n_far = jnp.maximum(i - 1, 0)
    init = (jnp.full((2 * tq, 1), NEG, F32), jnp.zeros((2 * tq, 1), F32), jnp.zeros((2 * tq, LANES), F32))
    carry = lax.fori_loop(0, n_far // 2, far_step, init)
    carry = lax.cond(n_far % 2 == 1, far_single, lambda c: c, carry)

    b_prev = jnp.concatenate([bias_ref[0, :, 0:tq], bias_ref[1, :, 0:tq]], axis=0)
    s, vj = scores(jnp.maximum(i - 1, 0) * tq, tq)
    s = jnp.where(i >= 1, s + b_prev, NEG)
    carry = update(carry, s, vj)

    b_diag = jnp.concatenate([bias_ref[0, :, tq:2 * tq], bias_ref[1, :, tq:2 * tq]], axis=0)
    s, vj = scores(i * tq, tq)
    rq = lax.broadcasted_iota(jnp.int32, (2 * tq, tq), 0)
    rq = jnp.where(rq >= tq, rq - tq, rq)
    ck = lax.broadcasted_iota(jnp.int32, (2 * tq, tq), 1)
    s = jnp.where(ck <= rq, s + b_diag, NEG)
    m, l, acc = update(carry, s, vj)

    o = acc / l
    o = o[0:tq] - lam * o[tq:2 * tq]
    ms = jnp.mean(o * o, axis=-1, keepdims=True)
    o = o * lax.rsqrt(ms + EPS) * subg_ref[...] * out_scale
    o_ref[...] = o.astype(o_ref.dtype)


def diff_attention(qk, rest, bias_tiles, scal, subln_g, *, b, t, tq, v_col0):
    m = b * t
    nq = t // tq
    nh = N_DIFF_HEADS
    return pl.pallas_call(
        functools.partial(_diff_attn_kernel, tq=tq),
        out_shape=jax.ShapeDtypeStruct((m, nh * LANES), BF16),
        grid=(b, nh, nq),
        in_specs=[
            pl.BlockSpec(memory_space=pltpu.SMEM),
            pl.BlockSpec((tq, LANES), lambda bi, h, i: (bi * nq + i, h)),
            pl.BlockSpec((t, LANES), lambda bi, h, i: (bi, nh + h)),
            pl.BlockSpec((t, LANES), lambda bi, h, i: (bi, v_col0 + h)),
            pl.BlockSpec((2, tq, 2 * tq), lambda bi, h, i: (h, 0, 0)),
            pl.BlockSpec((1, LANES), lambda bi, h, i: (0, 0)),
        ],
        out_specs=pl.BlockSpec((tq, LANES), lambda bi, h, i: (bi * nq + i, h)),
        compiler_params=_cparams(("parallel", "parallel", "arbitrary")),
        name="diff_attention",
    )(scal, qk, qk, rest, bias_tiles, subln_g.reshape(1, LANES).astype(F32))


def _sigmoid(x):
    return 1.0 / (1.0 + jnp.exp(-x))


def _softplus(x):
    return jnp.maximum(x, 0.0) + jnp.log(1.0 + jnp.exp(-jnp.abs(x)))


def _delta_kernel(hp_ref, xq_ref, xk_ref, xv_ref, pq_ref, pk_ref, pv_ref, wq_ref, wk_ref, wv_ref,
                  ab_ref, abt_ref, z_ref, og_ref, o_ref, s_ref, *, tt, nhb):
    hb = pl.program_id(1)
    i = pl.program_id(2)
    nc = tt // CHUNK
    dh = DELTA_HEAD_DIM
    heads = range(nhb)

    @pl.when(i == 0)
    def _():
        s_ref[...] = jnp.zeros_like(s_ref)

    def conv_silu(x_ref, p_ref, w_ref):
        prev = jnp.where(i > 0, p_ref[...].astype(F32), 0.0)
        xf = jnp.concatenate([prev, x_ref[...].astype(F32)], axis=0)
        w = w_ref[...]
        base = SUBLANES - (CONV_WIDTH - 1)
        y = xf[base:base + tt] * w[0:1]
        for c in range(1, CONV_WIDTH):
            y = y + xf[base + c:base + c + tt] * w[c:c + 1]
        return y * _sigmoid(y)

    def split(x):
        return [x[:, hd * dh:(hd + 1) * dh] for hd in heads]

    q = split(conv_silu(xq_ref, pq_ref, wq_ref))
    k = split(conv_silu(xk_ref, pk_ref, wk_ref))
    v = split(conv_silu(xv_ref, pv_ref, wv_ref))
    q = [x * lax.rsqrt(jnp.sum(x * x, axis=-1, keepdims=True) + EPS) * (dh ** -0.5) for x in q]
    k = [x * lax.rsqrt(jnp.sum(x * x, axis=-1, keepdims=True) + EPS) for x in k]

    ab = ab_ref[...]
    lane = lax.broadcasted_iota(jnp.int32, ab.shape, 1)
    beta_col, g_col, g_row = [], [], []
    for hd in heads:
        h = hb * nhb + hd
        neg_a = -jnp.exp(jnp.full((1, 1), hp_ref[0, h], F32))
        dt_bias = hp_ref[1, h]
        lb_col = jnp.sum(jnp.where(lane == h, ab, 0.0), axis=-1, keepdims=True)
        la_col = jnp.sum(jnp.where(lane == N_DELTA_HEADS + h, ab, 0.0), axis=-1, keepdims=True)
        la_row = abt_ref[pl.ds(N_DELTA_HEADS + h, 1), :]
        beta_col.append(_sigmoid(lb_col))
        g_col.append(neg_a * _softplus(la_col + dt_bias))
        g_row.append(neg_a * _softplus(la_row + dt_bias))

    ri = lax.broadcasted_iota(jnp.int32, (tt, tt), 0)
    ci = lax.broadcasted_iota(jnp.int32, (tt, tt), 1)
    same = (ri // CHUNK) == (ci // CHUNK)
    tril = same & (ri >= ci)
    strict = same & (ri > ci)
    triu = same & (ri <= ci)

    def mm(a, b):
        return jnp.dot(a.astype(BF16), b.astype(BF16), preferred_element_type=F32)

    def mm_nt(a, b):
        return lax.dot_general(a.astype(BF16), b.astype(BF16), (((1,), (1,)), ((), ())),
                               preferred_element_type=F32)

    def mm_tn(a, b):
        return lax.dot_general(a.astype(BF16), b.astype(BF16), (((0,), (0,)), ((), ())),
                               preferred_element_type=F32)

    gc_col = [jnp.sum(jnp.where(tril, g, 0.0), axis=-1, keepdims=True) for g in g_row]
    gc_row = [jnp.sum(jnp.where(triu, g, 0.0), axis=0, keepdims=True) for g in g_col]
    decay = [jnp.where(tril, jnp.exp(jnp.where(tril, a - b, 0.0)), 0.0) for a, b in zip(gc_col, gc_row)]
    e_gc = [jnp.exp(g) for g in gc_col]
    kk = [mm_nt(x, x) for x in k]
    qk = [mm_nt(a, b) for a, b in zip(q, k)]
    xm = [jnp.where(strict, -(b * kx * d), 0.0) for b, kx, d in zip(beta_col, kk, decay)]
    intra = [jnp.where(tril, a * d, 0.0) for a, d in zip(qk, decay)]
    r = [jnp.concatenate([b * vx, (b * e) * kx], axis=-1) for b, vx, e, kx in zip(beta_col, v, e_gc, k)]
    n_sq = CHUNK.bit_length() - 1
    for lvl in range(n_sq):
        r = [x + mm(m_, x) for m_, x in zip(xm, r)]
        if lvl + 1 < n_sq:
            xm = [mm(m_, m_) for m_ in xm]
    u = [x[:, 0:dh] for x in r]
    w = [x[:, dh:2 * dh] for x in r]
    qg = [a * e for a, e in zip(q, e_gc)]

    state = [s_ref[hd] for hd in heads]
    o_state = [[] for _ in heads]
    v_new = [[] for _ in heads]
    for c in range(nc):
        r0 = c * CHUNK
        sl = slice(r0, r0 + CHUNK)
        g_last = [g[r0 + CHUNK - 1:r0 + CHUNK] for g in gc_col]
        vn = [u[hd][sl] - mm(w[hd][sl], state[hd]) for hd in heads]
        os_ = [mm(qg[hd][sl], state[hd]) for hd in heads]
        k_dec = [k[hd][sl] * jnp.exp(g_last[hd] - gc_col[hd][sl]) for hd in heads]
        state = [state[hd] * jnp.exp(g_last[hd]) + mm_tn(k_dec[hd], vn[hd]) for hd in heads]
        for hd in heads:
            v_new[hd].append(vn[hd])
            o_state[hd].append(os_[hd])
    for hd in heads:
        s_ref[hd] = state[hd]

    outs = []
    for hd in heads:
        o = jnp.concatenate(o_state[hd], axis=0) + mm(intra[hd], jnp.concatenate(v_new[hd], axis=0))
        ms = jnp.mean(o * o, axis=-1, keepdims=True)
        outs.append(o * lax.rsqrt(ms + EPS) * og_ref[...])
    z = z_ref[...].astype(F32)
    o_ref[...] = (jnp.concatenate(outs, axis=-1) * (z * _sigmoid(z))).astype(o_ref.dtype)


def gated_deltanet(rest, ab, abt, conv_w, head_params, out_g, *, b, t, tt, nhb, qkv_col0, z_col0):
    m = b * t
    nt = t // tt
    nh = N_DELTA_HEADS
    sub = tt // SUBLANES
    wl = nhb * LANES
    assert nh % nhb == 0 and qkv_col0 % nhb == 0 and z_col0 % nhb == 0

    def cur(off):
        return pl.BlockSpec((tt, wl), lambda bi, h, i: (bi * nt + i, off // nhb + h))

    def prev(off):
        return pl.BlockSpec((SUBLANES, wl),
                            lambda bi, h, i: (jnp.maximum((bi * nt + i) * sub - 1, 0), off // nhb + h))

    def wspec(off):
        return pl.BlockSpec((CONV_WIDTH, wl), lambda bi, h, i: (0, off // nhb + h))

    return pl.pallas_call(
        functools.partial(_delta_kernel, tt=tt, nhb=nhb),
        out_shape=jax.ShapeDtypeStruct((m, nh * LANES), BF16),
        grid=(b, nh // nhb, nt),
        in_specs=[
            pl.BlockSpec(memory_space=pltpu.SMEM),
            cur(qkv_col0), cur(qkv_col0 + nh), cur(qkv_col0 + 2 * nh),
            prev(qkv_col0), prev(qkv_col0 + nh), prev(qkv_col0 + 2 * nh),
            wspec(0), wspec(nh), wspec(2 * nh),
            pl.BlockSpec((tt, LANES), lambda bi, h, i: (bi * nt + i, 0)),
            pl.BlockSpec((2 * nh, tt), lambda bi, h, i: (0, bi * nt + i)),
            cur(z_col0),
            pl.BlockSpec((1, LANES), lambda bi, h, i: (0, 0)),
        ],
        out_specs=pl.BlockSpec((tt, wl), lambda bi, h, i: (bi * nt + i, h)),
        scratch_shapes=[pltpu.VMEM((nhb, DELTA_HEAD_DIM, DELTA_HEAD_DIM), F32)],
        compiler_params=_cparams(("parallel", "parallel", "arbitrary")),
        name="gated_deltanet",
    )(head_params, rest, rest, rest, rest, rest, rest, conv_w, conv_w, conv_w, ab, abt, rest,
      out_g.reshape(1, LANES).astype(F32))


def _mem_attn_kernel(q_ref, k_ref, v_ref, o_ref):
    outs = []
    for h in range(N_MEM_HEADS):
        sl = slice(h * MEM_HEAD_DIM, (h + 1) * MEM_HEAD_DIM)
        s = lax.dot_general(q_ref[:, sl], k_ref[:, sl], (((1,), (1,)), ((), ())), preferred_element_type=F32)
        s = s - jnp.max(s, axis=-1, keepdims=True)
        p = jnp.exp(s)
        p = p / jnp.sum(p, axis=-1, keepdims=True)
        outs.append(jnp.dot(p.astype(BF16), v_ref[:, sl], preferred_element_type=F32))
    o_ref[...] = jnp.concatenate(outs, axis=-1).astype(o_ref.dtype)


def memory_attention(mq, mk, mv, *, b, t, mtok, tq):
    nq = t // tq
    w = N_MEM_HEADS * MEM_HEAD_DIM
    return pl.pallas_call(
        _mem_attn_kernel,
        out_shape=jax.ShapeDtypeStruct((b * t, w), BF16),
        grid=(b, nq),
        in_specs=[
            pl.BlockSpec((tq, w), lambda bi, i: (bi * nq + i, 0)),
            pl.BlockSpec((mtok, w), lambda bi, i: (bi, 0)),
            pl.BlockSpec((mtok, w), lambda bi, i: (bi, 0)),
        ],
        out_specs=pl.BlockSpec((tq, w), lambda bi, i: (bi * nq + i, 0)),
        compiler_params=_cparams(("parallel", "arbitrary")),
        name="memory_attention",
    )(mq, mk, mv)


def _merge_kernel(x_ref, ya_ref, yb_ref, yc_ref, ga_ref, gb_ref, gc_ref, bg_ref,
                  wa_ref, wb_ref, wc_ref, wo_ref, fg_ref, x1_ref, h2_ref):
    def branch(y_ref, g_ref, w_ref, idx):
        gate = _sigmoid(g_ref[...].astype(F32) + bg_ref[idx:idx + 1, :])
        return gate * jnp.dot(y_ref[...], w_ref[...], preferred_element_type=F32)

    merged = branch(ya_ref, ga_ref, wa_ref, 0) + branch(yb_ref, gb_ref, wb_ref, 1) + branch(yc_ref, gc_ref, wc_ref, 2)
    x1 = x_ref[...] + jnp.dot(merged.astype(BF16), wo_ref[...], preferred_element_type=F32)
    x1_ref[...] = x1
    ms = jnp.mean(x1 * x1, axis=-1, keepdims=True)
    h2_ref[...] = (x1 * lax.rsqrt(ms + EPS) * fg_ref[...]).astype(h2_ref.dtype)


def merge_project(x2d, y_diff, y_delta, y_mem, rest, b_gate, w_a, w_b, w_c, w_o, ffn_g, *, gate_col0, tm):
    m, d = x2d.shape
    row = lambda i: (i, 0)
    const = lambda i: (0, 0)
    assert gate_col0 % d == 0

    def gspec(j):
        return pl.BlockSpec((tm, d), lambda i: (i, gate_col0 // d + j))

    return pl.pallas_call(
        _merge_kernel,
        out_shape=(jax.ShapeDtypeStruct((m, d), F32), jax.ShapeDtypeStruct((m, d), BF16)),
        grid=(m // tm,),
        in_specs=[
            pl.BlockSpec((tm, d), row), pl.BlockSpec((tm, d), row), pl.BlockSpec((tm, d), row),
            pl.BlockSpec((tm, d), row), gspec(0), gspec(1), gspec(2),
            pl.BlockSpec((3, d), const),
            pl.BlockSpec((d, d), const), pl.BlockSpec((d, d), const), pl.BlockSpec((d, d), const),
            pl.BlockSpec((d, d), const), pl.BlockSpec((1, d), const),
        ],
        out_specs=(pl.BlockSpec((tm, d), row), pl.BlockSpec((tm, d), row)),
        compiler_params=_cparams(("parallel",)),
        name="merge_project",
    )(x2d, y_diff, y_delta, y_mem, rest, rest, rest, b_gate.reshape(3, d).astype(F32),
      w_a, w_b, w_c, w_o, ffn_g.reshape(1, d).astype(F32))


def _topk_rows(s, key, extra, k):
    big = jnp.iinfo(jnp.int32).max
    vals, keys, ext = [], [], []
    for _ in range(k):
        m = jnp.max(s, axis=0, keepdims=True)
        am = jnp.min(jnp.where(s == m, key, big), axis=0, keepdims=True)
        hit = key == am
        if extra is not None:
            ext.append(jnp.max(jnp.where(hit, extra, -1), axis=0, keepdims=True))
        s = jnp.where(hit, -jnp.inf, s)
        vals.append(m)
        keys.append(am)
    cat = lambda xs: jnp.concatenate(xs, axis=0)
    return cat(vals), cat(keys), (cat(ext) if extra is not None else None)


def _peer_route_kernel(q_ref, keys_ref, idx_ref, gate_ref):
    k = PEER_TOPK
    q = q_ref[...]
    tt = q.shape[0]

    def half(p):
        qp = q[:, p * PEER_HALF:(p + 1) * PEER_HALF]
        st = lax.dot_general(keys_ref[0, p], qp, (((1,), (1,)), ((), ())), preferred_element_type=F32)
        v, r, _ = _topk_rows(st, lax.broadcasted_iota(jnp.int32, st.shape, 0), None, k)
        return v, r

    s1, i1 = half(0)
    s2, i2 = half(1)
    g = SUBLANES
    sub = lax.broadcasted_iota(jnp.int32, (g, tt), 0)
    e1 = i1 * PEER_KEYS
    pieces = []
    for b in range(g):
        lim = min(g, k // (b + 1))
        sc = s1[0:g] + s2[b:b + 1]
        if lim < g:
            sc = jnp.where(sub < lim, sc, -jnp.inf)
        pieces.append((sc, sub * k + b, e1[0:g] + i2[b:b + 1]))
    pieces.append((s1[g:k] + s2[0:1], (sub + g) * k, e1[g:k] + i2[0:1]))
    pieces.append((s1[0:1] + s2[g:k], sub + g, e1[0:1] + i2[g:k]))
    cand_s = jnp.concatenate([p[0] for p in pieces], axis=0)
    cand_k = jnp.concatenate([p[1] for p in pieces], axis=0)
    cand_e = jnp.concatenate([p[2] for p in pieces], axis=0)
    top_s, _, top_e = _topk_rows(cand_s, cand_k, cand_e, k)
    e = jnp.exp(top_s - top_s[0:1])
    gate_ref[0] = e / jnp.sum(e, axis=0, keepdims=True)
    idx_ref[0] = top_e


def peer_route(qry, sub_keys, *, tt):
    m = qry.shape[0]
    nh = PEER_HEADS
    return pl.pallas_call(
        _peer_route_kernel,
        out_shape=(jax.ShapeDtypeStruct((nh, PEER_TOPK, m), jnp.int32),
                   jax.ShapeDtypeStruct((nh, PEER_TOPK, m), F32)),
        grid=(m // tt, nh),
        in_specs=[
            pl.BlockSpec((tt, 2 * PEER_HALF), lambda i, h: (i, h)),
            pl.BlockSpec((1, 2, PEER_KEYS, PEER_HALF), lambda i, h: (h, 0, 0, 0)),
        ],
        out_specs=(pl.BlockSpec((1, PEER_TOPK, tt), lambda i, h: (h, 0, i)),
                   pl.BlockSpec((1, PEER_TOPK, tt), lambda i, h: (h, 0, i))),
        compiler_params=_cparams(("parallel", "arbitrary")),
        name="peer_route",
    )(qry, sub_keys)


def _gelu_exact(x):
    return 0.5 * x * (1.0 + lax.erf(x * (2.0 ** -0.5)))


def _peer_expert_kernel(idx_ref, idxn_ref, gate_ref, h_ref, x_ref, pool_ref, poolt_ref, uv_hbm,
                        o_ref, *scratch, tb, ne, nbuf, dist):
    bufs, sem = scratch[:nbuf], scratch[nbuf]
    i = pl.program_id(0)
    n = pl.num_programs(0)
    rows = tb * ne
    d_sub = h_ref.shape[1]
    n_seg = 2 * tb
    seg = rows // n_seg

    def issue_rows(iref, row0, buf, which, r_lo, r_hi):
        for r in range(r_lo, r_hi):
            e = iref[row0 + r // ne, r % ne]
            pltpu.make_async_copy(uv_hbm.at[e], buf.at[r], sem.at[which, r % 2]).start(priority=r % 2)

    def wait_tile(buf, which):
        for k in range(2):
            pltpu.make_async_copy(uv_hbm.at[pl.ds(0, rows // 2)], buf.at[pl.ds(0, rows // 2)],
                                  sem.at[which, k]).wait()

    @pl.when(i == 0)
    def _():
        for a in range(dist):
            def tok(t, carry, a=a):
                for j in range(ne):
                    pltpu.make_async_copy(uv_hbm.at[idx_ref[a * tb + t, j]], bufs[a].at[t * ne + j],
                                          sem.at[a, j % 2]).start(priority=j % 2)
                return carry
            lax.fori_loop(0, tb, tok, 0)

    blk = 2 * d_sub
    sub_id = lax.broadcasted_iota(jnp.int32, (d_sub, ne * blk), 0)
    col_id = lax.broadcasted_iota(jnp.int32, (d_sub, ne * blk), 1)
    diag_u = (col_id % blk) == sub_id
    diag_v = (col_id % blk) == sub_id + d_sub

    def tile(row0, buf, nxt_iref, nxt_row0, nxt_buf, nxt_which):
        drows = []
        for t in range(tb):
            z = buf[pl.ds(t * ne, ne)].reshape(ne * blk, LANES)
            c = lax.dot_general(h_ref[row0 + t], z, (((1,), (1,)), ((), ())), preferred_element_type=F32)
            drows.append(jnp.sum(jnp.where(diag_u, c, 0.0), axis=0, keepdims=True))
            issue_rows(nxt_iref, nxt_row0, nxt_buf, nxt_which, t * seg, (t + 1) * seg)
        dall = jnp.concatenate(drows, axis=0)
        d_hi = dall.astype(BF16)
        d_lo = (dall - d_hi.astype(F32)).astype(BF16)
        a = (jnp.dot(d_hi, pool_ref[...], preferred_element_type=F32)
             + jnp.dot(d_lo, pool_ref[...], preferred_element_type=F32))
        w = gate_ref[pl.ds(row0, tb), :] * _gelu_exact(a)
        wrep = jnp.dot(w.astype(BF16), poolt_ref[...], preferred_element_type=F32)
        for t in range(tb):
            wexp = jnp.where(diag_v, wrep[t:t + 1, :], 0.0).astype(BF16)
            z = buf[pl.ds(t * ne, ne)].reshape(ne * blk, LANES)
            o_ref[row0 + t] = x_ref[row0 + t] + jnp.dot(wexp, z, preferred_element_type=F32)
            issue_rows(nxt_iref, nxt_row0, nxt_buf, nxt_which, (tb + t) * seg, (tb + t + 1) * seg)

    for a in range(nbuf):
        wait_tile(bufs[a], a)
        nxt = a + dist
        if nxt < nbuf:
            tile(a * tb, bufs[a], idx_ref, nxt * tb, bufs[nxt], nxt)
        else:
            tile(a * tb, bufs[a], idxn_ref, (nxt - nbuf) * tb, bufs[nxt - nbuf], nxt - nbuf)

    @pl.when(i == n - 1)
    def _():
        for a in range(dist):
            wait_tile(bufs[a], a)


def peer_experts(idx, gates, h3, x3, uv3, *, tb):
    m, ne = idx.shape
    d_sub = h3.shape[1]
    nbuf, dist = 4, 2
    ts = nbuf * tb
    n = m // ts
    rows = tb * ne
    blk = 2 * d_sub
    cid = jnp.arange(ne * blk) // blk
    pool = (cid[:, None] == jnp.arange(ne)[None, :]).astype(BF16)
    tok = lambda i: (i, 0)
    tok3 = lambda i: (i, 0, 0)
    return pl.pallas_call(
        functools.partial(_peer_expert_kernel, tb=tb, ne=ne, nbuf=nbuf, dist=dist),
        out_shape=jax.ShapeDtypeStruct(x3.shape, F32),
        grid=(n,),
        in_specs=[
            pl.BlockSpec((ts, ne), tok, memory_space=pltpu.SMEM),
            pl.BlockSpec((ts, ne), lambda i: (jnp.minimum(i + 1, n - 1), 0), memory_space=pltpu.SMEM),
            pl.BlockSpec((ts, ne), tok),
            pl.BlockSpec((ts, d_sub, LANES), tok3),
            pl.BlockSpec((ts, d_sub, LANES), tok3),
            pl.BlockSpec((ne * blk, ne), lambda i: (0, 0)),
            pl.BlockSpec((ne, ne * blk), lambda i: (0, 0)),
            pl.BlockSpec(memory_space=pl.ANY),
        ],
        out_specs=pl.BlockSpec((ts, d_sub, LANES), tok3),
        scratch_shapes=[pltpu.VMEM((rows, blk, LANES), BF16) for _ in range(nbuf)]
        + [pltpu.SemaphoreType.DMA((nbuf, 2))],
        compiler_params=_cparams(("arbitrary",)),
        name="peer_experts",
    )(idx, idx, gates, h3, x3, pool, pool.T, uv3)


def kernel(x, mem, positions, attn_norm_g, mem_norm_g, w_in, b_gate, diff_q_norm_g, diff_k_norm_g, lambda_q1, lambda_k1, lambda_q2, lambda_k2, diff_subln_g, rel_bias_table, conv_w, a_log, dt_bias, delta_out_norm_g, w_mem_kv, mem_q_norm_g, mem_k_norm_g, w_br_diff, w_br_delta, w_br_mem, w_out, ffn_norm_g, w_query, sub_keys, expert_u, expert_v):
    del positions
    b, t, d = x.shape
    mtok = mem.shape[1]
    m = b * t
    depth = w_in.shape[0]
    nh = N_DIFF_HEADS
    qk_w = nh * 2 * DIFF_HEAD_DIM
    dv_w = nh * 2 * DIFF_HEAD_DIM
    dl_w = N_DELTA_HEADS * DELTA_HEAD_DIM
    mem_w = N_MEM_HEADS * MEM_HEAD_DIM
    tq = min(256, t)
    tt_delta = min(256, t)

    x2d = x.reshape(m, d)
    for l in range(depth):
        lam_init = 0.8 - 0.6 * math.exp(-0.3 * l)
        wl = w_in[l]
        o = 0
        w_qk = wl[:, o:o + 2 * qk_w]; o += 2 * qk_w
        w_dv = wl[:, o:o + dv_w]; o += dv_w
        w_lqkv = wl[:, o:o + 3 * dl_w]; o += 3 * dl_w
        w_lz = wl[:, o:o + dl_w]; o += dl_w
        w_ab = wl[:, o:o + 2 * N_DELTA_HEADS]; o += 2 * N_DELTA_HEADS
        w_mq = wl[:, o:o + mem_w]; o += mem_w
        w_gate = wl[:, o:o + 3 * d]

        qk_gain = jnp.concatenate([jnp.tile(diff_q_norm_g[l], 2 * nh) * (DIFF_HEAD_DIM ** -0.5 * LOG2E),
                                   jnp.tile(diff_k_norm_g[l], 2 * nh)])
        qk = norm_matmul(x2d, w_qk.astype(BF16), norm_g=attn_norm_g[l], group=DIFF_HEAD_DIM, group_gain=qk_gain)
        w_rest = jnp.concatenate([w_dv, w_lqkv, w_lz, w_gate], axis=1).astype(BF16)
        rest = norm_matmul(x2d, w_rest, norm_g=attn_norm_g[l])
        mq_gain = jnp.tile(mem_q_norm_g[l], N_MEM_HEADS) * (MEM_HEAD_DIM ** -0.5)
        mq = norm_matmul(x2d, w_mq.astype(BF16), norm_g=attn_norm_g[l], group=MEM_HEAD_DIM, group_gain=mq_gain)
        w_ab_pad = jnp.pad(w_ab, ((0, 0), (0, LANES - 2 * N_DELTA_HEADS))).astype(BF16)
        ab = norm_matmul(x2d, w_ab_pad, norm_g=attn_norm_g[l], out_dtype=F32, tn=LANES)
        abt = ab[:, :2 * N_DELTA_HEADS].T

        lam = (jnp.exp(jnp.sum(lambda_q1[l].astype(F32) * lambda_k1[l].astype(F32)))
               - jnp.exp(jnp.sum(lambda_q2[l].astype(F32) * lambda_k2[l].astype(F32))) + lam_init)
        bias_tiles = rel_bias_tiles(rel_bias_table, tq)
        scal = jnp.concatenate([jnp.stack([lam, jnp.asarray(1.0 - lam_init, F32)]),
                                rel_bias_table[REL_BUCKETS - 1].astype(F32) * LOG2E])
        y_diff = diff_attention(qk, rest, bias_tiles, scal, diff_subln_g[l], b=b, t=t, tq=tq, v_col0=0)

        head_params = jnp.stack([a_log[l], dt_bias[l]]).astype(F32)
        y_delta = gated_deltanet(rest, ab, abt, conv_w[l].astype(F32), head_params, delta_out_norm_g[l],
                                 b=b, t=t, tt=tt_delta, nhb=8, qkv_col0=dv_w // LANES,
                                 z_col0=(dv_w + 3 * dl_w) // LANES)

        mem2d = mem.reshape(b * mtok, d)
        wkv = w_mem_kv[l].astype(BF16)
        mk = norm_matmul(mem2d, wkv[:, :mem_w], norm_g=mem_norm_g[l], group=MEM_HEAD_DIM,
                         group_gain=jnp.tile(mem_k_norm_g[l], N_MEM_HEADS))
        mv = norm_matmul(mem2d, wkv[:, mem_w:], norm_g=mem_norm_g[l])
        y_mem = memory_attention(mq, mk, mv, b=b, t=t, mtok=mtok, tq=min(512, t))

        x1, h2 = merge_project(x2d, y_diff, y_delta, y_mem, rest, b_gate[l],
                               w_br_diff[l].astype(BF16), w_br_delta[l].astype(BF16), w_br_mem[l].astype(BF16),
                               w_out[l].astype(BF16), ffn_norm_g[l],
                               gate_col0=dv_w + 4 * dl_w, tm=min(512, m))

        qry = norm_matmul(h2, w_query[l].astype(BF16))
        idx_t, gate_t = peer_route(qry, sub_keys[l].astype(BF16), tt=min(512, m))
        ne = PEER_HEADS * PEER_TOPK
        idx = idx_t.reshape(ne, m).T
        gates = gate_t.reshape(ne, m).T
        sub = d // LANES
        uv3 = jnp.concatenate([expert_u[l].reshape(-1, sub, LANES), expert_v[l].reshape(-1, sub, LANES)],
                              axis=1).astype(BF16)
        x2 = peer_experts(idx, gates, h2.reshape(m, sub, LANES), x1.reshape(m, sub, LANES), uv3, tb=8)
        x2d = x2.reshape(m, d)
    return x2d.reshape(b, t, d)
```

```python
import functools
import math

import jax
import jax.numpy as jnp
from jax import lax
from jax.experimental import pallas as pl
from jax.experimental.pallas import tpu as pltpu

F32 = jnp.float32
BF16 = jnp.bfloat16
EPS = 1e-6
NEG = -1e30
LOG2E = math.log2(math.e)

N_DIFF_HEADS = 8
DIFF_HEAD_DIM = 64
N_DELTA_HEADS = 8
DELTA_HEAD_DIM = 128
CONV_WIDTH = 4
CHUNK = 64
N_MEM_HEADS = 4
MEM_HEAD_DIM = 256
REL_BUCKETS = 32
REL_MAX_DIST = 128
PEER_HEADS = 8
PEER_KEYS = 128
PEER_TOPK = 16
PEER_HALF = 128
LANES = 128
SUBLANES = 8
VMEM_LIMIT = 56 * 1024 * 1024


def _cparams(sem):
    return pltpu.CompilerParams(dimension_semantics=sem, vmem_limit_bytes=VMEM_LIMIT)


def _norm_matmul_kernel(*refs, has_norm, has_group):
    it = iter(refs)
    x_ref = next(it)
    g_ref = next(it) if has_norm else None
    w_ref = next(it)
    gm_ref = next(it) if has_group else None
    gain_ref = next(it) if has_group else None
    o_ref = next(it)
    h_ref = next(it)

    @pl.when(pl.program_id(1) == 0)
    def _():
        x = x_ref[...].astype(F32)
        if has_norm:
            ms = jnp.mean(x * x, axis=-1, keepdims=True)
            x = x * lax.rsqrt(ms + EPS) * g_ref[...]
        h_ref[...] = x.astype(BF16)

    y = jnp.dot(h_ref[...], w_ref[...], preferred_element_type=F32)
    if has_group:
        ms = jnp.dot((y * y).astype(BF16), gm_ref[...], preferred_element_type=F32)
        y = y * lax.rsqrt(ms + EPS) * gain_ref[...]
    o_ref[...] = y.astype(o_ref.dtype)


def norm_matmul(x, w, *, norm_g=None, group=None, group_gain=None, out_dtype=BF16, tm=1024, tn=None):
    m, k = x.shape
    n = w.shape[1]
    has_norm = norm_g is not None
    has_group = group is not None
    if tn is None:
        tn = 512 if has_group else min(n, 2048)
    tm = min(tm, m)
    assert m % tm == 0 and n % tn == 0
    in_specs = [pl.BlockSpec((tm, k), lambda i, j: (i, 0))]
    args = [x]
    if has_norm:
        in_specs.append(pl.BlockSpec((1, k), lambda i, j: (0, 0)))
        args.append(norm_g.reshape(1, k).astype(F32))
    in_specs.append(pl.BlockSpec((k, tn), lambda i, j: (0, j)))
    args.append(w)
    if has_group:
        gid = jnp.arange(tn) // group
        gm = jnp.where(gid[:, None] == gid[None, :], 1.0 / group, 0.0).astype(BF16)
        in_specs.append(pl.BlockSpec((tn, tn), lambda i, j: (0, 0)))
        args.append(gm)
        in_specs.append(pl.BlockSpec((1, tn), lambda i, j: (0, j)))
        args.append(group_gain.reshape(1, n).astype(F32))
    return pl.pallas_call(
        functools.partial(_norm_matmul_kernel, has_norm=has_norm, has_group=has_group),
        out_shape=jax.ShapeDtypeStruct((m, n), out_dtype),
        grid=(m // tm, n // tn),
        in_specs=in_specs,
        out_specs=pl.BlockSpec((tm, tn), lambda i, j: (i, j)),
        scratch_shapes=[pltpu.VMEM((tm, k), BF16)],
        compiler_params=_cparams(("parallel", "arbitrary")),
        name="norm_matmul",
    )(*args)


def _t5_bucket(n):
    max_exact = REL_BUCKETS // 2
    nf = jnp.maximum(n, 1).astype(F32)
    large = max_exact + (jnp.log(nf / max_exact) / math.log(REL_MAX_DIST / max_exact)
                         * (REL_BUCKETS - max_exact)).astype(jnp.int32)
    large = jnp.minimum(large, REL_BUCKETS - 1)
    return jnp.where(n < max_exact, n, large)


def _rel_bias_kernel(table_ref, o_ref, *, tq):
    hm = pl.program_id(0)
    r = lax.broadcasted_iota(jnp.int32, (tq, 2 * tq), 0)
    c = lax.broadcasted_iota(jnp.int32, (tq, 2 * tq), 1)
    bucket = _t5_bucket(jnp.maximum(r - c + tq, 0))
    acc = jnp.zeros((tq, 2 * tq), F32)
    for b in range(REL_BUCKETS):
        acc = jnp.where(bucket == b, table_ref[b, hm], acc)
    o_ref[0] = acc * LOG2E


def rel_bias_tiles(rel_table, tq):
    nmaps = rel_table.shape[1]
    return pl.pallas_call(
        functools.partial(_rel_bias_kernel, tq=tq),
        out_shape=jax.ShapeDtypeStruct((nmaps, tq, 2 * tq), F32),
        grid=(nmaps,),
        in_specs=[pl.BlockSpec(memory_space=pltpu.SMEM)],
        out_specs=pl.BlockSpec((1, tq, 2 * tq), lambda i: (i, 0, 0)),
        compiler_params=_cparams(("arbitrary",)),
        name="rel_bias_tiles",
    )(rel_table.astype(F32))


def _diff_attn_kernel(scal_ref, q_ref, k_ref, v_ref, bias_ref, subg_ref, o_ref, *, tq, hpb):
    hb = pl.program_id(1)
    i = pl.program_id(2)
    heads = range(hpb)
    lam = scal_ref[0]
    out_scale = scal_ref[1]

    lane = lax.broadcasted_iota(jnp.int32, (tq, LANES), 1)
    row2 = lax.broadcasted_iota(jnp.int32, (2 * tq, 1), 0)
    qq, cfar = [], []
    for hd in heads:
        q = q_ref[:, hd * LANES:(hd + 1) * LANES]
        zero = jnp.zeros_like(q)
        qq.append(jnp.concatenate([jnp.where(lane < DIFF_HEAD_DIM, q, zero),
                                   jnp.where(lane >= DIFF_HEAD_DIM, q, zero)], axis=0))
        h = hb * hpb + hd
        cfar.append(jnp.where(row2 < tq, scal_ref[2 + 2 * h], scal_ref[3 + 2 * h]))

    def update(carry, s, vj, shift=None):
        m, l, acc = carry
        rowmax = [jnp.max(x, axis=-1, keepdims=True) for x in s]
        if shift is not None:
            rowmax = [a + c for a, c in zip(rowmax, shift)]
        m_new = [jnp.maximum(a, b_) for a, b_ in zip(m, rowmax)]
        alpha = [jnp.exp2(a - b_) for a, b_ in zip(m, m_new)]
        sub = m_new if shift is None else [a - c for a, c in zip(m_new, shift)]
        p = [jnp.exp2(x - a) for x, a in zip(s, sub)]
        l = [a * b_ + jnp.sum(x, axis=-1, keepdims=True) for a, b_, x in zip(l, alpha, p)]
        acc = [a * b_ + jnp.dot(x.astype(BF16), v_, preferred_element_type=F32)
               for a, b_, x, v_ in zip(acc, alpha, p, vj)]
        return m_new, l, acc

    def scores(start, width):
        rows = pl.ds(pl.multiple_of(start, tq), width)
        s = [lax.dot_general(qq[hd], k_ref[rows, hd * LANES:(hd + 1) * LANES], (((1,), (1,)), ((), ())),
                             preferred_element_type=F32) for hd in heads]
        vj = [v_ref[rows, hd * LANES:(hd + 1) * LANES] for hd in heads]
        return s, vj

    def pack(carry):
        m, l, acc = carry
        return tuple(m) + tuple(l) + tuple(acc)

    def unpack(flat):
        return list(flat[0:hpb]), list(flat[hpb:2 * hpb]), list(flat[2 * hpb:3 * hpb])

    def far_step(j, flat):
        s, vj = scores(j * (2 * tq), 2 * tq)
        return pack(update(unpack(flat), s, vj, cfar))

    def far_single(flat):
        s, vj = scores((i - 2) * tq, tq)
        return pack(update(unpack(flat), s, vj, cfar))

    n_far = jnp.maximum(i - 1, 0)
    init = ([jnp.full((2 * tq, 1), NEG, F32) for _ in heads], [jnp.zeros((2 * tq, 1), F32) for _ in heads],
            [jnp.zeros((2 * tq, LANES), F32) for _ in heads])
    flat = lax.fori_loop(0, n_far // 2, far_step, pack(init))
    flat = lax.cond(n_far % 2 == 1, far_single, lambda c: c, flat)
    carry = unpack(flat)

    s, vj = scores(jnp.maximum(i - 1, 0) * tq, tq)
    s = [jnp.where(i >= 1, x + jnp.concatenate([bias_ref[2 * hd, :, 0:tq], bias_ref[2 * hd + 1, :, 0:tq]], axis=0),
                   NEG) for hd, x in zip(heads, s)]
    carry = update(carry, s, vj)

    s, vj = scores(i * tq, tq)
    rq = lax.broadcasted_iota(jnp.int32, (2 * tq, tq), 0)
    rq = jnp.where(rq >= tq, rq - tq, rq)
    ck = lax.broadcasted_iota(jnp.int32, (2 * tq, tq), 1)
    s = [jnp.where(ck <= rq, x + jnp.concatenate([bias_ref[2 * hd, :, tq:2 * tq],
                                                  bias_ref[2 * hd + 1, :, tq:2 * tq]], axis=0), NEG)
         for hd, x in zip(heads, s)]
    m, l, acc = update(carry, s, vj)

    outs = []
    for hd in heads:
        o = acc[hd] / l[hd]
        o = o[0:tq] - lam * o[tq:2 * tq]
        ms = jnp.mean(o * o, axis=-1, keepdims=True)
        outs.append(o * lax.rsqrt(ms + EPS) * subg_ref[...] * out_scale)
    o_ref[...] = jnp.concatenate(outs, axis=-1).astype(o_ref.dtype)


def diff_attention(qk, rest, bias_tiles, scal, subln_g, *, b, t, tq, hpb, v_col0):
    m = b * t
    nq = t // tq
    nh = N_DIFF_HEADS
    wl = hpb * LANES
    assert nh % hpb == 0 and v_col0 % hpb == 0
    return pl.pallas_call(
        functools.partial(_diff_attn_kernel, tq=tq, hpb=hpb),
        out_shape=jax.ShapeDtypeStruct((m, nh * LANES), BF16),
        grid=(b, nh // hpb, nq),
        in_specs=[
            pl.BlockSpec(memory_space=pltpu.SMEM),
            pl.BlockSpec((tq, wl), lambda bi, h, i: (bi * nq + i, h)),
            pl.BlockSpec((t, wl), lambda bi, h, i: (bi, nh // hpb + h)),
            pl.BlockSpec((t, wl), lambda bi, h, i: (bi, v_col0 // hpb + h)),
            pl.BlockSpec((2 * hpb, tq, 2 * tq), lambda bi, h, i: (h, 0, 0)),
            pl.BlockSpec((1, LANES), lambda bi, h, i: (0, 0)),
        ],
        out_specs=pl.BlockSpec((tq, wl), lambda bi, h, i: (bi * nq + i, h)),
        compiler_params=_cparams(("parallel", "parallel", "arbitrary")),
        name="diff_attention",
    )(scal, qk, qk, rest, bias_tiles, subln_g.reshape(1, LANES).astype(F32))


def _sigmoid(x):
    return 1.0 / (1.0 + jnp.exp(-x))


def _softplus(x):
    return jnp.maximum(x, 0.0) + jnp.log(1.0 + jnp.exp(-jnp.abs(x)))


def _delta_kernel(hp_ref, xq_ref, xk_ref, xv_ref, pq_ref, pk_ref, pv_ref, wq_ref, wk_ref, wv_ref,
                  ab_ref, abt_ref, z_ref, og_ref, o_ref, s_ref, *, tt, nhb):
    hb = pl.program_id(1)
    i = pl.program_id(2)
    nc = tt // CHUNK
    dh = DELTA_HEAD_DIM
    heads = range(nhb)

    @pl.when(i == 0)
    def _():
        s_ref[...] = jnp.zeros_like(s_ref)

    def conv_silu(x_ref, p_ref, w_ref):
        prev = jnp.where(i > 0, p_ref[...].astype(F32), 0.0)
        xf = jnp.concatenate([prev, x_ref[...].astype(F32)], axis=0)
        w = w_ref[...]
        base = SUBLANES - (CONV_WIDTH - 1)
        y = xf[base:base + tt] * w[0:1]
        for c in range(1, CONV_WIDTH):
            y = y + xf[base + c:base + c + tt] * w[c:c + 1]
        return y * _sigmoid(y)

    def split(x):
        return [x[:, hd * dh:(hd + 1) * dh] for hd in heads]

    q = split(conv_silu(xq_ref, pq_ref, wq_ref))
    k = split(conv_silu(xk_ref, pk_ref, wk_ref))
    v = split(conv_silu(xv_ref, pv_ref, wv_ref))
    q = [x * lax.rsqrt(jnp.sum(x * x, axis=-1, keepdims=True) + EPS) * (dh ** -0.5) for x in q]
    k = [x * lax.rsqrt(jnp.sum(x * x, axis=-1, keepdims=True) + EPS) for x in k]

    ab = ab_ref[...]
    lane = lax.broadcasted_iota(jnp.int32, ab.shape, 1)
    beta_col, g_col, g_row = [], [], []
    for hd in heads:
        h = hb * nhb + hd
        neg_a = -jnp.exp(jnp.full((1, 1), hp_ref[0, h], F32))
        dt_bias = hp_ref[1, h]
        lb_col = jnp.sum(jnp.where(lane == h, ab, 0.0), axis=-1, keepdims=True)
        la_col = jnp.sum(jnp.where(lane == N_DELTA_HEADS + h, ab, 0.0), axis=-1, keepdims=True)
        la_row = abt_ref[pl.ds(N_DELTA_HEADS + h, 1), :]
        beta_col.append(_sigmoid(lb_col))
        g_col.append(neg_a * _softplus(la_col + dt_bias))
        g_row.append(neg_a * _softplus(la_row + dt_bias))

    ri = lax.broadcasted_iota(jnp.int32, (tt, tt), 0)
    ci = lax.broadcasted_iota(jnp.int32, (tt, tt), 1)
    same = (ri // CHUNK) == (ci // CHUNK)
    tril = same & (ri >= ci)
    strict = same & (ri > ci)
    triu = same & (ri <= ci)

    def mm(a, b):
        return jnp.dot(a.astype(BF16), b.astype(BF16), preferred_element_type=F32)

    def mm_nt(a, b):
        return lax.dot_general(a.astype(BF16), b.astype(BF16), (((1,), (1,)), ((), ())),
                               preferred_element_type=F32)

    def mm_tn(a, b):
        return lax.dot_general(a.astype(BF16), b.astype(BF16), (((0,), (0,)), ((), ())),
                               preferred_element_type=F32)

    gc_col = [jnp.sum(jnp.where(tril, g, 0.0), axis=-1, keepdims=True) for g in g_row]
    gc_row = [jnp.sum(jnp.where(triu, g, 0.0), axis=0, keepdims=True) for g in g_col]
    decay = [jnp.where(tril, jnp.exp(jnp.where(tril, a - b, 0.0)), 0.0) for a, b in zip(gc_col, gc_row)]
    e_gc = [jnp.exp(g) for g in gc_col]
    kk = [mm_nt(x, x) for x in k]
    qk = [mm_nt(a, b) for a, b in zip(q, k)]
    xm = [jnp.where(strict, -(b * kx * d), 0.0) for b, kx, d in zip(beta_col, kk, decay)]
    intra = [jnp.where(tril, a * d, 0.0) for a, d in zip(qk, decay)]
    r = [jnp.concatenate([b * vx, (b * e) * kx], axis=-1) for b, vx, e, kx in zip(beta_col, v, e_gc, k)]
    n_sq = CHUNK.bit_length() - 1
    for lvl in range(n_sq):
        r = [x + mm(m_, x) for m_, x in zip(xm, r)]
        if lvl + 1 < n_sq:
            xm = [mm(m_, m_) for m_ in xm]
    u = [x[:, 0:dh] for x in r]
    w = [x[:, dh:2 * dh] for x in r]
    qg = [a * e for a, e in zip(q, e_gc)]

    state = [s_ref[hd] for hd in heads]
    o_state = [[] for _ in heads]
    v_new = [[] for _ in heads]
    for c in range(nc):
        r0 = c * CHUNK
        sl = slice(r0, r0 + CHUNK)
        g_last = [g[r0 + CHUNK - 1:r0 + CHUNK] for g in gc_col]
        vn = [u[hd][sl] - mm(w[hd][sl], state[hd]) for hd in heads]
        os_ = [mm(qg[hd][sl], state[hd]) for hd in heads]
        k_dec = [k[hd][sl] * jnp.exp(g_last[hd] - gc_col[hd][sl]) for hd in heads]
        state = [state[hd] * jnp.exp(g_last[hd]) + mm_tn(k_dec[hd], vn[hd]) for hd in heads]
        for hd in heads:
            v_new[hd].append(vn[hd])
            o_state[hd].append(os_[hd])
    for hd in heads:
        s_ref[hd] = state[hd]

    outs = []
    for hd in heads:
        o = jnp.concatenate(o_state[hd], axis=0) + mm(intra[hd], jnp.concatenate(v_new[hd], axis=0))
        ms = jnp.mean(o * o, axis=-1, keepdims=True)
        outs.append(o * lax.rsqrt(ms + EPS) * og_ref[...])
    z = z_ref[...].astype(F32)
    o_ref[...] = (jnp.concatenate(outs, axis=-1) * (z * _sigmoid(z))).astype(o_ref.dtype)


def gated_deltanet(rest, ab, abt, conv_w, head_params, out_g, *, b, t, tt, nhb, qkv_col0, z_col0):
    m = b * t
    nt = t // tt
    nh = N_DELTA_HEADS
    sub = tt // SUBLANES
    wl = nhb * LANES
    assert nh % nhb == 0 and qkv_col0 % nhb == 0 and z_col0 % nhb == 0

    def cur(off):
        return pl.BlockSpec((tt, wl), lambda bi, h, i: (bi * nt + i, off // nhb + h))

    def prev(off):
        return pl.BlockSpec((SUBLANES, wl),
                            lambda bi, h, i: (jnp.maximum((bi * nt + i) * sub - 1, 0), off // nhb + h))

    def wspec(off):
        return pl.BlockSpec((CONV_WIDTH, wl), lambda bi, h, i: (0, off // nhb + h))

    return pl.pallas_call(
        functools.partial(_delta_kernel, tt=tt, nhb=nhb),
        out_shape=jax.ShapeDtypeStruct((m, nh * LANES), BF16),
        grid=(b, nh // nhb, nt),
        in_specs=[
            pl.BlockSpec(memory_space=pltpu.SMEM),
            cur(qkv_col0), cur(qkv_col0 + nh), cur(qkv_col0 + 2 * nh),
            prev(qkv_col0), prev(qkv_col0 + nh), prev(qkv_col0 + 2 * nh),
            wspec(0), wspec(nh), wspec(2 * nh),
            pl.BlockSpec((tt, LANES), lambda bi, h, i: (bi * nt + i, 0)),
            pl.BlockSpec((2 * nh, tt), lambda bi, h, i: (0, bi * nt + i)),
            cur(z_col0),
            pl.BlockSpec((1, LANES), lambda bi, h, i: (0, 0)),
        ],
        out_specs=pl.BlockSpec((tt, wl), lambda bi, h, i: (bi * nt + i, h)),
        scratch_shapes=[pltpu.VMEM((nhb, DELTA_HEAD_DIM, DELTA_HEAD_DIM), F32)],
        compiler_params=_cparams(("parallel", "parallel", "arbitrary")),
        name="gated_deltanet",
    )(head_params, rest, rest, rest, rest, rest, rest, conv_w, conv_w, conv_w, ab, abt, rest,
      out_g.reshape(1, LANES).astype(F32))


def _mem_attn_kernel(q_ref, k_ref, v_ref, o_ref):
    outs = []
    for h in range(N_MEM_HEADS):
        sl = slice(h * MEM_HEAD_DIM, (h + 1) * MEM_HEAD_DIM)
        s = lax.dot_general(q_ref[:, sl], k_ref[:, sl], (((1,), (1,)), ((), ())), preferred_element_type=F32)
        s = s - jnp.max(s, axis=-1, keepdims=True)
        p = jnp.exp(s)
        p = p / jnp.sum(p, axis=-1, keepdims=True)
        outs.append(jnp.dot(p.astype(BF16), v_ref[:, sl], preferred_element_type=F32))
    o_ref[...] = jnp.concatenate(outs, axis=-1).astype(o_ref.dtype)


def memory_attention(mq, mk, mv, *, b, t, mtok, tq):
    nq = t // tq
    w = N_MEM_HEADS * MEM_HEAD_DIM
    return pl.pallas_call(
        _mem_attn_kernel,
        out_shape=jax.ShapeDtypeStruct((b * t, w), BF16),
        grid=(b, nq),
        in_specs=[
            pl.BlockSpec((tq, w), lambda bi, i: (bi * nq + i, 0)),
            pl.BlockSpec((mtok, w), lambda bi, i: (bi, 0)),
            pl.BlockSpec((mtok, w), lambda bi, i: (bi, 0)),
        ],
        out_specs=pl.BlockSpec((tq, w), lambda bi, i: (bi * nq + i, 0)),
        compiler_params=_cparams(("parallel", "arbitrary")),
        name="memory_attention",
    )(mq, mk, mv)


def _merge_kernel(x_ref, ya_ref, yb_ref, yc_ref, ga_ref, gb_ref, gc_ref, bg_ref,
                  wa_ref, wb_ref, wc_ref, wo_ref, fg_ref, x1_ref, h2_ref):
    def branch(y_ref, g_ref, w_ref, idx):
        gate = _sigmoid(g_ref[...].astype(F32) + bg_ref[idx:idx + 1, :])
        return gate * jnp.dot(y_ref[...], w_ref[...], preferred_element_type=F32)

    merged = branch(ya_ref, ga_ref, wa_ref, 0) + branch(yb_ref, gb_ref, wb_ref, 1) + branch(yc_ref, gc_ref, wc_ref, 2)
    x1 = x_ref[...] + jnp.dot(merged.astype(BF16), wo_ref[...], preferred_element_type=F32)
    x1_ref[...] = x1
    ms = jnp.mean(x1 * x1, axis=-1, keepdims=True)
    h2_ref[...] = (x1 * lax.rsqrt(ms + EPS) * fg_ref[...]).astype(h2_ref.dtype)


def merge_project(x2d, y_diff, y_delta, y_mem, rest, b_gate, w_a, w_b, w_c, w_o, ffn_g, *, gate_col0, tm):
    m, d = x2d.shape
    row = lambda i: (i, 0)
    const = lambda i: (0, 0)
    assert gate_col0 % d == 0

    def gspec(j):
        return pl.BlockSpec((tm, d), lambda i: (i, gate_col0 // d + j))

    return pl.pallas_call(
        _merge_kernel,
        out_shape=(jax.ShapeDtypeStruct((m, d), F32), jax.ShapeDtypeStruct((m, d), BF16)),
        grid=(m // tm,),
        in_specs=[
            pl.BlockSpec((tm, d), row), pl.BlockSpec((tm, d), row), pl.BlockSpec((tm, d), row),
            pl.BlockSpec((tm, d), row), gspec(0), gspec(1), gspec(2),
            pl.BlockSpec((3, d), const),
            pl.BlockSpec((d, d), const), pl.BlockSpec((d, d), const), pl.BlockSpec((d, d), const),
            pl.BlockSpec((d, d), const), pl.BlockSpec((1, d), const),
        ],
        out_specs=(pl.BlockSpec((tm, d), row), pl.BlockSpec((tm, d), row)),
        compiler_params=_cparams(("parallel",)),
        name="merge_project",
    )(x2d, y_diff, y_delta, y_mem, rest, rest, rest, b_gate.reshape(3, d).astype(F32),
      w_a, w_b, w_c, w_o, ffn_g.reshape(1, d).astype(F32))


def _topk_rows(s, key, extra, k):
    big = jnp.iinfo(jnp.int32).max
    vals, keys, ext = [], [], []
    for _ in range(k):
        m = jnp.max(s, axis=0, keepdims=True)
        am = jnp.min(jnp.where(s == m, key, big), axis=0, keepdims=True)
        hit = key == am
        if extra is not None:
            ext.append(jnp.max(jnp.where(hit, extra, -1), axis=0, keepdims=True))
        s = jnp.where(hit, -jnp.inf, s)
        vals.append(m)
        keys.append(am)
    cat = lambda xs: jnp.concatenate(xs, axis=0)
    return cat(vals), cat(keys), (cat(ext) if extra is not None else None)


def _peer_route_kernel(q_ref, keys_ref, idx_ref, gate_ref):
    k = PEER_TOPK
    q = q_ref[...]
    tt = q.shape[0]

    def half(p):
        qp = q[:, p * PEER_HALF:(p + 1) * PEER_HALF]
        st = lax.dot_general(keys_ref[0, p], qp, (((1,), (1,)), ((), ())), preferred_element_type=F32)
        v, r, _ = _topk_rows(st, lax.broadcasted_iota(jnp.int32, st.shape, 0), None, k)
        return v, r

    s1, i1 = half(0)
    s2, i2 = half(1)
    g = SUBLANES
    sub = lax.broadcasted_iota(jnp.int32, (g, tt), 0)
    e1 = i1 * PEER_KEYS
    pieces = []
    for b in range(g):
        lim = min(g, k // (b + 1))
        sc = s1[0:g] + s2[b:b + 1]
        if lim < g:
            sc = jnp.where(sub < lim, sc, -jnp.inf)
        pieces.append((sc, sub * k + b, e1[0:g] + i2[b:b + 1]))
    pieces.append((s1[g:k] + s2[0:1], (sub + g) * k, e1[g:k] + i2[0:1]))
    pieces.append((s1[0:1] + s2[g:k], sub + g, e1[0:1] + i2[g:k]))
    cand_s = jnp.concatenate([p[0] for p in pieces], axis=0)
    cand_k = jnp.concatenate([p[1] for p in pieces], axis=0)
    cand_e = jnp.concatenate([p[2] for p in pieces], axis=0)
    top_s, _, top_e = _topk_rows(cand_s, cand_k, cand_e, k)
    e = jnp.exp(top_s - top_s[0:1])
    gate_ref[0] = e / jnp.sum(e, axis=0, keepdims=True)
    idx_ref[0] = top_e


def peer_route(qry, sub_keys, *, tt):
    m = qry.shape[0]
    nh = PEER_HEADS
    return pl.pallas_call(
        _peer_route_kernel,
        out_shape=(jax.ShapeDtypeStruct((nh, PEER_TOPK, m), jnp.int32),
                   jax.ShapeDtypeStruct((nh, PEER_TOPK, m), F32)),
        grid=(m // tt, nh),
        in_specs=[
            pl.BlockSpec((tt, 2 * PEER_HALF), lambda i, h: (i, h)),
            pl.BlockSpec((1, 2, PEER_KEYS, PEER_HALF), lambda i, h: (h, 0, 0, 0)),
        ],
        out_specs=(pl.BlockSpec((1, PEER_TOPK, tt), lambda i, h: (h, 0, i)),
                   pl.BlockSpec((1, PEER_TOPK, tt), lambda i, h: (h, 0, i))),
        compiler_params=_cparams(("parallel", "arbitrary")),
        name="peer_route",
    )(qry, sub_keys)


def _gelu_exact(x):
    return 0.5 * x * (1.0 + lax.erf(x * (2.0 ** -0.5)))


def _peer_expert_kernel(idx_ref, idxn_ref, gate_ref, h_ref, x_ref, pool_ref, poolt_ref, uv_hbm,
                        o_ref, *scratch, tb, ne, nbuf, dist):
    bufs, sem = scratch[:nbuf], scratch[nbuf]
    i = pl.program_id(0)
    n = pl.num_programs(0)
    rows = tb * ne
    d_sub = h_ref.shape[1]
    n_seg = 2 * tb
    seg = rows // n_seg

    def issue_rows(iref, row0, buf, which, r_lo, r_hi):
        for r in range(r_lo, r_hi):
            e = iref[row0 + r // ne, r % ne]
            pltpu.make_async_copy(uv_hbm.at[e], buf.at[r], sem.at[which, r % 2]).start(priority=r % 2)

    def wait_tile(buf, which):
        for k in range(2):
            pltpu.make_async_copy(uv_hbm.at[pl.ds(0, rows // 2)], buf.at[pl.ds(0, rows // 2)],
                                  sem.at[which, k]).wait()

    @pl.when(i == 0)
    def _():
        for a in range(dist):
            def tok(t, carry, a=a):
                for j in range(ne):
                    pltpu.make_async_copy(uv_hbm.at[idx_ref[a * tb + t, j]], bufs[a].at[t * ne + j],
                                          sem.at[a, j % 2]).start(priority=j % 2)
                return carry
            lax.fori_loop(0, tb, tok, 0)

    blk = 2 * d_sub
    sub_id = lax.broadcasted_iota(jnp.int32, (d_sub, ne * blk), 0)
    col_id = lax.broadcasted_iota(jnp.int32, (d_sub, ne * blk), 1)
    diag_u = (col_id % blk) == sub_id
    diag_v = (col_id % blk) == sub_id + d_sub

    def tile(row0, buf, nxt_iref, nxt_row0, nxt_buf, nxt_which):
        drows = []
        for t in range(tb):
            z = buf[pl.ds(t * ne, ne)].reshape(ne * blk, LANES)
            c = lax.dot_general(h_ref[row0 + t], z, (((1,), (1,)), ((), ())), preferred_element_type=F32)
            drows.append(jnp.sum(jnp.where(diag_u, c, 0.0), axis=0, keepdims=True))
            issue_rows(nxt_iref, nxt_row0, nxt_buf, nxt_which, t * seg, (t + 1) * seg)
        dall = jnp.concatenate(drows, axis=0)
        d_hi = dall.astype(BF16)
        d_lo = (dall - d_hi.astype(F32)).astype(BF16)
        a = (jnp.dot(d_hi, pool_ref[...], preferred_element_type=F32)
             + jnp.dot(d_lo, pool_ref[...], preferred_element_type=F32))
        w = gate_ref[pl.ds(row0, tb), :] * _gelu_exact(a)
        wrep = jnp.dot(w.astype(BF16), poolt_ref[...], preferred_element_type=F32)
        for t in range(tb):
            wexp = jnp.where(diag_v, wrep[t:t + 1, :], 0.0).astype(BF16)
            z = buf[pl.ds(t * ne, ne)].reshape(ne * blk, LANES)
            o_ref[row0 + t] = x_ref[row0 + t] + jnp.dot(wexp, z, preferred_element_type=F32)
            issue_rows(nxt_iref, nxt_row0, nxt_buf, nxt_which, (tb + t) * seg, (tb + t + 1) * seg)

    for a in range(nbuf):
        wait_tile(bufs[a], a)
        nxt = a + dist
        if nxt < nbuf:
            tile(a * tb, bufs[a], idx_ref, nxt * tb, bufs[nxt], nxt)
        else:
            tile(a * tb, bufs[a], idxn_ref, (nxt - nbuf) * tb, bufs[nxt - nbuf], nxt - nbuf)

    @pl.when(i == n - 1)
    def _():
        for a in range(dist):
            wait_tile(bufs[a], a)


def peer_experts(idx, gates, h3, x3, uv3, *, tb):
    m, ne = idx.shape
    d_sub = h3.shape[1]
    nbuf, dist = 4, 2
    ts = nbuf * tb
    n = m // ts
    rows = tb * ne
    blk = 2 * d_sub
    cid = jnp.arange(ne * blk) // blk
    pool = (cid[:, None] == jnp.arange(ne)[None, :]).astype(BF16)
    tok = lambda i: (i, 0)
    tok3 = lambda i: (i, 0, 0)
    return pl.pallas_call(
        functools.partial(_peer_expert_kernel, tb=tb, ne=ne, nbuf=nbuf, dist=dist),
        out_shape=jax.ShapeDtypeStruct(x3.shape, F32),
        grid=(n,),
        in_specs=[
            pl.BlockSpec((ts, ne), tok, memory_space=pltpu.SMEM),
            pl.BlockSpec((ts, ne), lambda i: (jnp.minimum(i + 1, n - 1), 0), memory_space=pltpu.SMEM),
            pl.BlockSpec((ts, ne), tok),
            pl.BlockSpec((ts, d_sub, LANES), tok3),
            pl.BlockSpec((ts, d_sub, LANES), tok3),
            pl.BlockSpec((ne * blk, ne), lambda i: (0, 0)),
            pl.BlockSpec((ne, ne * blk), lambda i: (0, 0)),
            pl.BlockSpec(memory_space=pl.ANY),
        ],
        out_specs=pl.BlockSpec((ts, d_sub, LANES), tok3),
        scratch_shapes=[pltpu.VMEM((rows, blk, LANES), BF16) for _ in range(nbuf)]
        + [pltpu.SemaphoreType.DMA((nbuf, 2))],
        compiler_params=_cparams(("arbitrary",)),
        name="peer_experts",
    )(idx, idx, gates, h3, x3, pool, pool.T, uv3)


def kernel(x, mem, positions, attn_norm_g, mem_norm_g, w_in, b_gate, diff_q_norm_g, diff_k_norm_g, lambda_q1, lambda_k1, lambda_q2, lambda_k2, diff_subln_g, rel_bias_table, conv_w, a_log, dt_bias, delta_out_norm_g, w_mem_kv, mem_q_norm_g, mem_k_norm_g, w_br_diff, w_br_delta, w_br_mem, w_out, ffn_norm_g, w_query, sub_keys, expert_u, expert_v):
    del positions
    b, t, d = x.shape
    mtok = mem.shape[1]
    m = b * t
    depth = w_in.shape[0]
    nh = N_DIFF_HEADS
    qk_w = nh * 2 * DIFF_HEAD_DIM
    dv_w = nh * 2 * DIFF_HEAD_DIM
    dl_w = N_DELTA_HEADS * DELTA_HEAD_DIM
    mem_w = N_MEM_HEADS * MEM_HEAD_DIM
    tq = min(256, t)
    tt_delta = min(256, t)

    x2d = x.reshape(m, d)
    for l in range(depth):
        lam_init = 0.8 - 0.6 * math.exp(-0.3 * l)
        wl = w_in[l]
        o = 0
        w_qk = wl[:, o:o + 2 * qk_w]; o += 2 * qk_w
        w_dv = wl[:, o:o + dv_w]; o += dv_w
        w_lqkv = wl[:, o:o + 3 * dl_w]; o += 3 * dl_w
        w_lz = wl[:, o:o + dl_w]; o += dl_w
        w_ab = wl[:, o:o + 2 * N_DELTA_HEADS]; o += 2 * N_DELTA_HEADS
        w_mq = wl[:, o:o + mem_w]; o += mem_w
        w_gate = wl[:, o:o + 3 * d]

        qk_gain = jnp.concatenate([jnp.tile(diff_q_norm_g[l], 2 * nh) * (DIFF_HEAD_DIM ** -0.5 * LOG2E),
                                   jnp.tile(diff_k_norm_g[l], 2 * nh)])
        qk = norm_matmul(x2d, w_qk.astype(BF16), norm_g=attn_norm_g[l], group=DIFF_HEAD_DIM, group_gain=qk_gain)
        w_rest = jnp.concatenate([w_dv, w_lqkv, w_lz, w_gate], axis=1).astype(BF16)
        rest = norm_matmul(x2d, w_rest, norm_g=attn_norm_g[l])
        mq_gain = jnp.tile(mem_q_norm_g[l], N_MEM_HEADS) * (MEM_HEAD_DIM ** -0.5)
        mq = norm_matmul(x2d, w_mq.astype(BF16), norm_g=attn_norm_g[l], group=MEM_HEAD_DIM, group_gain=mq_gain)
        w_ab_pad = jnp.pad(w_ab, ((0, 0), (0, LANES - 2 * N_DELTA_HEADS))).astype(BF16)
        ab = norm_matmul(x2d, w_ab_pad, norm_g=attn_norm_g[l], out_dtype=F32, tn=LANES)
        abt = ab[:, :2 * N_DELTA_HEADS].T

        lam = (jnp.exp(jnp.sum(lambda_q1[l].astype(F32) * lambda_k1[l].astype(F32)))
               - jnp.exp(jnp.sum(lambda_q2[l].astype(F32) * lambda_k2[l].astype(F32))) + lam_init)
        bias_tiles = rel_bias_tiles(rel_bias_table, tq)
        scal = jnp.concatenate([jnp.stack([lam, jnp.asarray(1.0 - lam_init, F32)]),
                                rel_bias_table[REL_BUCKETS - 1].astype(F32) * LOG2E])
        y_diff = diff_attention(qk, rest, bias_tiles, scal, diff_subln_g[l], b=b, t=t, tq=tq, hpb=2, v_col0=0)

        head_params = jnp.stack([a_log[l], dt_bias[l]]).astype(F32)
        y_delta = gated_deltanet(rest, ab, abt, conv_w[l].astype(F32), head_params, delta_out_norm_g[l],
                                 b=b, t=t, tt=tt_delta, nhb=8, qkv_col0=dv_w // LANES,
                                 z_col0=(dv_w + 3 * dl_w) // LANES)

        mem2d = mem.reshape(b * mtok, d)
        wkv = w_mem_kv[l].astype(BF16)
        mk = norm_matmul(mem2d, wkv[:, :mem_w], norm_g=mem_norm_g[l], group=MEM_HEAD_DIM,
                         group_gain=jnp.tile(mem_k_norm_g[l], N_MEM_HEADS))
        mv = norm_matmul(mem2d, wkv[:, mem_w:], norm_g=mem_norm_g[l])
        y_mem = memory_attention(mq, mk, mv, b=b, t=t, mtok=mtok, tq=min(512, t))

        x1, h2 = merge_project(x2d, y_diff, y_delta, y_mem, rest, b_gate[l],
                               w_br_diff[l].astype(BF16), w_br_delta[l].astype(BF16), w_br_mem[l].astype(BF16),
                               w_out[l].astype(BF16), ffn_norm_g[l],
                               gate_col0=dv_w + 4 * dl_w, tm=min(512, m))

        qry = norm_matmul(h2, w_query[l].astype(BF16))
        idx_t, gate_t = peer_route(qry, sub_keys[l].astype(BF16), tt=min(512, m))
        ne = PEER_HEADS * PEER_TOPK
        idx = idx_t.reshape(ne, m).T
        gates = gate_t.reshape(ne, m).T
        sub = d // LANES
        uv3 = jnp.concatenate([expert_u[l].reshape(-1, sub, LANES), expert_v[l].reshape(-1, sub, LANES)],
                              axis=1).astype(BF16)
        x2 = peer_experts(idx, gates, h2.reshape(m, sub, LANES), x1.reshape(m, sub, LANES), uv3, tb=8)
        x2d = x2.reshape(m, d)
    return x2d.reshape(b, t, d)
```

```python
import functools
import math

import jax
import jax.numpy as jnp
from jax import lax
from jax.experimental import pallas as pl
from jax.experimental.pallas import tpu as pltpu

F32 = jnp.float32
BF16 = jnp.bfloat16
EPS = 1e-6
NEG = -1e30
LOG2E = math.log2(math.e)

N_DIFF_HEADS = 8
DIFF_HEAD_DIM = 64
N_DELTA_HEADS = 8
DELTA_HEAD_DIM = 128
CONV_WIDTH = 4
CHUNK = 64
N_MEM_HEADS = 4
MEM_HEAD_DIM = 256
REL_BUCKETS = 32
REL_MAX_DIST = 128
PEER_HEADS = 8
PEER_KEYS = 128
PEER_TOPK = 16
PEER_HALF = 128
LANES = 128
SUBLANES = 8
VMEM_LIMIT = 56 * 1024 * 1024


def _cparams(sem):
    return pltpu.CompilerParams(dimension_semantics=sem, vmem_limit_bytes=VMEM_LIMIT)


def _norm_matmul_kernel(*refs, has_norm, has_group):
    it = iter(refs)
    x_ref = next(it)
    g_ref = next(it) if has_norm else None
    w_ref = next(it)
    gm_ref = next(it) if has_group else None
    gain_ref = next(it) if has_group else None
    o_ref = next(it)
    h_ref = next(it)

    @pl.when(pl.program_id(1) == 0)
    def _():
        x = x_ref[...].astype(F32)
        if has_norm:
            ms = jnp.mean(x * x, axis=-1, keepdims=True)
            x = x * lax.rsqrt(ms + EPS) * g_ref[...]
        h_ref[...] = x.astype(BF16)

    y = jnp.dot(h_ref[...], w_ref[...], preferred_element_type=F32)
    if has_group:
        ms = jnp.dot((y * y).astype(BF16), gm_ref[...], preferred_element_type=F32)
        y = y * lax.rsqrt(ms + EPS) * gain_ref[...]
    o_ref[...] = y.astype(o_ref.dtype)


def norm_matmul(x, w, *, norm_g=None, group=None, group_gain=None, out_dtype=BF16, tm=1024, tn=None):
    m, k = x.shape
    n = w.shape[1]
    has_norm = norm_g is not None
    has_group = group is not None
    if tn is None:
        tn = 512 if has_group else min(n, 2048)
    tm = min(tm, m)
    assert m % tm == 0 and n % tn == 0
    in_specs = [pl.BlockSpec((tm, k), lambda i, j: (i, 0))]
    args = [x]
    if has_norm:
        in_specs.append(pl.BlockSpec((1, k), lambda i, j: (0, 0)))
        args.append(norm_g.reshape(1, k).astype(F32))
    in_specs.append(pl.BlockSpec((k, tn), lambda i, j: (0, j)))
    args.append(w)
    if has_group:
        gid = jnp.arange(tn) // group
        gm = jnp.where(gid[:, None] == gid[None, :], 1.0 / group, 0.0).astype(BF16)
        in_specs.append(pl.BlockSpec((tn, tn), lambda i, j: (0, 0)))
        args.append(gm)
        in_specs.append(pl.BlockSpec((1, tn), lambda i, j: (0, j)))
        args.append(group_gain.reshape(1, n).astype(F32))
    return pl.pallas_call(
        functools.partial(_norm_matmul_kernel, has_norm=has_norm, has_group=has_group),
        out_shape=jax.ShapeDtypeStruct((m, n), out_dtype),
        grid=(m // tm, n // tn),
        in_specs=in_specs,
        out_specs=pl.BlockSpec((tm, tn), lambda i, j: (i, j)),
        scratch_shapes=[pltpu.VMEM((tm, k), BF16)],
        compiler_params=_cparams(("parallel", "arbitrary")),
        name="norm_matmul",
    )(*args)


def _t5_bucket(n):
    max_exact = REL_BUCKETS // 2
    nf = jnp.maximum(n, 1).astype(F32)
    large = max_exact + (jnp.log(nf / max_exact) / math.log(REL_MAX_DIST / max_exact)
                         * (REL_BUCKETS - max_exact)).astype(jnp.int32)
    large = jnp.minimum(large, REL_BUCKETS - 1)
    return jnp.where(n < max_exact, n, large)


def _rel_bias_kernel(table_ref, o_ref, *, tq):
    hm = pl.program_id(0)
    c = lax.broadcasted_iota(jnp.int32, (2 * tq, tq), 0)
    r = lax.broadcasted_iota(jnp.int32, (2 * tq, tq), 1)
    bucket = _t5_bucket(jnp.maximum(r - c + tq, 0))
    acc = jnp.zeros((2 * tq, tq), F32)
    for b in range(REL_BUCKETS):
        acc = jnp.where(bucket == b, table_ref[b, hm], acc)
    o_ref[0] = acc * LOG2E


def rel_bias_tiles(rel_table, tq):
    nmaps = rel_table.shape[1]
    return pl.pallas_call(
        functools.partial(_rel_bias_kernel, tq=tq),
        out_shape=jax.ShapeDtypeStruct((nmaps, 2 * tq, tq), F32),
        grid=(nmaps,),
        in_specs=[pl.BlockSpec(memory_space=pltpu.SMEM)],
        out_specs=pl.BlockSpec((1, 2 * tq, tq), lambda i: (i, 0, 0)),
        compiler_params=_cparams(("arbitrary",)),
        name="rel_bias_tiles",
    )(rel_table.astype(F32))


def _diff_attn_kernel(scal_ref, q_ref, k_ref, vt_ref, bias_ref, subg_ref, o_ref, *, tq, hpb):
    hb = pl.program_id(1)
    i = pl.program_id(2)
    heads = range(hpb)
    lam = scal_ref[0]
    out_scale = scal_ref[1]
    w2 = 2 * tq

    sub = lax.broadcasted_iota(jnp.int32, (LANES, tq), 0)
    col2 = lax.broadcasted_iota(jnp.int32, (1, w2), 1)
    qt, cfar = [], []
    for hd in heads:
        q = q_ref[:, hd * LANES:(hd + 1) * LANES].astype(F32).T
        qt.append(jnp.concatenate([jnp.where(sub < DIFF_HEAD_DIM, q, 0.0),
                                   jnp.where(sub >= DIFF_HEAD_DIM, q, 0.0)], axis=1).astype(BF16))
        h = hb * hpb + hd
        cfar.append(jnp.where(col2 < tq, scal_ref[2 + 2 * h], scal_ref[3 + 2 * h]))

    def update(carry, s, vt, shift=None):
        m, l, acc = carry
        colmax = [jnp.max(x, axis=0, keepdims=True) for x in s]
        if shift is not None:
            colmax = [a + c for a, c in zip(colmax, shift)]
        m_new = [jnp.maximum(a, b_) for a, b_ in zip(m, colmax)]
        alpha = [jnp.exp2(a - b_) for a, b_ in zip(m, m_new)]
        sub_ = m_new if shift is None else [a - c for a, c in zip(m_new, shift)]
        p = [jnp.exp2(x - a) for x, a in zip(s, sub_)]
        l = [a * b_ + jnp.sum(x, axis=0, keepdims=True) for a, b_, x in zip(l, alpha, p)]
        acc = [a * b_ + jnp.dot(v_, x.astype(BF16), preferred_element_type=F32)
               for a, b_, x, v_ in zip(acc, alpha, p, vt)]
        return m_new, l, acc

    def scores(tile0, ntiles):
        rows = pl.ds(pl.multiple_of(tile0 * tq, tq), ntiles * tq)
        s = [jnp.dot(k_ref[rows, hd * LANES:(hd + 1) * LANES], qt[hd], preferred_element_type=F32) for hd in heads]
        vt = [jnp.concatenate([vt_ref[0, hd, tile0 + c] for c in range(ntiles)], axis=1) if ntiles > 1
              else vt_ref[0, hd, tile0] for hd in heads]
        return s, vt

    def pack(carry):
        m, l, acc = carry
        return tuple(m) + tuple(l) + tuple(acc)

    def unpack(flat):
        return list(flat[0:hpb]), list(flat[hpb:2 * hpb]), list(flat[2 * hpb:3 * hpb])

    def far_step(j, flat):
        s, vt = scores(2 * j, 2)
        return pack(update(unpack(flat), s, vt, cfar))

    def far_single(flat):
        s, vt = scores(i - 2, 1)
        return pack(update(unpack(flat), s, vt, cfar))

    n_far = jnp.maximum(i - 1, 0)
    init = ([jnp.full((1, w2), NEG, F32) for _ in heads], [jnp.zeros((1, w2), F32) for _ in heads],
            [jnp.zeros((LANES, w2), F32) for _ in heads])
    flat = lax.fori_loop(0, n_far // 2, far_step, pack(init))
    flat = lax.cond(n_far % 2 == 1, far_single, lambda c: c, flat)
    carry = unpack(flat)

    def near_bias(hd, lo):
        return jnp.concatenate([bias_ref[2 * hd, lo:lo + tq, :], bias_ref[2 * hd + 1, lo:lo + tq, :]], axis=1)

    s, vt = scores(jnp.maximum(i - 1, 0), 1)
    s = [jnp.where(i >= 1, x + near_bias(hd, 0), NEG) for hd, x in zip(heads, s)]
    carry = update(carry, s, vt)

    s, vt = scores(i, 1)
    kk = lax.broadcasted_iota(jnp.int32, (tq, w2), 0)
    qi = lax.broadcasted_iota(jnp.int32, (tq, w2), 1)
    qi = jnp.where(qi >= tq, qi - tq, qi)
    s = [jnp.where(kk <= qi, x + near_bias(hd, tq), NEG) for hd, x in zip(heads, s)]
    m, l, acc = update(carry, s, vt)

    outs = []
    for hd in heads:
        o = acc[hd] / l[hd]
        o = o[:, 0:tq] - lam * o[:, tq:w2]
        ms = jnp.mean(o * o, axis=0, keepdims=True)
        o = o * lax.rsqrt(ms + EPS) * subg_ref[...] * out_scale
        outs.append(o.T)
    o_ref[...] = jnp.concatenate(outs, axis=-1).astype(o_ref.dtype)


def diff_attention(qk, vt, bias_tiles, scal, subln_g, *, b, t, tq, hpb):
    m = b * t
    nq = t // tq
    nh = N_DIFF_HEADS
    wl = hpb * LANES
    assert nh % hpb == 0
    return pl.pallas_call(
        functools.partial(_diff_attn_kernel, tq=tq, hpb=hpb),
        out_shape=jax.ShapeDtypeStruct((m, nh * LANES), BF16),
        grid=(b, nh // hpb, nq),
        in_specs=[
            pl.BlockSpec(memory_space=pltpu.SMEM),
            pl.BlockSpec((tq, wl), lambda bi, h, i: (bi * nq + i, h)),
            pl.BlockSpec((t, wl), lambda bi, h, i: (bi, nh // hpb + h)),
            pl.BlockSpec((1, hpb, nq, LANES, tq), lambda bi, h, i: (bi, h, 0, 0, 0)),
            pl.BlockSpec((2 * hpb, 2 * tq, tq), lambda bi, h, i: (h, 0, 0)),
            pl.BlockSpec((LANES, 1), lambda bi, h, i: (0, 0)),
        ],
        out_specs=pl.BlockSpec((tq, wl), lambda bi, h, i: (bi * nq + i, h)),
        compiler_params=_cparams(("parallel", "parallel", "arbitrary")),
        name="diff_attention",
    )(scal, qk, qk, vt, bias_tiles, subln_g.reshape(LANES, 1).astype(F32))


def _sigmoid(x):
    return 1.0 / (1.0 + jnp.exp(-x))


def _softplus(x):
    return jnp.maximum(x, 0.0) + jnp.log(1.0 + jnp.exp(-jnp.abs(x)))


def _delta_kernel(hp_ref, xq_ref, xk_ref, xv_ref, pq_ref, pk_ref, pv_ref, wq_ref, wk_ref, wv_ref,
                  ab_ref, abt_ref, z_ref, og_ref, o_ref, s_ref, *, tt, nhb):
    hb = pl.program_id(1)
    i = pl.program_id(2)
    nc = tt // CHUNK
    dh = DELTA_HEAD_DIM
    heads = range(nhb)

    @pl.when(i == 0)
    def _():
        s_ref[...] = jnp.zeros_like(s_ref)

    def conv_silu(x_ref, p_ref, w_ref):
        prev = jnp.where(i > 0, p_ref[...].astype(F32), 0.0)
        xf = jnp.concatenate([prev, x_ref[...].astype(F32)], axis=0)
        w = w_ref[...]
        base = SUBLANES - (CONV_WIDTH - 1)
        y = xf[base:base + tt] * w[0:1]
        for c in range(1, CONV_WIDTH):
            y = y + xf[base + c:base + c + tt] * w[c:c + 1]
        return y * _sigmoid(y)

    def split(x):
        return [x[:, hd * dh:(hd + 1) * dh] for hd in heads]

    q = split(conv_silu(xq_ref, pq_ref, wq_ref))
    k = split(conv_silu(xk_ref, pk_ref, wk_ref))
    v = split(conv_silu(xv_ref, pv_ref, wv_ref))
    q = [x * lax.rsqrt(jnp.sum(x * x, axis=-1, keepdims=True) + EPS) * (dh ** -0.5) for x in q]
    k = [x * lax.rsqrt(jnp.sum(x * x, axis=-1, keepdims=True) + EPS) for x in k]

    ab = ab_ref[...]
    lane = lax.broadcasted_iota(jnp.int32, ab.shape, 1)
    beta_col, g_col, g_row = [], [], []
    for hd in heads:
        h = hb * nhb + hd
        neg_a = -jnp.exp(jnp.full((1, 1), hp_ref[0, h], F32))
        dt_bias = hp_ref[1, h]
        lb_col = jnp.sum(jnp.where(lane == h, ab, 0.0), axis=-1, keepdims=True)
        la_col = jnp.sum(jnp.where(lane == N_DELTA_HEADS + h, ab, 0.0), axis=-1, keepdims=True)
        la_row = abt_ref[pl.ds(N_DELTA_HEADS + h, 1), :]
        beta_col.append(_sigmoid(lb_col))
        g_col.append(neg_a * _softplus(la_col + dt_bias))
        g_row.append(neg_a * _softplus(la_row + dt_bias))

    ri = lax.broadcasted_iota(jnp.int32, (tt, tt), 0)
    ci = lax.broadcasted_iota(jnp.int32, (tt, tt), 1)
    same = (ri // CHUNK) == (ci // CHUNK)
    tril = same & (ri >= ci)
    strict = same & (ri > ci)
    triu = same & (ri <= ci)

    def mm(a, b):
        return jnp.dot(a.astype(BF16), b.astype(BF16), preferred_element_type=F32)

    def mm_nt(a, b):
        return lax.dot_general(a.astype(BF16), b.astype(BF16), (((1,), (1,)), ((), ())),
                               preferred_element_type=F32)

    def mm_tn(a, b):
        return lax.dot_general(a.astype(BF16), b.astype(BF16), (((0,), (0,)), ((), ())),
                               preferred_element_type=F32)

    gc_col = [jnp.sum(jnp.where(tril, g, 0.0), axis=-1, keepdims=True) for g in g_row]
    gc_row = [jnp.sum(jnp.where(triu, g, 0.0), axis=0, keepdims=True) for g in g_col]
    decay = [jnp.where(tril, jnp.exp(jnp.where(tril, a - b, 0.0)), 0.0) for a, b in zip(gc_col, gc_row)]
    e_gc = [jnp.exp(g) for g in gc_col]
    kk = [mm_nt(x, x) for x in k]
    qk = [mm_nt(a, b) for a, b in zip(q, k)]
    xm = [jnp.where(strict, -(b * kx * d), 0.0) for b, kx, d in zip(beta_col, kk, decay)]
    intra = [jnp.where(tril, a * d, 0.0) for a, d in zip(qk, decay)]
    r = [jnp.concatenate([b * vx, (b * e) * kx], axis=-1) for b, vx, e, kx in zip(beta_col, v, e_gc, k)]
    n_sq = CHUNK.bit_length() - 1
    for lvl in range(n_sq):
        r = [x + mm(m_, x) for m_, x in zip(xm, r)]
        if lvl + 1 < n_sq:
            xm = [mm(m_, m_) for m_ in xm]
    u = [x[:, 0:dh] for x in r]
    w = [x[:, dh:2 * dh] for x in r]
    qg = [a * e for a, e in zip(q, e_gc)]

    state = [s_ref[hd] for hd in heads]
    o_state = [[] for _ in heads]
    v_new = [[] for _ in heads]
    for c in range(nc):
        r0 = c * CHUNK
        sl = slice(r0, r0 + CHUNK)
        g_last = [g[r0 + CHUNK - 1:r0 + CHUNK] for g in gc_col]
        vn = [u[hd][sl] - mm(w[hd][sl], state[hd]) for hd in heads]
        os_ = [mm(qg[hd][sl], state[hd]) for hd in heads]
        k_dec = [k[hd][sl] * jnp.exp(g_last[hd] - gc_col[hd][sl]) for hd in heads]
        state = [state[hd] * jnp.exp(g_last[hd]) + mm_tn(k_dec[hd], vn[hd]) for hd in heads]
        for hd in heads:
            v_new[hd].append(vn[hd])
            o_state[hd].append(os_[hd])
    for hd in heads:
        s_ref[hd] = state[hd]

    outs = []
    for hd in heads:
        o = jnp.concatenate(o_state[hd], axis=0) + mm(intra[hd], jnp.concatenate(v_new[hd], axis=0))
        ms = jnp.mean(o * o, axis=-1, keepdims=True)
        outs.append(o * lax.rsqrt(ms + EPS) * og_ref[...])
    z = z_ref[...].astype(F32)
    o_ref[...] = (jnp.concatenate(outs, axis=-1) * (z * _sigmoid(z))).astype(o_ref.dtype)


def gated_deltanet(rest, ab, abt, conv_w, head_params, out_g, *, b, t, tt, nhb, qkv_col0, z_col0):
    m = b * t
    nt = t // tt
    nh = N_DELTA_HEADS
    sub = tt // SUBLANES
    wl = nhb * LANES
    assert nh % nhb == 0 and qkv_col0 % nhb == 0 and z_col0 % nhb == 0

    def cur(off):
        return pl.BlockSpec((tt, wl), lambda bi, h, i: (bi * nt + i, off // nhb + h))

    def prev(off):
        return pl.BlockSpec((SUBLANES, wl),
                            lambda bi, h, i: (jnp.maximum((bi * nt + i) * sub - 1, 0), off // nhb + h))

    def wspec(off):
        return pl.BlockSpec((CONV_WIDTH, wl), lambda bi, h, i: (0, off // nhb + h))

    return pl.pallas_call(
        functools.partial(_delta_kernel, tt=tt, nhb=nhb),
        out_shape=jax.ShapeDtypeStruct((m, nh * LANES), BF16),
        grid=(b, nh // nhb, nt),
        in_specs=[
            pl.BlockSpec(memory_space=pltpu.SMEM),
            cur(qkv_col0), cur(qkv_col0 + nh), cur(qkv_col0 + 2 * nh),
            prev(qkv_col0), prev(qkv_col0 + nh), prev(qkv_col0 + 2 * nh),
            wspec(0), wspec(nh), wspec(2 * nh),
            pl.BlockSpec((tt, LANES), lambda bi, h, i: (bi * nt + i, 0)),
            pl.BlockSpec((2 * nh, tt), lambda bi, h, i: (0, bi * nt + i)),
            cur(z_col0),
            pl.BlockSpec((1, LANES), lambda bi, h, i: (0, 0)),
        ],
        out_specs=pl.BlockSpec((tt, wl), lambda bi, h, i: (bi * nt + i, h)),
        scratch_shapes=[pltpu.VMEM((nhb, DELTA_HEAD_DIM, DELTA_HEAD_DIM), F32)],
        compiler_params=_cparams(("parallel", "parallel", "arbitrary")),
        name="gated_deltanet",
    )(head_params, rest, rest, rest, rest, rest, rest, conv_w, conv_w, conv_w, ab, abt, rest,
      out_g.reshape(1, LANES).astype(F32))


def _mem_attn_kernel(q_ref, k_ref, v_ref, o_ref):
    outs = []
    for h in range(N_MEM_HEADS):
        sl = slice(h * MEM_HEAD_DIM, (h + 1) * MEM_HEAD_DIM)
        s = lax.dot_general(q_ref[:, sl], k_ref[:, sl], (((1,), (1,)), ((), ())), preferred_element_type=F32)
        s = s - jnp.max(s, axis=-1, keepdims=True)
        p = jnp.exp(s)
        p = p / jnp.sum(p, axis=-1, keepdims=True)
        outs.append(jnp.dot(p.astype(BF16), v_ref[:, sl], preferred_element_type=F32))
    o_ref[...] = jnp.concatenate(outs, axis=-1).astype(o_ref.dtype)


def memory_attention(mq, mk, mv, *, b, t, mtok, tq):
    nq = t // tq
    w = N_MEM_HEADS * MEM_HEAD_DIM
    return pl.pallas_call(
        _mem_attn_kernel,
        out_shape=jax.ShapeDtypeStruct((b * t, w), BF16),
        grid=(b, nq),
        in_specs=[
            pl.BlockSpec((tq, w), lambda bi, i: (bi * nq + i, 0)),
            pl.BlockSpec((mtok, w), lambda bi, i: (bi, 0)),
            pl.BlockSpec((mtok, w), lambda bi, i: (bi, 0)),
        ],
        out_specs=pl.BlockSpec((tq, w), lambda bi, i: (bi * nq + i, 0)),
        compiler_params=_cparams(("parallel", "arbitrary")),
        name="memory_attention",
    )(mq, mk, mv)


def _merge_kernel(x_ref, ya_ref, yb_ref, yc_ref, ga_ref, gb_ref, gc_ref, bg_ref,
                  wa_ref, wb_ref, wc_ref, wo_ref, fg_ref, x1_ref, h2_ref):
    def branch(y_ref, g_ref, w_ref, idx):
        gate = _sigmoid(g_ref[...].astype(F32) + bg_ref[idx:idx + 1, :])
        return gate * jnp.dot(y_ref[...], w_ref[...], preferred_element_type=F32)

    merged = branch(ya_ref, ga_ref, wa_ref, 0) + branch(yb_ref, gb_ref, wb_ref, 1) + branch(yc_ref, gc_ref, wc_ref, 2)
    x1 = x_ref[...] + jnp.dot(merged.astype(BF16), wo_ref[...], preferred_element_type=F32)
    x1_ref[...] = x1
    ms = jnp.mean(x1 * x1, axis=-1, keepdims=True)
    h2_ref[...] = (x1 * lax.rsqrt(ms + EPS) * fg_ref[...]).astype(h2_ref.dtype)


def merge_project(x2d, y_diff, y_delta, y_mem, rest, b_gate, w_a, w_b, w_c, w_o, ffn_g, *, gate_col0, tm):
    m, d = x2d.shape
    row = lambda i: (i, 0)
    const = lambda i: (0, 0)
    assert gate_col0 % d == 0

    def gspec(j):
        return pl.BlockSpec((tm, d), lambda i: (i, gate_col0 // d + j))

    return pl.pallas_call(
        _merge_kernel,
        out_shape=(jax.ShapeDtypeStruct((m, d), F32), jax.ShapeDtypeStruct((m, d), BF16)),
        grid=(m // tm,),
        in_specs=[
            pl.BlockSpec((tm, d), row), pl.BlockSpec((tm, d), row), pl.BlockSpec((tm, d), row),
            pl.BlockSpec((tm, d), row), gspec(0), gspec(1), gspec(2),
            pl.BlockSpec((3, d), const),
            pl.BlockSpec((d, d), const), pl.BlockSpec((d, d), const), pl.BlockSpec((d, d), const),
            pl.BlockSpec((d, d), const), pl.BlockSpec((1, d), const),
        ],
        out_specs=(pl.BlockSpec((tm, d), row), pl.BlockSpec((tm, d), row)),
        compiler_params=_cparams(("parallel",)),
        name="merge_project",
    )(x2d, y_diff, y_delta, y_mem, rest, rest, rest, b_gate.reshape(3, d).astype(F32),
      w_a, w_b, w_c, w_o, ffn_g.reshape(1, d).astype(F32))


def _topk_rows(s, key, extra, k):
    big = jnp.iinfo(jnp.int32).max
    vals, keys, ext = [], [], []
    for _ in range(k):
        m = jnp.max(s, axis=0, keepdims=True)
        am = jnp.min(jnp.where(s == m, key, big), axis=0, keepdims=True)
        hit = key == am
        if extra is not None:
            ext.append(jnp.max(jnp.where(hit, extra, -1), axis=0, keepdims=True))
        s = jnp.where(hit, -jnp.inf, s)
        vals.append(m)
        keys.append(am)
    cat = lambda xs: jnp.concatenate(xs, axis=0)
    return cat(vals), cat(keys), (cat(ext) if extra is not None else None)


def _peer_route_kernel(q_ref, keys_ref, idx_ref, gate_ref):
    k = PEER_TOPK
    q = q_ref[...]
    tt = q.shape[0]

    def half(p):
        qp = q[:, p * PEER_HALF:(p + 1) * PEER_HALF]
        st = lax.dot_general(keys_ref[0, p], qp, (((1,), (1,)), ((), ())), preferred_element_type=F32)
        v, r, _ = _topk_rows(st, lax.broadcasted_iota(jnp.int32, st.shape, 0), None, k)
        return v, r

    s1, i1 = half(0)
    s2, i2 = half(1)
    g = SUBLANES
    sub = lax.broadcasted_iota(jnp.int32, (g, tt), 0)
    e1 = i1 * PEER_KEYS
    pieces = []
    for b in range(g):
        lim = min(g, k // (b + 1))
        sc = s1[0:g] + s2[b:b + 1]
        if lim < g:
            sc = jnp.where(sub < lim, sc, -jnp.inf)
        pieces.append((sc, sub * k + b, e1[0:g] + i2[b:b + 1]))
    pieces.append((s1[g:k] + s2[0:1], (sub + g) * k, e1[g:k] + i2[0:1]))
    pieces.append((s1[0:1] + s2[g:k], sub + g, e1[0:1] + i2[g:k]))
    cand_s = jnp.concatenate([p[0] for p in pieces], axis=0)
    cand_k = jnp.concatenate([p[1] for p in pieces], axis=0)
    cand_e = jnp.concatenate([p[2] for p in pieces], axis=0)
    top_s, _, top_e = _topk_rows(cand_s, cand_k, cand_e, k)
    e = jnp.exp(top_s - top_s[0:1])
    gate_ref[0] = e / jnp.sum(e, axis=0, keepdims=True)
    idx_ref[0] = top_e


def peer_route(qry, sub_keys, *, tt):
    m = qry.shape[0]
    nh = PEER_HEADS
    return pl.pallas_call(
        _peer_route_kernel,
        out_shape=(jax.ShapeDtypeStruct((nh, PEER_TOPK, m), jnp.int32),
                   jax.ShapeDtypeStruct((nh, PEER_TOPK, m), F32)),
        grid=(m // tt, nh),
        in_specs=[
            pl.BlockSpec((tt, 2 * PEER_HALF), lambda i, h: (i, h)),
            pl.BlockSpec((1, 2, PEER_KEYS, PEER_HALF), lambda i, h: (h, 0, 0, 0)),
        ],
        out_specs=(pl.BlockSpec((1, PEER_TOPK, tt), lambda i, h: (h, 0, i)),
                   pl.BlockSpec((1, PEER_TOPK, tt), lambda i, h: (h, 0, i))),
        compiler_params=_cparams(("parallel", "arbitrary")),
        name="peer_route",
    )(qry, sub_keys)


def _gelu_exact(x):
    return 0.5 * x * (1.0 + lax.erf(x * (2.0 ** -0.5)))


def _peer_expert_kernel(idx_ref, idxn_ref, gate_ref, h_ref, x_ref, pool_ref, poolt_ref, uv_hbm,
                        o_ref, *scratch, tb, ne, nbuf, dist):
    bufs, sem = scratch[:nbuf], scratch[nbuf]
    i = pl.program_id(0)
    n = pl.num_programs(0)
    rows = tb * ne
    d_sub = h_ref.shape[1]
    n_seg = 2 * tb
    seg = rows // n_seg

    def issue_rows(iref, row0, buf, which, r_lo, r_hi):
        for r in range(r_lo, r_hi):
            e = iref[row0 + r // ne, r % ne]
            pltpu.make_async_copy(uv_hbm.at[e], buf.at[r], sem.at[which, r % 2]).start(priority=r % 2)

    def wait_tile(buf, which):
        for k in range(2):
            pltpu.make_async_copy(uv_hbm.at[pl.ds(0, rows // 2)], buf.at[pl.ds(0, rows // 2)],
                                  sem.at[which, k]).wait()

    @pl.when(i == 0)
    def _():
        for a in range(dist):
            def tok(t, carry, a=a):
                for j in range(ne):
                    pltpu.make_async_copy(uv_hbm.at[idx_ref[a * tb + t, j]], bufs[a].at[t * ne + j],
                                          sem.at[a, j % 2]).start(priority=j % 2)
                return carry
            lax.fori_loop(0, tb, tok, 0)

    blk = 2 * d_sub
    sub_id = lax.broadcasted_iota(jnp.int32, (d_sub, ne * blk), 0)
    col_id = lax.broadcasted_iota(jnp.int32, (d_sub, ne * blk), 1)
    diag_u = (col_id % blk) == sub_id
    diag_v = (col_id % blk) == sub_id + d_sub

    def tile(row0, buf, nxt_iref, nxt_row0, nxt_buf, nxt_which):
        drows = []
        for t in range(tb):
            z = buf[pl.ds(t * ne, ne)].reshape(ne * blk, LANES)
            c = lax.dot_general(h_ref[row0 + t], z, (((1,), (1,)), ((), ())), preferred_element_type=F32)
            drows.append(jnp.sum(jnp.where(diag_u, c, 0.0), axis=0, keepdims=True))
            issue_rows(nxt_iref, nxt_row0, nxt_buf, nxt_which, t * seg, (t + 1) * seg)
        dall = jnp.concatenate(drows, axis=0)
        d_hi = dall.astype(BF16)
        d_lo = (dall - d_hi.astype(F32)).astype(BF16)
        a = (jnp.dot(d_hi, pool_ref[...], preferred_element_type=F32)
             + jnp.dot(d_lo, pool_ref[...], preferred_element_type=F32))
        w = gate_ref[pl.ds(row0, tb), :] * _gelu_exact(a)
        wrep = jnp.dot(w.astype(BF16), poolt_ref[...], preferred_element_type=F32)
        for t in range(tb):
            wexp = jnp.where(diag_v, wrep[t:t + 1, :], 0.0).astype(BF16)
            z = buf[pl.ds(t * ne, ne)].reshape(ne * blk, LANES)
            o_ref[row0 + t] = x_ref[row0 + t] + jnp.dot(wexp, z, preferred_element_type=F32)
            issue_rows(nxt_iref, nxt_row0, nxt_buf, nxt_which, (tb + t) * seg, (tb + t + 1) * seg)

    for a in range(nbuf):
        wait_tile(bufs[a], a)
        nxt = a + dist
        if nxt < nbuf:
            tile(a * tb, bufs[a], idx_ref, nxt * tb, bufs[nxt], nxt)
        else:
            tile(a * tb, bufs[a], idxn_ref, (nxt - nbuf) * tb, bufs[nxt - nbuf], nxt - nbuf)

    @pl.when(i == n - 1)
    def _():
        for a in range(dist):
            wait_tile(bufs[a], a)


def peer_experts(idx, gates, h3, x3, uv3, *, tb):
    m, ne = idx.shape
    d_sub = h3.shape[1]
    nbuf, dist = 4, 2
    ts = nbuf * tb
    n = m // ts
    rows = tb * ne
    blk = 2 * d_sub
    cid = jnp.arange(ne * blk) // blk
    pool = (cid[:, None] == jnp.arange(ne)[None, :]).astype(BF16)
    tok = lambda i: (i, 0)
    tok3 = lambda i: (i, 0, 0)
    return pl.pallas_call(
        functools.partial(_peer_expert_kernel, tb=tb, ne=ne, nbuf=nbuf, dist=dist),
        out_shape=jax.ShapeDtypeStruct(x3.shape, F32),
        grid=(n,),
        in_specs=[
            pl.BlockSpec((ts, ne), tok, memory_space=pltpu.SMEM),
            pl.BlockSpec((ts, ne), lambda i: (jnp.minimum(i + 1, n - 1), 0), memory_space=pltpu.SMEM),
            pl.BlockSpec((ts, ne), tok),
            pl.BlockSpec((ts, d_sub, LANES), tok3),
            pl.BlockSpec((ts, d_sub, LANES), tok3),
            pl.BlockSpec((ne * blk, ne), lambda i: (0, 0)),
            pl.BlockSpec((ne, ne * blk), lambda i: (0, 0)),
            pl.BlockSpec(memory_space=pl.ANY),
        ],
        out_specs=pl.BlockSpec((ts, d_sub, LANES), tok3),
        scratch_shapes=[pltpu.VMEM((rows, blk, LANES), BF16) for _ in range(nbuf)]
        + [pltpu.SemaphoreType.DMA((nbuf, 2))],
        compiler_params=_cparams(("arbitrary",)),
        name="peer_experts",
    )(idx, idx, gates, h3, x3, pool, pool.T, uv3)


def kernel(x, mem, positions, attn_norm_g, mem_norm_g, w_in, b_gate, diff_q_norm_g, diff_k_norm_g, lambda_q1, lambda_k1, lambda_q2, lambda_k2, diff_subln_g, rel_bias_table, conv_w, a_log, dt_bias, delta_out_norm_g, w_mem_kv, mem_q_norm_g, mem_k_norm_g, w_br_diff, w_br_delta, w_br_mem, w_out, ffn_norm_g, w_query, sub_keys, expert_u, expert_v):
    del positions
    b, t, d = x.shape
    mtok = mem.shape[1]
    m = b * t
    depth = w_in.shape[0]
    nh = N_DIFF_HEADS
    qk_w = nh * 2 * DIFF_HEAD_DIM
    dv_w = nh * 2 * DIFF_HEAD_DIM
    dl_w = N_DELTA_HEADS * DELTA_HEAD_DIM
    mem_w = N_MEM_HEADS * MEM_HEAD_DIM
    tq = min(256, t)
    tt_delta = min(256, t)

    x2d = x.reshape(m, d)
    for l in range(depth):
        lam_init = 0.8 - 0.6 * math.exp(-0.3 * l)
        wl = w_in[l]
        o = 0
        w_qk = wl[:, o:o + 2 * qk_w]; o += 2 * qk_w
        w_dv = wl[:, o:o + dv_w]; o += dv_w
        w_lqkv = wl[:, o:o + 3 * dl_w]; o += 3 * dl_w
        w_lz = wl[:, o:o + dl_w]; o += dl_w
        w_ab = wl[:, o:o + 2 * N_DELTA_HEADS]; o += 2 * N_DELTA_HEADS
        w_mq = wl[:, o:o + mem_w]; o += mem_w
        w_gate = wl[:, o:o + 3 * d]

        qk_gain = jnp.concatenate([jnp.tile(diff_q_norm_g[l], 2 * nh) * (DIFF_HEAD_DIM ** -0.5 * LOG2E),
                                   jnp.tile(diff_k_norm_g[l], 2 * nh)])
        qk = norm_matmul(x2d, w_qk.astype(BF16), norm_g=attn_norm_g[l], group=DIFF_HEAD_DIM, group_gain=qk_gain)
        w_rest = jnp.concatenate([w_dv, w_lqkv, w_lz, w_gate], axis=1).astype(BF16)
        rest = norm_matmul(x2d, w_rest, norm_g=attn_norm_g[l])
        mq_gain = jnp.tile(mem_q_norm_g[l], N_MEM_HEADS) * (MEM_HEAD_DIM ** -0.5)
        mq = norm_matmul(x2d, w_mq.astype(BF16), norm_g=attn_norm_g[l], group=MEM_HEAD_DIM, group_gain=mq_gain)
        w_ab_pad = jnp.pad(w_ab, ((0, 0), (0, LANES - 2 * N_DELTA_HEADS))).astype(BF16)
        ab = norm_matmul(x2d, w_ab_pad, norm_g=attn_norm_g[l], out_dtype=F32, tn=LANES)
        abt = ab[:, :2 * N_DELTA_HEADS].T

        lam = (jnp.exp(jnp.sum(lambda_q1[l].astype(F32) * lambda_k1[l].astype(F32)))
               - jnp.exp(jnp.sum(lambda_q2[l].astype(F32) * lambda_k2[l].astype(F32))) + lam_init)
        bias_tiles = rel_bias_tiles(rel_bias_table, tq)
        scal = jnp.concatenate([jnp.stack([lam, jnp.asarray(1.0 - lam_init, F32)]),
                                rel_bias_table[REL_BUCKETS - 1].astype(F32) * LOG2E])
        vt = rest[:, :dv_w].reshape(b, t // tq, tq, nh, LANES).transpose(0, 3, 1, 4, 2)
        y_diff = diff_attention(qk, vt, bias_tiles, scal, diff_subln_g[l], b=b, t=t, tq=tq, hpb=4)

        head_params = jnp.stack([a_log[l], dt_bias[l]]).astype(F32)
        y_delta = gated_deltanet(rest, ab, abt, conv_w[l].astype(F32), head_params, delta_out_norm_g[l],
                                 b=b, t=t, tt=tt_delta, nhb=8, qkv_col0=dv_w // LANES,
                                 z_col0=(dv_w + 3 * dl_w) // LANES)

        mem2d = mem.reshape(b * mtok, d)
        wkv = w_mem_kv[l].astype(BF16)
        mk = norm_matmul(mem2d, wkv[:, :mem_w], norm_g=mem_norm_g[l], group=MEM_HEAD_DIM,
                         group_gain=jnp.tile(mem_k_norm_g[l], N_MEM_HEADS))
        mv = norm_matmul(mem2d, wkv[:, mem_w:], norm_g=mem_norm_g[l])
        y_mem = memory_attention(mq, mk, mv, b=b, t=t, mtok=mtok, tq=min(512, t))

        x1, h2 = merge_project(x2d, y_diff, y_delta, y_mem, rest, b_gate[l],
                               w_br_diff[l].astype(BF16), w_br_delta[l].astype(BF16), w_br_mem[l].astype(BF16),
                               w_out[l].astype(BF16), ffn_norm_g[l],
                               gate_col0=dv_w + 4 * dl_w, tm=min(512, m))

        qry = norm_matmul(h2, w_query[l].astype(BF16))
        idx_t, gate_t = peer_route(qry, sub_keys[l].astype(BF16), tt=min(512, m))
        ne = PEER_HEADS * PEER_TOPK
        idx = idx_t.reshape(ne, m).T
        gates = gate_t.reshape(ne, m).T
        sub = d // LANES
        uv3 = jnp.concatenate([expert_u[l].reshape(-1, sub, LANES), expert_v[l].reshape(-1, sub, LANES)],
                              axis=1).astype(BF16)
        x2 = peer_experts(idx, gates, h2.reshape(m, sub, LANES), x1.reshape(m, sub, LANES), uv3, tb=8)
        x2d = x2.reshape(m, d)
    return x2d.reshape(b, t, d)
```

```python
import functools
import math

import jax
import jax.numpy as jnp
from jax import lax
from jax.experimental import pallas as pl
from jax.experimental.pallas import tpu as pltpu

F32 = jnp.float32
BF16 = jnp.bfloat16
EPS = 1e-6
NEG = -1e30
LOG2E = math.log2(math.e)

N_DIFF_HEADS = 8
DIFF_HEAD_DIM = 64
N_DELTA_HEADS = 8
DELTA_HEAD_DIM = 128
CONV_WIDTH = 4
CHUNK = 64
N_MEM_HEADS = 4
MEM_HEAD_DIM = 256
REL_BUCKETS = 32
REL_MAX_DIST = 128
PEER_HEADS = 8
PEER_KEYS = 128
PEER_TOPK = 16
PEER_HALF = 128
LANES = 128
SUBLANES = 8
VMEM_LIMIT = 56 * 1024 * 1024


def _cparams(sem):
    return pltpu.CompilerParams(dimension_semantics=sem, vmem_limit_bytes=VMEM_LIMIT)


def _norm_matmul_kernel(*refs, has_norm, has_group):
    it = iter(refs)
    x_ref = next(it)
    g_ref = next(it) if has_norm else None
    w_ref = next(it)
    gm_ref = next(it) if has_group else None
    gain_ref = next(it) if has_group else None
    o_ref = next(it)
    h_ref = next(it)

    @pl.when(pl.program_id(1) == 0)
    def _():
        x = x_ref[...].astype(F32)
        if has_norm:
            ms = jnp.mean(x * x, axis=-1, keepdims=True)
            x = x * lax.rsqrt(ms + EPS) * g_ref[...]
        h_ref[...] = x.astype(BF16)

    y = jnp.dot(h_ref[...], w_ref[...], preferred_element_type=F32)
    if has_group:
        ms = jnp.dot((y * y).astype(BF16), gm_ref[...], preferred_element_type=F32)
        y = y * lax.rsqrt(ms + EPS) * gain_ref[...]
    o_ref[...] = y.astype(o_ref.dtype)


def norm_matmul(x, w, *, norm_g=None, group=None, group_gain=None, out_dtype=BF16, tm=1024, tn=None):
    m, k = x.shape
    n = w.shape[1]
    has_norm = norm_g is not None
    has_group = group is not None
    if tn is None:
        tn = 512 if has_group else min(n, 2048)
    tm = min(tm, m)
    assert m % tm == 0 and n % tn == 0
    in_specs = [pl.BlockSpec((tm, k), lambda i, j: (i, 0))]
    args = [x]
    if has_norm:
        in_specs.append(pl.BlockSpec((1, k), lambda i, j: (0, 0)))
        args.append(norm_g.reshape(1, k).astype(F32))
    in_specs.append(pl.BlockSpec((k, tn), lambda i, j: (0, j)))
    args.append(w)
    if has_group:
        gid = jnp.arange(tn) // group
        gm = jnp.where(gid[:, None] == gid[None, :], 1.0 / group, 0.0).astype(BF16)
        in_specs.append(pl.BlockSpec((tn, tn), lambda i, j: (0, 0)))
        args.append(gm)
        in_specs.append(pl.BlockSpec((1, tn), lambda i, j: (0, j)))
        args.append(group_gain.reshape(1, n).astype(F32))
    return pl.pallas_call(
        functools.partial(_norm_matmul_kernel, has_norm=has_norm, has_group=has_group),
        out_shape=jax.ShapeDtypeStruct((m, n), out_dtype),
        grid=(m // tm, n // tn),
        in_specs=in_specs,
        out_specs=pl.BlockSpec((tm, tn), lambda i, j: (i, j)),
        scratch_shapes=[pltpu.VMEM((tm, k), BF16)],
        compiler_params=_cparams(("parallel", "arbitrary")),
        name="norm_matmul",
    )(*args)


def _t5_bucket(n):
    max_exact = REL_BUCKETS // 2
    nf = jnp.maximum(n, 1).astype(F32)
    large = max_exact + (jnp.log(nf / max_exact) / math.log(REL_MAX_DIST / max_exact)
                         * (REL_BUCKETS - max_exact)).astype(jnp.int32)
    large = jnp.minimum(large, REL_BUCKETS - 1)
    return jnp.where(n < max_exact, n, large)


def _rel_bias_kernel(table_ref, o_ref, *, tq):
    hm = pl.program_id(0)
    c = lax.broadcasted_iota(jnp.int32, (2 * tq, tq), 0)
    r = lax.broadcasted_iota(jnp.int32, (2 * tq, tq), 1)
    bucket = _t5_bucket(jnp.maximum(r - c + tq, 0))
    acc = jnp.zeros((2 * tq, tq), F32)
    for b in range(REL_BUCKETS):
        acc = jnp.where(bucket == b, table_ref[b, hm], acc)
    o_ref[0] = acc * LOG2E


def rel_bias_tiles(rel_table, tq):
    nmaps = rel_table.shape[1]
    return pl.pallas_call(
        functools.partial(_rel_bias_kernel, tq=tq),
        out_shape=jax.ShapeDtypeStruct((nmaps, 2 * tq, tq), F32),
        grid=(nmaps,),
        in_specs=[pl.BlockSpec(memory_space=pltpu.SMEM)],
        out_specs=pl.BlockSpec((1, 2 * tq, tq), lambda i: (i, 0, 0)),
        compiler_params=_cparams(("arbitrary",)),
        name="rel_bias_tiles",
    )(rel_table.astype(F32))


def _diff_attn_kernel(scal_ref, q_ref, k_ref, vt_ref, bias_ref, subg_ref, o_ref, *, tq, hpb):
    hb = pl.program_id(1)
    i = pl.program_id(2)
    heads = range(hpb)
    lam = scal_ref[0]
    out_scale = scal_ref[1]
    w2 = 2 * tq

    sub = lax.broadcasted_iota(jnp.int32, (LANES, tq), 0)
    col2 = lax.broadcasted_iota(jnp.int32, (1, w2), 1)
    qt, cfar = [], []
    for hd in heads:
        q = q_ref[:, hd * LANES:(hd + 1) * LANES].astype(F32).T
        qt.append(jnp.concatenate([jnp.where(sub < DIFF_HEAD_DIM, q, 0.0),
                                   jnp.where(sub >= DIFF_HEAD_DIM, q, 0.0)], axis=1).astype(BF16))
        h = hb * hpb + hd
        cfar.append(jnp.where(col2 < tq, scal_ref[2 + 2 * h], scal_ref[3 + 2 * h]))

    def update(carry, s, vt, shift=None):
        m, l, acc = carry
        colmax = [jnp.max(x, axis=0, keepdims=True) for x in s]
        if shift is not None:
            colmax = [a + c for a, c in zip(colmax, shift)]
        m_new = [jnp.maximum(a, b_) for a, b_ in zip(m, colmax)]
        alpha = [jnp.exp2(a - b_) for a, b_ in zip(m, m_new)]
        sub_ = m_new if shift is None else [a - c for a, c in zip(m_new, shift)]
        p = [jnp.exp2(x - a) for x, a in zip(s, sub_)]
        l = [a * b_ + jnp.sum(x, axis=0, keepdims=True) for a, b_, x in zip(l, alpha, p)]
        acc = [a * b_ + jnp.dot(v_, x.astype(BF16), preferred_element_type=F32)
               for a, b_, x, v_ in zip(acc, alpha, p, vt)]
        return m_new, l, acc

    def scores(tile0, ntiles):
        rows = pl.ds(pl.multiple_of(tile0 * tq, tq), ntiles * tq)
        s = [jnp.dot(k_ref[rows, hd * LANES:(hd + 1) * LANES], qt[hd], preferred_element_type=F32) for hd in heads]
        vt = [jnp.concatenate([vt_ref[0, hd, tile0 + c] for c in range(ntiles)], axis=1) if ntiles > 1
              else vt_ref[0, hd, tile0] for hd in heads]
        return s, vt

    def pack(carry):
        m, l, acc = carry
        return tuple(m) + tuple(l) + tuple(acc)

    def unpack(flat):
        return list(flat[0:hpb]), list(flat[hpb:2 * hpb]), list(flat[2 * hpb:3 * hpb])

    def far_step(j, flat):
        s, vt = scores(2 * j, 2)
        return pack(update(unpack(flat), s, vt, cfar))

    def far_single(flat):
        s, vt = scores(i - 2, 1)
        return pack(update(unpack(flat), s, vt, cfar))

    n_far = jnp.maximum(i - 1, 0)
    init = ([jnp.full((1, w2), NEG, F32) for _ in heads], [jnp.zeros((1, w2), F32) for _ in heads],
            [jnp.zeros((LANES, w2), F32) for _ in heads])
    flat = lax.fori_loop(0, n_far // 2, far_step, pack(init))
    flat = lax.cond(n_far % 2 == 1, far_single, lambda c: c, flat)
    carry = unpack(flat)

    def near_bias(hd, lo):
        return jnp.concatenate([bias_ref[2 * hd, lo:lo + tq, :], bias_ref[2 * hd + 1, lo:lo + tq, :]], axis=1)

    s, vt = scores(jnp.maximum(i - 1, 0), 1)
    s = [jnp.where(i >= 1, x + near_bias(hd, 0), NEG) for hd, x in zip(heads, s)]
    carry = update(carry, s, vt)

    s, vt = scores(i, 1)
    kk = lax.broadcasted_iota(jnp.int32, (tq, w2), 0)
    qi = lax.broadcasted_iota(jnp.int32, (tq, w2), 1)
    qi = jnp.where(qi >= tq, qi - tq, qi)
    s = [jnp.where(kk <= qi, x + near_bias(hd, tq), NEG) for hd, x in zip(heads, s)]
    m, l, acc = update(carry, s, vt)

    outs = []
    for hd in heads:
        o = acc[hd] / l[hd]
        o = o[:, 0:tq] - lam * o[:, tq:w2]
        ms = jnp.mean(o * o, axis=0, keepdims=True)
        o = o * lax.rsqrt(ms + EPS) * subg_ref[...] * out_scale
        outs.append(o.T)
    o_ref[...] = jnp.concatenate(outs, axis=-1).astype(o_ref.dtype)


def diff_attention(qk, vt, bias_tiles, scal, subln_g, *, b, t, tq, hpb):
    m = b * t
    nq = t // tq
    nh = N_DIFF_HEADS
    wl = hpb * LANES
    assert nh % hpb == 0
    return pl.pallas_call(
        functools.partial(_diff_attn_kernel, tq=tq, hpb=hpb),
        out_shape=jax.ShapeDtypeStruct((m, nh * LANES), BF16),
        grid=(b, nh // hpb, nq),
        in_specs=[
            pl.BlockSpec(memory_space=pltpu.SMEM),
            pl.BlockSpec((tq, wl), lambda bi, h, i: (bi * nq + i, h)),
            pl.BlockSpec((t, wl), lambda bi, h, i: (bi, nh // hpb + h)),
            pl.BlockSpec((1, hpb, nq, LANES, tq), lambda bi, h, i: (bi, h, 0, 0, 0)),
            pl.BlockSpec((2 * hpb, 2 * tq, tq), lambda bi, h, i: (h, 0, 0)),
            pl.BlockSpec((LANES, 1), lambda bi, h, i: (0, 0)),
        ],
        out_specs=pl.BlockSpec((tq, wl), lambda bi, h, i: (bi * nq + i, h)),
        compiler_params=_cparams(("parallel", "parallel", "arbitrary")),
        name="diff_attention",
    )(scal, qk, qk, vt, bias_tiles, subln_g.reshape(LANES, 1).astype(F32))


def _sigmoid(x):
    return 1.0 / (1.0 + jnp.exp(-x))


def _softplus(x):
    return jnp.maximum(x, 0.0) + jnp.log(1.0 + jnp.exp(-jnp.abs(x)))


def _delta_kernel(hp_ref, xq_ref, xk_ref, xv_ref, pq_ref, pk_ref, pv_ref, wq_ref, wk_ref, wv_ref,
                  ab_ref, abt_ref, z_ref, og_ref, o_ref, s_ref, *, tt, nhb):
    hb = pl.program_id(1)
    i = pl.program_id(2)
    nc = tt // CHUNK
    dh = DELTA_HEAD_DIM
    heads = range(nhb)

    @pl.when(i == 0)
    def _():
        s_ref[...] = jnp.zeros_like(s_ref)

    def conv_silu(x_ref, p_ref, w_ref):
        prev = jnp.where(i > 0, p_ref[...].astype(F32), 0.0)
        xf = jnp.concatenate([prev, x_ref[...].astype(F32)], axis=0)
        w = w_ref[...]
        base = SUBLANES - (CONV_WIDTH - 1)
        y = xf[base:base + tt] * w[0:1]
        for c in range(1, CONV_WIDTH):
            y = y + xf[base + c:base + c + tt] * w[c:c + 1]
        return y * _sigmoid(y)

    def split(x):
        return [x[:, hd * dh:(hd + 1) * dh] for hd in heads]

    q = split(conv_silu(xq_ref, pq_ref, wq_ref))
    k = split(conv_silu(xk_ref, pk_ref, wk_ref))
    v = split(conv_silu(xv_ref, pv_ref, wv_ref))
    q = [x * lax.rsqrt(jnp.sum(x * x, axis=-1, keepdims=True) + EPS) * (dh ** -0.5) for x in q]
    k = [x * lax.rsqrt(jnp.sum(x * x, axis=-1, keepdims=True) + EPS) for x in k]

    ab = ab_ref[...]
    lane = lax.broadcasted_iota(jnp.int32, ab.shape, 1)
    beta_col, g_col, g_row = [], [], []
    for hd in heads:
        h = hb * nhb + hd
        neg_a = -jnp.exp(jnp.full((1, 1), hp_ref[0, h], F32))
        dt_bias = hp_ref[1, h]
        lb_col = jnp.sum(jnp.where(lane == h, ab, 0.0), axis=-1, keepdims=True)
        la_col = jnp.sum(jnp.where(lane == N_DELTA_HEADS + h, ab, 0.0), axis=-1, keepdims=True)
        la_row = abt_ref[pl.ds(N_DELTA_HEADS + h, 1), :]
        beta_col.append(_sigmoid(lb_col))
        g_col.append(neg_a * _softplus(la_col + dt_bias))
        g_row.append(neg_a * _softplus(la_row + dt_bias))

    ri = lax.broadcasted_iota(jnp.int32, (tt, tt), 0)
    ci = lax.broadcasted_iota(jnp.int32, (tt, tt), 1)
    same = (ri // CHUNK) == (ci // CHUNK)
    tril = same & (ri >= ci)
    strict = same & (ri > ci)
    triu = same & (ri <= ci)

    def mm(a, b):
        return jnp.dot(a.astype(BF16), b.astype(BF16), preferred_element_type=F32)

    def mm_nt(a, b):
        return lax.dot_general(a.astype(BF16), b.astype(BF16), (((1,), (1,)), ((), ())),
                               preferred_element_type=F32)

    def mm_tn(a, b):
        return lax.dot_general(a.astype(BF16), b.astype(BF16), (((0,), (0,)), ((), ())),
                               preferred_element_type=F32)

    gc_col = [jnp.sum(jnp.where(tril, g, 0.0), axis=-1, keepdims=True) for g in g_row]
    gc_row = [jnp.sum(jnp.where(triu, g, 0.0), axis=0, keepdims=True) for g in g_col]
    decay = [jnp.where(tril, jnp.exp(jnp.where(tril, a - b, 0.0)), 0.0) for a, b in zip(gc_col, gc_row)]
    e_gc = [jnp.exp(g) for g in gc_col]
    kk = [mm_nt(x, x) for x in k]
    qk = [mm_nt(a, b) for a, b in zip(q, k)]
    xm = [jnp.where(strict, -(b * kx * d), 0.0) for b, kx, d in zip(beta_col, kk, decay)]
    intra = [jnp.where(tril, a * d, 0.0) for a, d in zip(qk, decay)]
    r = [jnp.concatenate([b * vx, (b * e) * kx], axis=-1) for b, vx, e, kx in zip(beta_col, v, e_gc, k)]
    n_sq = CHUNK.bit_length() - 1
    for lvl in range(n_sq):
        r = [x + mm(m_, x) for m_, x in zip(xm, r)]
        if lvl + 1 < n_sq:
            xm = [mm(m_, m_) for m_ in xm]
    u = [x[:, 0:dh] for x in r]
    w = [x[:, dh:2 * dh] for x in r]
    qg = [a * e for a, e in zip(q, e_gc)]

    state = [s_ref[hd] for hd in heads]
    o_state = [[] for _ in heads]
    v_new = [[] for _ in heads]
    for c in range(nc):
        r0 = c * CHUNK
        sl = slice(r0, r0 + CHUNK)
        g_last = [g[r0 + CHUNK - 1:r0 + CHUNK] for g in gc_col]
        vn = [u[hd][sl] - mm(w[hd][sl], state[hd]) for hd in heads]
        os_ = [mm(qg[hd][sl], state[hd]) for hd in heads]
        k_dec = [k[hd][sl] * jnp.exp(g_last[hd] - gc_col[hd][sl]) for hd in heads]
        state = [state[hd] * jnp.exp(g_last[hd]) + mm_tn(k_dec[hd], vn[hd]) for hd in heads]
        for hd in heads:
            v_new[hd].append(vn[hd])
            o_state[hd].append(os_[hd])
    for hd in heads:
        s_ref[hd] = state[hd]

    outs = []
    for hd in heads:
        o = jnp.concatenate(o_state[hd], axis=0) + mm(intra[hd], jnp.concatenate(v_new[hd], axis=0))
        ms = jnp.mean(o * o, axis=-1, keepdims=True)
        outs.append(o * lax.rsqrt(ms + EPS) * og_ref[...])
    z = z_ref[...].astype(F32)
    o_ref[...] = (jnp.concatenate(outs, axis=-1) * (z * _sigmoid(z))).astype(o_ref.dtype)


def gated_deltanet(rest, ab, abt, conv_w, head_params, out_g, *, b, t, tt, nhb, qkv_col0, z_col0):
    m = b * t
    nt = t // tt
    nh = N_DELTA_HEADS
    sub = tt // SUBLANES
    wl = nhb * LANES
    assert nh % nhb == 0 and qkv_col0 % nhb == 0 and z_col0 % nhb == 0

    def cur(off):
        return pl.BlockSpec((tt, wl), lambda bi, h, i: (bi * nt + i, off // nhb + h))

    def prev(off):
        return pl.BlockSpec((SUBLANES, wl),
                            lambda bi, h, i: (jnp.maximum((bi * nt + i) * sub - 1, 0), off // nhb + h))

    def wspec(off):
        return pl.BlockSpec((CONV_WIDTH, wl), lambda bi, h, i: (0, off // nhb + h))

    return pl.pallas_call(
        functools.partial(_delta_kernel, tt=tt, nhb=nhb),
        out_shape=jax.ShapeDtypeStruct((m, nh * LANES), BF16),
        grid=(b, nh // nhb, nt),
        in_specs=[
            pl.BlockSpec(memory_space=pltpu.SMEM),
            cur(qkv_col0), cur(qkv_col0 + nh), cur(qkv_col0 + 2 * nh),
            prev(qkv_col0), prev(qkv_col0 + nh), prev(qkv_col0 + 2 * nh),
            wspec(0), wspec(nh), wspec(2 * nh),
            pl.BlockSpec((tt, LANES), lambda bi, h, i: (bi * nt + i, 0)),
            pl.BlockSpec((2 * nh, tt), lambda bi, h, i: (0, bi * nt + i)),
            cur(z_col0),
            pl.BlockSpec((1, LANES), lambda bi, h, i: (0, 0)),
        ],
        out_specs=pl.BlockSpec((tt, wl), lambda bi, h, i: (bi * nt + i, h)),
        scratch_shapes=[pltpu.VMEM((nhb, DELTA_HEAD_DIM, DELTA_HEAD_DIM), F32)],
        compiler_params=_cparams(("parallel", "parallel", "arbitrary")),
        name="gated_deltanet",
    )(head_params, rest, rest, rest, rest, rest, rest, conv_w, conv_w, conv_w, ab, abt, rest,
      out_g.reshape(1, LANES).astype(F32))


def _mem_attn_kernel(q_ref, k_ref, v_ref, o_ref):
    outs = []
    for h in range(N_MEM_HEADS):
        sl = slice(h * MEM_HEAD_DIM, (h + 1) * MEM_HEAD_DIM)
        s = lax.dot_general(q_ref[:, sl], k_ref[:, sl], (((1,), (1,)), ((), ())), preferred_element_type=F32)
        s = s - jnp.max(s, axis=-1, keepdims=True)
        p = jnp.exp(s)
        p = p / jnp.sum(p, axis=-1, keepdims=True)
        outs.append(jnp.dot(p.astype(BF16), v_ref[:, sl], preferred_element_type=F32))
    o_ref[...] = jnp.concatenate(outs, axis=-1).astype(o_ref.dtype)


def memory_attention(mq, mk, mv, *, b, t, mtok, tq):
    nq = t // tq
    w = N_MEM_HEADS * MEM_HEAD_DIM
    return pl.pallas_call(
        _mem_attn_kernel,
        out_shape=jax.ShapeDtypeStruct((b * t, w), BF16),
        grid=(b, nq),
        in_specs=[
            pl.BlockSpec((tq, w), lambda bi, i: (bi * nq + i, 0)),
            pl.BlockSpec((mtok, w), lambda bi, i: (bi, 0)),
            pl.BlockSpec((mtok, w), lambda bi, i: (bi, 0)),
        ],
        out_specs=pl.BlockSpec((tq, w), lambda bi, i: (bi * nq + i, 0)),
        compiler_params=_cparams(("parallel", "arbitrary")),
        name="memory_attention",
    )(mq, mk, mv)


def _merge_kernel(x_ref, ya_ref, yb_ref, yc_ref, ga_ref, gb_ref, gc_ref, bg_ref,
                  wa_ref, wb_ref, wc_ref, wo_ref, fg_ref, x1_ref, h2_ref):
    def branch(y_ref, g_ref, w_ref, idx):
        gate = _sigmoid(g_ref[...].astype(F32) + bg_ref[idx:idx + 1, :])
        return gate * jnp.dot(y_ref[...], w_ref[...], preferred_element_type=F32)

    merged = branch(ya_ref, ga_ref, wa_ref, 0) + branch(yb_ref, gb_ref, wb_ref, 1) + branch(yc_ref, gc_ref, wc_ref, 2)
    x1 = x_ref[...] + jnp.dot(merged.astype(BF16), wo_ref[...], preferred_element_type=F32)
    x1_ref[...] = x1
    ms = jnp.mean(x1 * x1, axis=-1, keepdims=True)
    h2_ref[...] = (x1 * lax.rsqrt(ms + EPS) * fg_ref[...]).astype(h2_ref.dtype)


def merge_project(x2d, y_diff, y_delta, y_mem, rest, b_gate, w_a, w_b, w_c, w_o, ffn_g, *, gate_col0, tm):
    m, d = x2d.shape
    row = lambda i: (i, 0)
    const = lambda i: (0, 0)
    assert gate_col0 % d == 0

    def gspec(j):
        return pl.BlockSpec((tm, d), lambda i: (i, gate_col0 // d + j))

    return pl.pallas_call(
        _merge_kernel,
        out_shape=(jax.ShapeDtypeStruct((m, d), F32), jax.ShapeDtypeStruct((m, d), BF16)),
        grid=(m // tm,),
        in_specs=[
            pl.BlockSpec((tm, d), row), pl.BlockSpec((tm, d), row), pl.BlockSpec((tm, d), row),
            pl.BlockSpec((tm, d), row), gspec(0), gspec(1), gspec(2),
            pl.BlockSpec((3, d), const),
            pl.BlockSpec((d, d), const), pl.BlockSpec((d, d), const), pl.BlockSpec((d, d), const),
            pl.BlockSpec((d, d), const), pl.BlockSpec((1, d), const),
        ],
        out_specs=(pl.BlockSpec((tm, d), row), pl.BlockSpec((tm, d), row)),
        compiler_params=_cparams(("parallel",)),
        name="merge_project",
    )(x2d, y_diff, y_delta, y_mem, rest, rest, rest, b_gate.reshape(3, d).astype(F32),
      w_a, w_b, w_c, w_o, ffn_g.reshape(1, d).astype(F32))


def _topk_rows(s, key, extra, k):
    big = jnp.iinfo(jnp.int32).max
    vals, keys, ext = [], [], []
    for _ in range(k):
        m = jnp.max(s, axis=0, keepdims=True)
        am = jnp.min(jnp.where(s == m, key, big), axis=0, keepdims=True)
        hit = key == am
        if extra is not None:
            ext.append(jnp.max(jnp.where(hit, extra, -1), axis=0, keepdims=True))
        s = jnp.where(hit, -jnp.inf, s)
        vals.append(m)
        keys.append(am)
    cat = lambda xs: jnp.concatenate(xs, axis=0)
    return cat(vals), cat(keys), (cat(ext) if extra is not None else None)


def _peer_route_kernel(q_ref, keys_ref, idx_ref, gate_ref, idx_acc, gate_acc):
    h = pl.program_id(1)
    k = PEER_TOPK
    q = q_ref[...]
    tt = q.shape[0]

    def half(p):
        qp = q[:, p * PEER_HALF:(p + 1) * PEER_HALF]
        st = lax.dot_general(keys_ref[0, p], qp, (((1,), (1,)), ((), ())), preferred_element_type=F32)
        v, r, _ = _topk_rows(st, lax.broadcasted_iota(jnp.int32, st.shape, 0), None, k)
        return v, r

    s1, i1 = half(0)
    s2, i2 = half(1)
    g = SUBLANES
    sub = lax.broadcasted_iota(jnp.int32, (g, tt), 0)
    e1 = i1 * PEER_KEYS
    pieces = []
    for b in range(g):
        lim = min(g, k // (b + 1))
        sc = s1[0:g] + s2[b:b + 1]
        if lim < g:
            sc = jnp.where(sub < lim, sc, -jnp.inf)
        pieces.append((sc, sub * k + b, e1[0:g] + i2[b:b + 1]))
    pieces.append((s1[g:k] + s2[0:1], (sub + g) * k, e1[g:k] + i2[0:1]))
    pieces.append((s1[0:1] + s2[g:k], sub + g, e1[0:1] + i2[g:k]))
    cand_s = jnp.concatenate([p[0] for p in pieces], axis=0)
    cand_k = jnp.concatenate([p[1] for p in pieces], axis=0)
    cand_e = jnp.concatenate([p[2] for p in pieces], axis=0)
    top_s, _, top_e = _topk_rows(cand_s, cand_k, cand_e, k)
    e = jnp.exp(top_s - top_s[0:1])
    rows = pl.ds(pl.multiple_of(h * k, k), k)
    gate_acc[rows, :] = e / jnp.sum(e, axis=0, keepdims=True)
    idx_acc[rows, :] = top_e.astype(F32)

    @pl.when(h == pl.num_programs(1) - 1)
    def _():
        gate_ref[...] = gate_acc[...].T
        idx_ref[...] = idx_acc[...].T.astype(jnp.int32)


def peer_route(qry, sub_keys, *, tt):
    m = qry.shape[0]
    nh = PEER_HEADS
    ne = nh * PEER_TOPK
    return pl.pallas_call(
        _peer_route_kernel,
        out_shape=(jax.ShapeDtypeStruct((m, ne), jnp.int32), jax.ShapeDtypeStruct((m, ne), F32)),
        grid=(m // tt, nh),
        in_specs=[
            pl.BlockSpec((tt, 2 * PEER_HALF), lambda i, h: (i, h)),
            pl.BlockSpec((1, 2, PEER_KEYS, PEER_HALF), lambda i, h: (h, 0, 0, 0)),
        ],
        out_specs=(pl.BlockSpec((tt, ne), lambda i, h: (i, 0)), pl.BlockSpec((tt, ne), lambda i, h: (i, 0))),
        scratch_shapes=[pltpu.VMEM((ne, tt), F32), pltpu.VMEM((ne, tt), F32)],
        compiler_params=_cparams(("parallel", "arbitrary")),
        name="peer_route",
    )(qry, sub_keys)


def _gelu_exact(x):
    return 0.5 * x * (1.0 + lax.erf(x * (2.0 ** -0.5)))


def _peer_expert_kernel(idx_ref, idxn_ref, gate_ref, h_ref, x_ref, pool_ref, poolt_ref, uv_hbm,
                        o_ref, *scratch, tb, ne, nbuf, dist):
    bufs, sem = scratch[:nbuf], scratch[nbuf]
    i = pl.program_id(0)
    n = pl.num_programs(0)
    rows = tb * ne
    d_sub = h_ref.shape[1] // LANES
    n_seg = 2 * tb
    seg = rows // n_seg

    def issue_rows(iref, row0, buf, which, r_lo, r_hi):
        for r in range(r_lo, r_hi):
            e = iref[row0 + r // ne, r % ne]
            pltpu.make_async_copy(uv_hbm.at[e], buf.at[r], sem.at[which, r % 2]).start(priority=r % 2)

    def wait_tile(buf, which):
        for k in range(2):
            pltpu.make_async_copy(uv_hbm.at[pl.ds(0, rows // 2)], buf.at[pl.ds(0, rows // 2)],
                                  sem.at[which, k]).wait()

    @pl.when(i == 0)
    def _():
        for a in range(dist):
            def tok(t, carry, a=a):
                for j in range(ne):
                    pltpu.make_async_copy(uv_hbm.at[idx_ref[a * tb + t, j]], bufs[a].at[t * ne + j],
                                          sem.at[a, j % 2]).start(priority=j % 2)
                return carry
            lax.fori_loop(0, tb, tok, 0)

    blk = 2 * d_sub
    sub_id = lax.broadcasted_iota(jnp.int32, (d_sub, ne * blk), 0)
    col_id = lax.broadcasted_iota(jnp.int32, (d_sub, ne * blk), 1)
    diag_u = (col_id % blk) == sub_id
    diag_v = (col_id % blk) == sub_id + d_sub

    def tile(row0, buf, nxt_iref, nxt_row0, nxt_buf, nxt_which):
        drows = []
        h8 = h_ref[pl.ds(row0, tb), :].astype(F32).reshape(tb, d_sub, LANES).astype(BF16)
        for t in range(tb):
            z = buf[pl.ds(t * ne, ne)].reshape(ne * blk, LANES)
            c = lax.dot_general(h8[t], z, (((1,), (1,)), ((), ())), preferred_element_type=F32)
            drows.append(jnp.sum(jnp.where(diag_u, c, 0.0), axis=0, keepdims=True))
            issue_rows(nxt_iref, nxt_row0, nxt_buf, nxt_which, t * seg, (t + 1) * seg)
        dall = jnp.concatenate(drows, axis=0)
        d_hi = dall.astype(BF16)
        d_lo = (dall - d_hi.astype(F32)).astype(BF16)
        a = (jnp.dot(d_hi, pool_ref[...], preferred_element_type=F32)
             + jnp.dot(d_lo, pool_ref[...], preferred_element_type=F32))
        w = gate_ref[pl.ds(row0, tb), :] * _gelu_exact(a)
        wrep = jnp.dot(w.astype(BF16), poolt_ref[...], preferred_element_type=F32)
        outs = []
        for t in range(tb):
            wexp = jnp.where(diag_v, wrep[t:t + 1, :], 0.0).astype(BF16)
            z = buf[pl.ds(t * ne, ne)].reshape(ne * blk, LANES)
            outs.append(jnp.dot(wexp, z, preferred_element_type=F32))
            issue_rows(nxt_iref, nxt_row0, nxt_buf, nxt_which, (tb + t) * seg, (tb + t + 1) * seg)
        o_ref[pl.ds(row0, tb), :] = x_ref[pl.ds(row0, tb), :] + jnp.stack(outs).reshape(tb, d_sub * LANES)

    for a in range(nbuf):
        wait_tile(bufs[a], a)
        nxt = a + dist
        if nxt < nbuf:
            tile(a * tb, bufs[a], idx_ref, nxt * tb, bufs[nxt], nxt)
        else:
            tile(a * tb, bufs[a], idxn_ref, (nxt - nbuf) * tb, bufs[nxt - nbuf], nxt - nbuf)

    @pl.when(i == n - 1)
    def _():
        for a in range(dist):
            wait_tile(bufs[a], a)


def peer_experts(idx, gates, h2d, x2d, uv3, *, tb):
    m, ne = idx.shape
    d_sub = h2d.shape[1] // LANES
    nbuf, dist = 4, 2
    ts = nbuf * tb
    n = m // ts
    rows = tb * ne
    blk = 2 * d_sub
    cid = jnp.arange(ne * blk) // blk
    pool = (cid[:, None] == jnp.arange(ne)[None, :]).astype(BF16)
    tok = lambda i: (i, 0)
    return pl.pallas_call(
        functools.partial(_peer_expert_kernel, tb=tb, ne=ne, nbuf=nbuf, dist=dist),
        out_shape=jax.ShapeDtypeStruct(x2d.shape, F32),
        grid=(n,),
        in_specs=[
            pl.BlockSpec((ts, ne), tok, memory_space=pltpu.SMEM),
            pl.BlockSpec((ts, ne), lambda i: (jnp.minimum(i + 1, n - 1), 0), memory_space=pltpu.SMEM),
            pl.BlockSpec((ts, ne), tok),
            pl.BlockSpec((ts, d_sub * LANES), tok),
            pl.BlockSpec((ts, d_sub * LANES), tok),
            pl.BlockSpec((ne * blk, ne), lambda i: (0, 0)),
            pl.BlockSpec((ne, ne * blk), lambda i: (0, 0)),
            pl.BlockSpec(memory_space=pl.ANY),
        ],
        out_specs=pl.BlockSpec((ts, d_sub * LANES), tok),
        scratch_shapes=[pltpu.VMEM((rows, blk, LANES), BF16) for _ in range(nbuf)]
        + [pltpu.SemaphoreType.DMA((nbuf, 2))],
        compiler_params=_cparams(("arbitrary",)),
        name="peer_experts",
    )(idx, idx, gates, h2d, x2d, pool, pool.T, uv3)


def kernel(x, mem, positions, attn_norm_g, mem_norm_g, w_in, b_gate, diff_q_norm_g, diff_k_norm_g, lambda_q1, lambda_k1, lambda_q2, lambda_k2, diff_subln_g, rel_bias_table, conv_w, a_log, dt_bias, delta_out_norm_g, w_mem_kv, mem_q_norm_g, mem_k_norm_g, w_br_diff, w_br_delta, w_br_mem, w_out, ffn_norm_g, w_query, sub_keys, expert_u, expert_v):
    del positions
    b, t, d = x.shape
    mtok = mem.shape[1]
    m = b * t
    depth = w_in.shape[0]
    nh = N_DIFF_HEADS
    qk_w = nh * 2 * DIFF_HEAD_DIM
    dv_w = nh * 2 * DIFF_HEAD_DIM
    dl_w = N_DELTA_HEADS * DELTA_HEAD_DIM
    mem_w = N_MEM_HEADS * MEM_HEAD_DIM
    tq = min(256, t)
    tt_delta = min(256, t)

    x2d = x.reshape(m, d)
    for l in range(depth):
        lam_init = 0.8 - 0.6 * math.exp(-0.3 * l)
        wl = w_in[l]
        o = 0
        w_qk = wl[:, o:o + 2 * qk_w]; o += 2 * qk_w
        w_dv = wl[:, o:o + dv_w]; o += dv_w
        w_lqkv = wl[:, o:o + 3 * dl_w]; o += 3 * dl_w
        w_lz = wl[:, o:o + dl_w]; o += dl_w
        w_ab = wl[:, o:o + 2 * N_DELTA_HEADS]; o += 2 * N_DELTA_HEADS
        w_mq = wl[:, o:o + mem_w]; o += mem_w
        w_gate = wl[:, o:o + 3 * d]

        qk_gain = jnp.concatenate([jnp.tile(diff_q_norm_g[l], 2 * nh) * (DIFF_HEAD_DIM ** -0.5 * LOG2E),
                                   jnp.tile(diff_k_norm_g[l], 2 * nh)])
        qk = norm_matmul(x2d, w_qk.astype(BF16), norm_g=attn_norm_g[l], group=DIFF_HEAD_DIM, group_gain=qk_gain)
        w_rest = jnp.concatenate([w_dv, w_lqkv, w_lz, w_gate], axis=1).astype(BF16)
        rest = norm_matmul(x2d, w_rest, norm_g=attn_norm_g[l])
        mq_gain = jnp.tile(mem_q_norm_g[l], N_MEM_HEADS) * (MEM_HEAD_DIM ** -0.5)
        mq = norm_matmul(x2d, w_mq.astype(BF16), norm_g=attn_norm_g[l], group=MEM_HEAD_DIM, group_gain=mq_gain)
        w_ab_pad = jnp.pad(w_ab, ((0, 0), (0, LANES - 2 * N_DELTA_HEADS))).astype(BF16)
        ab = norm_matmul(x2d, w_ab_pad, norm_g=attn_norm_g[l], out_dtype=F32, tn=LANES)
        abt = ab[:, :2 * N_DELTA_HEADS].T

        lam = (jnp.exp(jnp.sum(lambda_q1[l].astype(F32) * lambda_k1[l].astype(F32)))
               - jnp.exp(jnp.sum(lambda_q2[l].astype(F32) * lambda_k2[l].astype(F32))) + lam_init)
        bias_tiles = rel_bias_tiles(rel_bias_table, tq)
        scal = jnp.concatenate([jnp.stack([lam, jnp.asarray(1.0 - lam_init, F32)]),
                                rel_bias_table[REL_BUCKETS - 1].astype(F32) * LOG2E])
        vt = rest[:, :dv_w].reshape(b, t // tq, tq, nh, LANES).transpose(0, 3, 1, 4, 2)
        y_diff = diff_attention(qk, vt, bias_tiles, scal, diff_subln_g[l], b=b, t=t, tq=tq, hpb=4)

        head_params = jnp.stack([a_log[l], dt_bias[l]]).astype(F32)
        y_delta = gated_deltanet(rest, ab, abt, conv_w[l].astype(F32), head_params, delta_out_norm_g[l],
                                 b=b, t=t, tt=tt_delta, nhb=8, qkv_col0=dv_w // LANES,
                                 z_col0=(dv_w + 3 * dl_w) // LANES)

        mem2d = mem.reshape(b * mtok, d)
        wkv = w_mem_kv[l].astype(BF16)
        mk = norm_matmul(mem2d, wkv[:, :mem_w], norm_g=mem_norm_g[l], group=MEM_HEAD_DIM,
                         group_gain=jnp.tile(mem_k_norm_g[l], N_MEM_HEADS))
        mv = norm_matmul(mem2d, wkv[:, mem_w:], norm_g=mem_norm_g[l])
        y_mem = memory_attention(mq, mk, mv, b=b, t=t, mtok=mtok, tq=min(512, t))

        x1, h2 = merge_project(x2d, y_diff, y_delta, y_mem, rest, b_gate[l],
                               w_br_diff[l].astype(BF16), w_br_delta[l].astype(BF16), w_br_mem[l].astype(BF16),
                               w_out[l].astype(BF16), ffn_norm_g[l],
                               gate_col0=dv_w + 4 * dl_w, tm=min(512, m))

        qry = norm_matmul(h2, w_query[l].astype(BF16))
        idx, gates = peer_route(qry, sub_keys[l].astype(BF16), tt=min(512, m))
        sub = d // LANES
        uv3 = jnp.concatenate([expert_u[l].reshape(-1, sub, LANES), expert_v[l].reshape(-1, sub, LANES)],
                              axis=1).astype(BF16)
        x2d = peer_experts(idx, gates, h2, x1, uv3, tb=8)
    return x2d.reshape(b, t, d)
```

```python
import functools
import math

import jax
import jax.numpy as jnp
from jax import lax
from jax.experimental import pallas as pl
from jax.experimental.pallas import tpu as pltpu

F32 = jnp.float32
BF16 = jnp.bfloat16
EPS = 1e-6
NEG = -1e30
LOG2E = math.log2(math.e)

N_DIFF_HEADS = 8
DIFF_HEAD_DIM = 64
N_DELTA_HEADS = 8
DELTA_HEAD_DIM = 128
CONV_WIDTH = 4
CHUNK = 64
N_MEM_HEADS = 4
MEM_HEAD_DIM = 256
REL_BUCKETS = 32
REL_MAX_DIST = 128
PEER_HEADS = 8
PEER_KEYS = 128
PEER_TOPK = 16
PEER_HALF = 128
LANES = 128
SUBLANES = 8
VMEM_LIMIT = 56 * 1024 * 1024


def _cparams(sem):
    return pltpu.CompilerParams(dimension_semantics=sem, vmem_limit_bytes=VMEM_LIMIT)


def _norm_matmul_kernel(*refs, has_norm, has_group):
    it = iter(refs)
    x_ref = next(it)
    g_ref = next(it) if has_norm else None
    w_ref = next(it)
    gm_ref = next(it) if has_group else None
    gain_ref = next(it) if has_group else None
    o_ref = next(it)
    h_ref = next(it)

    @pl.when(pl.program_id(1) == 0)
    def _():
        x = x_ref[...].astype(F32)
        if has_norm:
            ms = jnp.mean(x * x, axis=-1, keepdims=True)
            x = x * lax.rsqrt(ms + EPS) * g_ref[...]
        h_ref[...] = x.astype(BF16)

    y = jnp.dot(h_ref[...], w_ref[...], preferred_element_type=F32)
    if has_group:
        ms = jnp.dot((y * y).astype(BF16), gm_ref[...], preferred_element_type=F32)
        y = y * lax.rsqrt(ms + EPS) * gain_ref[...]
    o_ref[...] = y.astype(o_ref.dtype)


def norm_matmul(x, w, *, norm_g=None, group=None, group_gain=None, out_dtype=BF16, tm=1024, tn=None):
    m, k = x.shape
    n = w.shape[1]
    has_norm = norm_g is not None
    has_group = group is not None
    if tn is None:
        tn = 512 if has_group else min(n, 2048)
    tm = min(tm, m)
    assert m % tm == 0 and n % tn == 0
    in_specs = [pl.BlockSpec((tm, k), lambda i, j: (i, 0))]
    args = [x]
    if has_norm:
        in_specs.append(pl.BlockSpec((1, k), lambda i, j: (0, 0)))
        args.append(norm_g.reshape(1, k).astype(F32))
    in_specs.append(pl.BlockSpec((k, tn), lambda i, j: (0, j)))
    args.append(w)
    if has_group:
        gid = jnp.arange(tn) // group
        gm = jnp.where(gid[:, None] == gid[None, :], 1.0 / group, 0.0).astype(BF16)
        in_specs.append(pl.BlockSpec((tn, tn), lambda i, j: (0, 0)))
        args.append(gm)
        in_specs.append(pl.BlockSpec((1, tn), lambda i, j: (0, j)))
        args.append(group_gain.reshape(1, n).astype(F32))
    return pl.pallas_call(
        functools.partial(_norm_matmul_kernel, has_norm=has_norm, has_group=has_group),
        out_shape=jax.ShapeDtypeStruct((m, n), out_dtype),
        grid=(m // tm, n // tn),
        in_specs=in_specs,
        out_specs=pl.BlockSpec((tm, tn), lambda i, j: (i, j)),
        scratch_shapes=[pltpu.VMEM((tm, k), BF16)],
        compiler_params=_cparams(("parallel", "arbitrary")),
        name="norm_matmul",
    )(*args)


def _t5_bucket(n):
    max_exact = REL_BUCKETS // 2
    nf = jnp.maximum(n, 1).astype(F32)
    large = max_exact + (jnp.log(nf / max_exact) / math.log(REL_MAX_DIST / max_exact)
                         * (REL_BUCKETS - max_exact)).astype(jnp.int32)
    large = jnp.minimum(large, REL_BUCKETS - 1)
    return jnp.where(n < max_exact, n, large)


def _rel_bias_kernel(table_ref, o_ref, *, tq):
    hm = pl.program_id(0)
    c = lax.broadcasted_iota(jnp.int32, (2 * tq, tq), 0)
    r = lax.broadcasted_iota(jnp.int32, (2 * tq, tq), 1)
    bucket = _t5_bucket(jnp.maximum(r - c + tq, 0))
    acc = jnp.zeros((2 * tq, tq), F32)
    for b in range(REL_BUCKETS):
        acc = jnp.where(bucket == b, table_ref[b, hm], acc)
    o_ref[0] = acc * LOG2E


def rel_bias_tiles(rel_table, tq):
    nmaps = rel_table.shape[1]
    return pl.pallas_call(
        functools.partial(_rel_bias_kernel, tq=tq),
        out_shape=jax.ShapeDtypeStruct((nmaps, 2 * tq, tq), F32),
        grid=(nmaps,),
        in_specs=[pl.BlockSpec(memory_space=pltpu.SMEM)],
        out_specs=pl.BlockSpec((1, 2 * tq, tq), lambda i: (i, 0, 0)),
        compiler_params=_cparams(("arbitrary",)),
        name="rel_bias_tiles",
    )(rel_table.astype(F32))


def _diff_attn_kernel(scal_ref, q_ref, k_ref, vt_ref, bias_ref, subg_ref, o_ref, *, tq, hpb):
    hb = pl.program_id(1)
    i = pl.program_id(2)
    heads = range(hpb)
    lam = scal_ref[0]
    out_scale = scal_ref[1]
    w2 = 2 * tq

    sub = lax.broadcasted_iota(jnp.int32, (LANES, tq), 0)
    col2 = lax.broadcasted_iota(jnp.int32, (1, w2), 1)
    qt, cfar = [], []
    for hd in heads:
        q = q_ref[:, hd * LANES:(hd + 1) * LANES].astype(F32).T
        qt.append(jnp.concatenate([jnp.where(sub < DIFF_HEAD_DIM, q, 0.0),
                                   jnp.where(sub >= DIFF_HEAD_DIM, q, 0.0)], axis=1).astype(BF16))
        h = hb * hpb + hd
        cfar.append(jnp.where(col2 < tq, scal_ref[2 + 2 * h], scal_ref[3 + 2 * h]))

    def update(carry, s, vt, shift=None):
        m, l, acc = carry
        colmax = [jnp.max(x, axis=0, keepdims=True) for x in s]
        if shift is not None:
            colmax = [a + c for a, c in zip(colmax, shift)]
        m_new = [jnp.maximum(a, b_) for a, b_ in zip(m, colmax)]
        alpha = [jnp.exp2(a - b_) for a, b_ in zip(m, m_new)]
        sub_ = m_new if shift is None else [a - c for a, c in zip(m_new, shift)]
        p = [jnp.exp2(x - a) for x, a in zip(s, sub_)]
        l = [a * b_ + jnp.sum(x, axis=0, keepdims=True) for a, b_, x in zip(l, alpha, p)]
        acc = [a * b_ + jnp.dot(v_, x.astype(BF16), preferred_element_type=F32)
               for a, b_, x, v_ in zip(acc, alpha, p, vt)]
        return m_new, l, acc

    def scores(tile0, ntiles):
        rows = pl.ds(pl.multiple_of(tile0 * tq, tq), ntiles * tq)
        s = [jnp.dot(k_ref[rows, hd * LANES:(hd + 1) * LANES], qt[hd], preferred_element_type=F32) for hd in heads]
        vt = [jnp.concatenate([vt_ref[0, hd, tile0 + c] for c in range(ntiles)], axis=1) if ntiles > 1
              else vt_ref[0, hd, tile0] for hd in heads]
        return s, vt

    def pack(carry):
        m, l, acc = carry
        return tuple(m) + tuple(l) + tuple(acc)

    def unpack(flat):
        return list(flat[0:hpb]), list(flat[hpb:2 * hpb]), list(flat[2 * hpb:3 * hpb])

    def far_step(j, flat):
        s, vt = scores(2 * j, 2)
        return pack(update(unpack(flat), s, vt, cfar))

    def far_single(flat):
        s, vt = scores(i - 2, 1)
        return pack(update(unpack(flat), s, vt, cfar))

    n_far = jnp.maximum(i - 1, 0)
    init = ([jnp.full((1, w2), NEG, F32) for _ in heads], [jnp.zeros((1, w2), F32) for _ in heads],
            [jnp.zeros((LANES, w2), F32) for _ in heads])
    flat = lax.fori_loop(0, n_far // 2, far_step, pack(init))
    flat = lax.cond(n_far % 2 == 1, far_single, lambda c: c, flat)
    carry = unpack(flat)

    def near_bias(hd, lo):
        return jnp.concatenate([bias_ref[2 * hd, lo:lo + tq, :], bias_ref[2 * hd + 1, lo:lo + tq, :]], axis=1)

    s, vt = scores(jnp.maximum(i - 1, 0), 1)
    s = [jnp.where(i >= 1, x + near_bias(hd, 0), NEG) for hd, x in zip(heads, s)]
    carry = update(carry, s, vt)

    s, vt = scores(i, 1)
    kk = lax.broadcasted_iota(jnp.int32, (tq, w2), 0)
    qi = lax.broadcasted_iota(jnp.int32, (tq, w2), 1)
    qi = jnp.where(qi >= tq, qi - tq, qi)
    s = [jnp.where(kk <= qi, x + near_bias(hd, tq), NEG) for hd, x in zip(heads, s)]
    m, l, acc = update(carry, s, vt)

    outs = []
    for hd in heads:
        o = acc[hd] / l[hd]
        o = o[:, 0:tq] - lam * o[:, tq:w2]
        ms = jnp.mean(o * o, axis=0, keepdims=True)
        o = o * lax.rsqrt(ms + EPS) * subg_ref[...] * out_scale
        outs.append(o.T)
    o_ref[...] = jnp.concatenate(outs, axis=-1).astype(o_ref.dtype)


def diff_attention(qk, vt, bias_tiles, scal, subln_g, *, b, t, tq, hpb):
    m = b * t
    nq = t // tq
    nh = N_DIFF_HEADS
    wl = hpb * LANES
    assert nh % hpb == 0
    return pl.pallas_call(
        functools.partial(_diff_attn_kernel, tq=tq, hpb=hpb),
        out_shape=jax.ShapeDtypeStruct((m, nh * LANES), BF16),
        grid=(b, nh // hpb, nq),
        in_specs=[
            pl.BlockSpec(memory_space=pltpu.SMEM),
            pl.BlockSpec((tq, wl), lambda bi, h, i: (bi * nq + i, h)),
            pl.BlockSpec((t, wl), lambda bi, h, i: (bi, nh // hpb + h)),
            pl.BlockSpec((1, hpb, nq, LANES, tq), lambda bi, h, i: (bi, h, 0, 0, 0)),
            pl.BlockSpec((2 * hpb, 2 * tq, tq), lambda bi, h, i: (h, 0, 0)),
            pl.BlockSpec((LANES, 1), lambda bi, h, i: (0, 0)),
        ],
        out_specs=pl.BlockSpec((tq, wl), lambda bi, h, i: (bi * nq + i, h)),
        compiler_params=_cparams(("parallel", "parallel", "arbitrary")),
        name="diff_attention",
    )(scal, qk, qk, vt, bias_tiles, subln_g.reshape(LANES, 1).astype(F32))


def _sigmoid(x):
    return 1.0 / (1.0 + jnp.exp(-x))


def _softplus(x):
    return jnp.maximum(x, 0.0) + jnp.log(1.0 + jnp.exp(-jnp.abs(x)))


def _delta_kernel(hp_ref, xq_ref, xk_ref, xv_ref, pq_ref, pk_ref, pv_ref, wq_ref, wk_ref, wv_ref,
                  ab_ref, abt_ref, z_ref, og_ref, o_ref, s_ref, *, tt, nhb):
    hb = pl.program_id(1)
    i = pl.program_id(2)
    nc = tt // CHUNK
    dh = DELTA_HEAD_DIM
    heads = range(nhb)

    @pl.when(i == 0)
    def _():
        s_ref[...] = jnp.zeros_like(s_ref)

    def conv_silu(x_ref, p_ref, w_ref):
        prev = jnp.where(i > 0, p_ref[...].astype(F32), 0.0)
        xf = jnp.concatenate([prev, x_ref[...].astype(F32)], axis=0)
        w = w_ref[...]
        base = SUBLANES - (CONV_WIDTH - 1)
        y = xf[base:base + tt] * w[0:1]
        for c in range(1, CONV_WIDTH):
            y = y + xf[base + c:base + c + tt] * w[c:c + 1]
        return y * _sigmoid(y)

    def split(x):
        return [x[:, hd * dh:(hd + 1) * dh] for hd in heads]

    q = split(conv_silu(xq_ref, pq_ref, wq_ref))
    k = split(conv_silu(xk_ref, pk_ref, wk_ref))
    v = split(conv_silu(xv_ref, pv_ref, wv_ref))
    q = [x * lax.rsqrt(jnp.sum(x * x, axis=-1, keepdims=True) + EPS) * (dh ** -0.5) for x in q]
    k = [x * lax.rsqrt(jnp.sum(x * x, axis=-1, keepdims=True) + EPS) for x in k]

    ab = ab_ref[...]
    lane = lax.broadcasted_iota(jnp.int32, ab.shape, 1)
    beta_col, g_col, g_row = [], [], []
    for hd in heads:
        h = hb * nhb + hd
        neg_a = -jnp.exp(jnp.full((1, 1), hp_ref[0, h], F32))
        dt_bias = hp_ref[1, h]
        lb_col = jnp.sum(jnp.where(lane == h, ab, 0.0), axis=-1, keepdims=True)
        la_col = jnp.sum(jnp.where(lane == N_DELTA_HEADS + h, ab, 0.0), axis=-1, keepdims=True)
        la_row = abt_ref[pl.ds(N_DELTA_HEADS + h, 1), :]
        beta_col.append(_sigmoid(lb_col))
        g_col.append(neg_a * _softplus(la_col + dt_bias))
        g_row.append(neg_a * _softplus(la_row + dt_bias))

    ri = lax.broadcasted_iota(jnp.int32, (tt, tt), 0)
    ci = lax.broadcasted_iota(jnp.int32, (tt, tt), 1)
    same = (ri // CHUNK) == (ci // CHUNK)
    tril = same & (ri >= ci)
    strict = same & (ri > ci)
    triu = same & (ri <= ci)

    def mm(a, b):
        return jnp.dot(a.astype(BF16), b.astype(BF16), preferred_element_type=F32)

    def mm_nt(a, b):
        return lax.dot_general(a.astype(BF16), b.astype(BF16), (((1,), (1,)), ((), ())),
                               preferred_element_type=F32)

    def mm_tn(a, b):
        return lax.dot_general(a.astype(BF16), b.astype(BF16), (((0,), (0,)), ((), ())),
                               preferred_element_type=F32)

    gc_col = [jnp.sum(jnp.where(tril, g, 0.0), axis=-1, keepdims=True) for g in g_row]
    gc_row = [jnp.sum(jnp.where(triu, g, 0.0), axis=0, keepdims=True) for g in g_col]
    decay = [jnp.where(tril, jnp.exp(jnp.where(tril, a - b, 0.0)), 0.0) for a, b in zip(gc_col, gc_row)]
    e_gc = [jnp.exp(g) for g in gc_col]
    kk = [mm_nt(x, x) for x in k]
    qk = [mm_nt(a, b) for a, b in zip(q, k)]
    xm = [jnp.where(strict, -(b * kx * d), 0.0) for b, kx, d in zip(beta_col, kk, decay)]
    intra = [jnp.where(tril, a * d, 0.0) for a, d in zip(qk, decay)]
    r = [jnp.concatenate([b * vx, (b * e) * kx], axis=-1) for b, vx, e, kx in zip(beta_col, v, e_gc, k)]
    n_sq = CHUNK.bit_length() - 1
    for lvl in range(n_sq):
        r = [x + mm(m_, x) for m_, x in zip(xm, r)]
        if lvl + 1 < n_sq:
            xm = [mm(m_, m_) for m_ in xm]
    u = [x[:, 0:dh] for x in r]
    w = [x[:, dh:2 * dh] for x in r]
    qg = [a * e for a, e in zip(q, e_gc)]

    state = [s_ref[hd] for hd in heads]
    o_state = [[] for _ in heads]
    v_new = [[] for _ in heads]
    for c in range(nc):
        r0 = c * CHUNK
        sl = slice(r0, r0 + CHUNK)
        g_last = [g[r0 + CHUNK - 1:r0 + CHUNK] for g in gc_col]
        vn = [u[hd][sl] - mm(w[hd][sl], state[hd]) for hd in heads]
        os_ = [mm(qg[hd][sl], state[hd]) for hd in heads]
        k_dec = [k[hd][sl] * jnp.exp(g_last[hd] - gc_col[hd][sl]) for hd in heads]
        state = [state[hd] * jnp.exp(g_last[hd]) + mm_tn(k_dec[hd], vn[hd]) for hd in heads]
        for hd in heads:
            v_new[hd].append(vn[hd])
            o_state[hd].append(os_[hd])
    for hd in heads:
        s_ref[hd] = state[hd]

    outs = []
    for hd in heads:
        o = jnp.concatenate(o_state[hd], axis=0) + mm(intra[hd], jnp.concatenate(v_new[hd], axis=0))
        ms = jnp.mean(o * o, axis=-1, keepdims=True)
        outs.append(o * lax.rsqrt(ms + EPS) * og_ref[...])
    z = z_ref[...].astype(F32)
    o_ref[...] = (jnp.concatenate(outs, axis=-1) * (z * _sigmoid(z))).astype(o_ref.dtype)


def gated_deltanet(rest, ab, abt, conv_w, head_params, out_g, *, b, t, tt, nhb, qkv_col0, z_col0):
    m = b * t
    nt = t // tt
    nh = N_DELTA_HEADS
    sub = tt // SUBLANES
    wl = nhb * LANES
    assert nh % nhb == 0 and qkv_col0 % nhb == 0 and z_col0 % nhb == 0

    def cur(off):
        return pl.BlockSpec((tt, wl), lambda bi, h, i: (bi * nt + i, off // nhb + h))

    def prev(off):
        return pl.BlockSpec((SUBLANES, wl),
                            lambda bi, h, i: (jnp.maximum((bi * nt + i) * sub - 1, 0), off // nhb + h))

    def wspec(off):
        return pl.BlockSpec((CONV_WIDTH, wl), lambda bi, h, i: (0, off // nhb + h))

    return pl.pallas_call(
        functools.partial(_delta_kernel, tt=tt, nhb=nhb),
        out_shape=jax.ShapeDtypeStruct((m, nh * LANES), BF16),
        grid=(b, nh // nhb, nt),
        in_specs=[
            pl.BlockSpec(memory_space=pltpu.SMEM),
            cur(qkv_col0), cur(qkv_col0 + nh), cur(qkv_col0 + 2 * nh),
            prev(qkv_col0), prev(qkv_col0 + nh), prev(qkv_col0 + 2 * nh),
            wspec(0), wspec(nh), wspec(2 * nh),
            pl.BlockSpec((tt, LANES), lambda bi, h, i: (bi * nt + i, 0)),
            pl.BlockSpec((2 * nh, tt), lambda bi, h, i: (0, bi * nt + i)),
            cur(z_col0),
            pl.BlockSpec((1, LANES), lambda bi, h, i: (0, 0)),
        ],
        out_specs=pl.BlockSpec((tt, wl), lambda bi, h, i: (bi * nt + i, h)),
        scratch_shapes=[pltpu.VMEM((nhb, DELTA_HEAD_DIM, DELTA_HEAD_DIM), F32)],
        compiler_params=_cparams(("parallel", "parallel", "arbitrary")),
        name="gated_deltanet",
    )(head_params, rest, rest, rest, rest, rest, rest, conv_w, conv_w, conv_w, ab, abt, rest,
      out_g.reshape(1, LANES).astype(F32))


def _mem_attn_kernel(q_ref, k_ref, v_ref, o_ref):
    outs = []
    for h in range(N_MEM_HEADS):
        sl = slice(h * MEM_HEAD_DIM, (h + 1) * MEM_HEAD_DIM)
        s = lax.dot_general(q_ref[:, sl], k_ref[:, sl], (((1,), (1,)), ((), ())), preferred_element_type=F32)
        s = s - jnp.max(s, axis=-1, keepdims=True)
        p = jnp.exp(s)
        p = p / jnp.sum(p, axis=-1, keepdims=True)
        outs.append(jnp.dot(p.astype(BF16), v_ref[:, sl], preferred_element_type=F32))
    o_ref[...] = jnp.concatenate(outs, axis=-1).astype(o_ref.dtype)


def memory_attention(mq, mk, mv, *, b, t, mtok, tq):
    nq = t // tq
    w = N_MEM_HEADS * MEM_HEAD_DIM
    return pl.pallas_call(
        _mem_attn_kernel,
        out_shape=jax.ShapeDtypeStruct((b * t, w), BF16),
        grid=(b, nq),
        in_specs=[
            pl.BlockSpec((tq, w), lambda bi, i: (bi * nq + i, 0)),
            pl.BlockSpec((mtok, w), lambda bi, i: (bi, 0)),
            pl.BlockSpec((mtok, w), lambda bi, i: (bi, 0)),
        ],
        out_specs=pl.BlockSpec((tq, w), lambda bi, i: (bi * nq + i, 0)),
        compiler_params=_cparams(("parallel", "arbitrary")),
        name="memory_attention",
    )(mq, mk, mv)


def _merge_kernel(x_ref, ya_ref, yb_ref, yc_ref, ga_ref, gb_ref, gc_ref, bg_ref,
                  wa_ref, wb_ref, wc_ref, wo_ref, fg_ref, x1_ref, h2_ref):
    def branch(y_ref, g_ref, w_ref, idx):
        gate = _sigmoid(g_ref[...].astype(F32) + bg_ref[idx:idx + 1, :])
        return gate * jnp.dot(y_ref[...], w_ref[...], preferred_element_type=F32)

    merged = branch(ya_ref, ga_ref, wa_ref, 0) + branch(yb_ref, gb_ref, wb_ref, 1) + branch(yc_ref, gc_ref, wc_ref, 2)
    x1 = x_ref[...] + jnp.dot(merged.astype(BF16), wo_ref[...], preferred_element_type=F32)
    x1_ref[...] = x1
    ms = jnp.mean(x1 * x1, axis=-1, keepdims=True)
    h2_ref[...] = (x1 * lax.rsqrt(ms + EPS) * fg_ref[...]).astype(h2_ref.dtype)


def merge_project(x2d, y_diff, y_delta, y_mem, rest, b_gate, w_a, w_b, w_c, w_o, ffn_g, *, gate_col0, tm):
    m, d = x2d.shape
    row = lambda i: (i, 0)
    const = lambda i: (0, 0)
    assert gate_col0 % d == 0

    def gspec(j):
        return pl.BlockSpec((tm, d), lambda i: (i, gate_col0 // d + j))

    return pl.pallas_call(
        _merge_kernel,
        out_shape=(jax.ShapeDtypeStruct((m, d), F32), jax.ShapeDtypeStruct((m, d), BF16)),
        grid=(m // tm,),
        in_specs=[
            pl.BlockSpec((tm, d), row), pl.BlockSpec((tm, d), row), pl.BlockSpec((tm, d), row),
            pl.BlockSpec((tm, d), row), gspec(0), gspec(1), gspec(2),
            pl.BlockSpec((3, d), const),
            pl.BlockSpec((d, d), const), pl.BlockSpec((d, d), const), pl.BlockSpec((d, d), const),
            pl.BlockSpec((d, d), const), pl.BlockSpec((1, d), const),
        ],
        out_specs=(pl.BlockSpec((tm, d), row), pl.BlockSpec((tm, d), row)),
        compiler_params=_cparams(("parallel",)),
        name="merge_project",
    )(x2d, y_diff, y_delta, y_mem, rest, rest, rest, b_gate.reshape(3, d).astype(F32),
      w_a, w_b, w_c, w_o, ffn_g.reshape(1, d).astype(F32))


def _topk_rows(s, key, extra, k):
    big = jnp.iinfo(jnp.int32).max
    vals, keys, ext = [], [], []
    for _ in range(k):
        m = jnp.max(s, axis=0, keepdims=True)
        am = jnp.min(jnp.where(s == m, key, big), axis=0, keepdims=True)
        hit = key == am
        if extra is not None:
            ext.append(jnp.max(jnp.where(hit, extra, -1), axis=0, keepdims=True))
        s = jnp.where(hit, -jnp.inf, s)
        vals.append(m)
        keys.append(am)
    cat = lambda xs: jnp.concatenate(xs, axis=0)
    return cat(vals), cat(keys), (cat(ext) if extra is not None else None)


def _peer_route_kernel(q_ref, keys_ref, idx_ref, gate_ref, idx_acc, gate_acc):
    h = pl.program_id(1)
    k = PEER_TOPK
    q = q_ref[...]
    tt = q.shape[0]

    def half(p):
        qp = q[:, p * PEER_HALF:(p + 1) * PEER_HALF]
        st = lax.dot_general(keys_ref[0, p], qp, (((1,), (1,)), ((), ())), preferred_element_type=F32)
        v, r, _ = _topk_rows(st, lax.broadcasted_iota(jnp.int32, st.shape, 0), None, k)
        return v, r

    s1, i1 = half(0)
    s2, i2 = half(1)
    g = SUBLANES
    sub = lax.broadcasted_iota(jnp.int32, (g, tt), 0)
    e1 = i1 * PEER_KEYS
    pieces = []
    for b in range(g):
        lim = min(g, k // (b + 1))
        sc = s1[0:g] + s2[b:b + 1]
        if lim < g:
            sc = jnp.where(sub < lim, sc, -jnp.inf)
        pieces.append((sc, sub * k + b, e1[0:g] + i2[b:b + 1]))
    pieces.append((s1[g:k] + s2[0:1], (sub + g) * k, e1[g:k] + i2[0:1]))
    pieces.append((s1[0:1] + s2[g:k], sub + g, e1[0:1] + i2[g:k]))
    cand_s = jnp.concatenate([p[0] for p in pieces], axis=0)
    cand_k = jnp.concatenate([p[1] for p in pieces], axis=0)
    cand_e = jnp.concatenate([p[2] for p in pieces], axis=0)
    top_s, _, top_e = _topk_rows(cand_s, cand_k, cand_e, k)
    e = jnp.exp(top_s - top_s[0:1])
    rows = pl.ds(pl.multiple_of(h * k, k), k)
    gate_acc[rows, :] = e / jnp.sum(e, axis=0, keepdims=True)
    idx_acc[rows, :] = top_e.astype(F32)

    @pl.when(h == pl.num_programs(1) - 1)
    def _():
        gate_ref[...] = gate_acc[...].T
        idx_ref[...] = idx_acc[...].T.astype(jnp.int32)


def peer_route(qry, sub_keys, *, tt):
    m = qry.shape[0]
    nh = PEER_HEADS
    ne = nh * PEER_TOPK
    return pl.pallas_call(
        _peer_route_kernel,
        out_shape=(jax.ShapeDtypeStruct((m, ne), jnp.int32), jax.ShapeDtypeStruct((m, ne), F32)),
        grid=(m // tt, nh),
        in_specs=[
            pl.BlockSpec((tt, 2 * PEER_HALF), lambda i, h: (i, h)),
            pl.BlockSpec((1, 2, PEER_KEYS, PEER_HALF), lambda i, h: (h, 0, 0, 0)),
        ],
        out_specs=(pl.BlockSpec((tt, ne), lambda i, h: (i, 0)), pl.BlockSpec((tt, ne), lambda i, h: (i, 0))),
        scratch_shapes=[pltpu.VMEM((ne, tt), F32), pltpu.VMEM((ne, tt), F32)],
        compiler_params=_cparams(("parallel", "arbitrary")),
        name="peer_route",
    )(qry, sub_keys)


def _gelu_exact(x):
    return 0.5 * x * (1.0 + lax.erf(x * (2.0 ** -0.5)))


def _peer_expert_kernel(idx_ref, idxn_ref, gate_ref, h_ref, x_ref, pool_ref, poolt_ref, uv_hbm,
                        o_ref, *scratch, tb, ne, nbuf, dist):
    bufs, sem = scratch[:nbuf], scratch[nbuf]
    i = pl.program_id(0)
    n = pl.num_programs(0)
    rows = tb * ne
    d_sub = h_ref.shape[1] // LANES
    n_seg = 2 * tb
    seg = rows // n_seg

    def issue_rows(iref, row0, buf, which, r_lo, r_hi):
        for r in range(r_lo, r_hi):
            e = iref[row0 + r // ne, r % ne]
            pltpu.make_async_copy(uv_hbm.at[e], buf.at[r], sem.at[which, r % 2]).start(priority=r % 2)

    def wait_tile(buf, which):
        for k in range(2):
            pltpu.make_async_copy(uv_hbm.at[pl.ds(0, rows // 2)], buf.at[pl.ds(0, rows // 2)],
                                  sem.at[which, k]).wait()

    @pl.when(i == 0)
    def _():
        for a in range(dist):
            def tok(t, carry, a=a):
                for j in range(ne):
                    pltpu.make_async_copy(uv_hbm.at[idx_ref[a * tb + t, j]], bufs[a].at[t * ne + j],
                                          sem.at[a, j % 2]).start(priority=j % 2)
                return carry
            lax.fori_loop(0, tb, tok, 0)

    blk = 2 * d_sub
    sub_id = lax.broadcasted_iota(jnp.int32, (d_sub, ne * blk), 0)
    col_id = lax.broadcasted_iota(jnp.int32, (d_sub, ne * blk), 1)
    diag_u = (col_id % blk) == sub_id
    diag_v = (col_id % blk) == sub_id + d_sub

    def tile(row0, buf, nxt_iref, nxt_row0, nxt_buf, nxt_which):
        drows = []
        h8 = h_ref[pl.ds(row0, tb), :].astype(F32).reshape(tb, d_sub, LANES).astype(BF16)
        for t in range(tb):
            z = buf[pl.ds(t * ne, ne)].reshape(ne * blk, LANES)
            c = lax.dot_general(h8[t], z, (((1,), (1,)), ((), ())), preferred_element_type=F32)
            drows.append(jnp.sum(jnp.where(diag_u, c, 0.0), axis=0, keepdims=True))
            issue_rows(nxt_iref, nxt_row0, nxt_buf, nxt_which, t * seg, (t + 1) * seg)
        dall = jnp.concatenate(drows, axis=0)
        d_hi = dall.astype(BF16)
        d_lo = (dall - d_hi.astype(F32)).astype(BF16)
        a = (jnp.dot(d_hi, pool_ref[...], preferred_element_type=F32)
             + jnp.dot(d_lo, pool_ref[...], preferred_element_type=F32))
        w = gate_ref[pl.ds(row0, tb), :] * _gelu_exact(a)
        wrep = jnp.dot(w.astype(BF16), poolt_ref[...], preferred_element_type=F32)
        outs = []
        for t in range(tb):
            wexp = jnp.where(diag_v, wrep[t:t + 1, :], 0.0).astype(BF16)
            z = buf[pl.ds(t * ne, ne)].reshape(ne * blk, LANES)
            outs.append(jnp.dot(wexp, z, preferred_element_type=F32))
            issue_rows(nxt_iref, nxt_row0, nxt_buf, nxt_which, (tb + t) * seg, (tb + t + 1) * seg)
        o_ref[pl.ds(row0, tb), :] = x_ref[pl.ds(row0, tb), :] + jnp.stack(outs).reshape(tb, d_sub * LANES)

    for a in range(nbuf):
        wait_tile(bufs[a], a)
        nxt = a + dist
        if nxt < nbuf:
            tile(a * tb, bufs[a], idx_ref, nxt * tb, bufs[nxt], nxt)
        else:
            tile(a * tb, bufs[a], idxn_ref, (nxt - nbuf) * tb, bufs[nxt - nbuf], nxt - nbuf)

    @pl.when(i == n - 1)
    def _():
        for a in range(dist):
            wait_tile(bufs[a], a)


def peer_experts(idx, gates, h2d, x2d, uv3, *, tb):
    m, ne = idx.shape
    d_sub = h2d.shape[1] // LANES
    nbuf, dist = 4, 3
    ts = nbuf * tb
    n = m // ts
    rows = tb * ne
    blk = 2 * d_sub
    cid = jnp.arange(ne * blk) // blk
    pool = (cid[:, None] == jnp.arange(ne)[None, :]).astype(BF16)
    tok = lambda i: (i, 0)
    return pl.pallas_call(
        functools.partial(_peer_expert_kernel, tb=tb, ne=ne, nbuf=nbuf, dist=dist),
        out_shape=jax.ShapeDtypeStruct(x2d.shape, F32),
        grid=(n,),
        in_specs=[
            pl.BlockSpec((ts, ne), tok, memory_space=pltpu.SMEM),
            pl.BlockSpec((ts, ne), lambda i: (jnp.minimum(i + 1, n - 1), 0), memory_space=pltpu.SMEM),
            pl.BlockSpec((ts, ne), tok),
            pl.BlockSpec((ts, d_sub * LANES), tok),
            pl.BlockSpec((ts, d_sub * LANES), tok),
            pl.BlockSpec((ne * blk, ne), lambda i: (0, 0)),
            pl.BlockSpec((ne, ne * blk), lambda i: (0, 0)),
            pl.BlockSpec(memory_space=pl.ANY),
        ],
        out_specs=pl.BlockSpec((ts, d_sub * LANES), tok),
        scratch_shapes=[pltpu.VMEM((rows, blk, LANES), BF16) for _ in range(nbuf)]
        + [pltpu.SemaphoreType.DMA((nbuf, 2))],
        compiler_params=_cparams(("arbitrary",)),
        name="peer_experts",
    )(idx, idx, gates, h2d, x2d, pool, pool.T, uv3)


def kernel(x, mem, positions, attn_norm_g, mem_norm_g, w_in, b_gate, diff_q_norm_g, diff_k_norm_g, lambda_q1, lambda_k1, lambda_q2, lambda_k2, diff_subln_g, rel_bias_table, conv_w, a_log, dt_bias, delta_out_norm_g, w_mem_kv, mem_q_norm_g, mem_k_norm_g, w_br_diff, w_br_delta, w_br_mem, w_out, ffn_norm_g, w_query, sub_keys, expert_u, expert_v):
    del positions
    b, t, d = x.shape
    mtok = mem.shape[1]
    m = b * t
    depth = w_in.shape[0]
    nh = N_DIFF_HEADS
    qk_w = nh * 2 * DIFF_HEAD_DIM
    dv_w = nh * 2 * DIFF_HEAD_DIM
    dl_w = N_DELTA_HEADS * DELTA_HEAD_DIM
    mem_w = N_MEM_HEADS * MEM_HEAD_DIM
    tq = min(256, t)
    tt_delta = min(256, t)

    x2d = x.reshape(m, d)
    for l in range(depth):
        lam_init = 0.8 - 0.6 * math.exp(-0.3 * l)
        wl = w_in[l]
        o = 0
        w_qk = wl[:, o:o + 2 * qk_w]; o += 2 * qk_w
        w_dv = wl[:, o:o + dv_w]; o += dv_w
        w_lqkv = wl[:, o:o + 3 * dl_w]; o += 3 * dl_w
        w_lz = wl[:, o:o + dl_w]; o += dl_w
        w_ab = wl[:, o:o + 2 * N_DELTA_HEADS]; o += 2 * N_DELTA_HEADS
        w_mq = wl[:, o:o + mem_w]; o += mem_w
        w_gate = wl[:, o:o + 3 * d]

        qk_gain = jnp.concatenate([jnp.tile(diff_q_norm_g[l], 2 * nh) * (DIFF_HEAD_DIM ** -0.5 * LOG2E),
                                   jnp.tile(diff_k_norm_g[l], 2 * nh)])
        qk = norm_matmul(x2d, w_qk.astype(BF16), norm_g=attn_norm_g[l], group=DIFF_HEAD_DIM, group_gain=qk_gain)
        w_rest = jnp.concatenate([w_dv, w_lqkv, w_lz, w_gate], axis=1).astype(BF16)
        rest = norm_matmul(x2d, w_rest, norm_g=attn_norm_g[l])
        mq_gain = jnp.tile(mem_q_norm_g[l], N_MEM_HEADS) * (MEM_HEAD_DIM ** -0.5)
        mq = norm_matmul(x2d, w_mq.astype(BF16), norm_g=attn_norm_g[l], group=MEM_HEAD_DIM, group_gain=mq_gain)
        w_ab_pad = jnp.pad(w_ab, ((0, 0), (0, LANES - 2 * N_DELTA_HEADS))).astype(BF16)
        ab = norm_matmul(x2d, w_ab_pad, norm_g=attn_norm_g[l], out_dtype=F32, tn=LANES)
        abt = ab[:, :2 * N_DELTA_HEADS].T

        lam = (jnp.exp(jnp.sum(lambda_q1[l].astype(F32) * lambda_k1[l].astype(F32)))
               - jnp.exp(jnp.sum(lambda_q2[l].astype(F32) * lambda_k2[l].astype(F32))) + lam_init)
        bias_tiles = rel_bias_tiles(rel_bias_table, tq)
        scal = jnp.concatenate([jnp.stack([lam, jnp.asarray(1.0 - lam_init, F32)]),
                                rel_bias_table[REL_BUCKETS - 1].astype(F32) * LOG2E])
        vt = rest[:, :dv_w].reshape(b, t // tq, tq, nh, LANES).transpose(0, 3, 1, 4, 2)
        y_diff = diff_attention(qk, vt, bias_tiles, scal, diff_subln_g[l], b=b, t=t, tq=tq, hpb=4)

        head_params = jnp.stack([a_log[l], dt_bias[l]]).astype(F32)
        y_delta = gated_deltanet(rest, ab, abt, conv_w[l].astype(F32), head_params, delta_out_norm_g[l],
                                 b=b, t=t, tt=tt_delta, nhb=8, qkv_col0=dv_w // LANES,
                                 z_col0=(dv_w + 3 * dl_w) // LANES)

        mem2d = mem.reshape(b * mtok, d)
        wkv = w_mem_kv[l].astype(BF16)
        mk = norm_matmul(mem2d, wkv[:, :mem_w], norm_g=mem_norm_g[l], group=MEM_HEAD_DIM,
                         group_gain=jnp.tile(mem_k_norm_g[l], N_MEM_HEADS))
        mv = norm_matmul(mem2d, wkv[:, mem_w:], norm_g=mem_norm_g[l])
        y_mem = memory_attention(mq, mk, mv, b=b, t=t, mtok=mtok, tq=min(512, t))

        x1, h2 = merge_project(x2d, y_diff, y_delta, y_mem, rest, b_gate[l],
                               w_br_diff[l].astype(BF16), w_br_delta[l].astype(BF16), w_br_mem[l].astype(BF16),
                               w_out[l].astype(BF16), ffn_norm_g[l],
                               gate_col0=dv_w + 4 * dl_w, tm=min(512, m))

        qry = norm_matmul(h2, w_query[l].astype(BF16))
        idx, gates = peer_route(qry, sub_keys[l].astype(BF16), tt=min(512, m))
        sub = d // LANES
        uv3 = jnp.concatenate([expert_u[l].reshape(-1, sub, LANES), expert_v[l].reshape(-1, sub, LANES)],
                              axis=1).astype(BF16)
        x2d = peer_experts(idx, gates, h2, x1, uv3, tb=8)
    return x2d.reshape(b, t, d)
```

```python
import functools
import math

import jax
import jax.numpy as jnp
from jax import lax
from jax.experimental import pallas as pl
from jax.experimental.pallas import tpu as pltpu

F32 = jnp.float32
BF16 = jnp.bfloat16
EPS = 1e-6
NEG = -1e30
LOG2E = math.log2(math.e)

N_DIFF_HEADS = 8
DIFF_HEAD_DIM = 64
N_DELTA_HEADS = 8
DELTA_HEAD_DIM = 128
CONV_WIDTH = 4
CHUNK = 64
N_MEM_HEADS = 4
MEM_HEAD_DIM = 256
REL_BUCKETS = 32
REL_MAX_DIST = 128
PEER_HEADS = 8
PEER_KEYS = 128
PEER_TOPK = 16
PEER_HALF = 128
LANES = 128
SUBLANES = 8
VMEM_LIMIT = 56 * 1024 * 1024

MATMUL_TM = 1024
MATMUL_TN = 2048
MATMUL_TN_GROUP = 512
ATTN_TQ = 256
ATTN_HEADS_PER_STEP = 4
DELTA_TT = 256
DELTA_HEADS_PER_STEP = 8
MEM_ATTN_TQ = 512
MERGE_TM = 512
ROUTE_TT = 1024
PEER_TB = 8
PEER_NBUF = 4
PEER_DIST = 3


def _cparams(sem):
    return pltpu.CompilerParams(dimension_semantics=sem, vmem_limit_bytes=VMEM_LIMIT)


def _norm_matmul_kernel(*refs, has_norm, has_group):
    it = iter(refs)
    x_ref = next(it)
    g_ref = next(it) if has_norm else None
    w_ref = next(it)
    gm_ref = next(it) if has_group else None
    gain_ref = next(it) if has_group else None
    o_ref = next(it)
    h_ref = next(it)

    @pl.when(pl.program_id(1) == 0)
    def _():
        x = x_ref[...].astype(F32)
        if has_norm:
            ms = jnp.mean(x * x, axis=-1, keepdims=True)
            x = x * lax.rsqrt(ms + EPS) * g_ref[...]
        h_ref[...] = x.astype(BF16)

    y = jnp.dot(h_ref[...], w_ref[...], preferred_element_type=F32)
    if has_group:
        ms = jnp.dot((y * y).astype(BF16), gm_ref[...], preferred_element_type=F32)
        y = y * lax.rsqrt(ms + EPS) * gain_ref[...]
    o_ref[...] = y.astype(o_ref.dtype)


def norm_matmul(x, w, *, norm_g=None, group=None, group_gain=None, out_dtype=BF16, tm=MATMUL_TM, tn=None):
    m, k = x.shape
    n = w.shape[1]
    has_norm = norm_g is not None
    has_group = group is not None
    if tn is None:
        tn = MATMUL_TN_GROUP if has_group else min(n, MATMUL_TN)
    tm = min(tm, m)
    assert m % tm == 0 and n % tn == 0
    in_specs = [pl.BlockSpec((tm, k), lambda i, j: (i, 0))]
    args = [x]
    if has_norm:
        in_specs.append(pl.BlockSpec((1, k), lambda i, j: (0, 0)))
        args.append(norm_g.reshape(1, k).astype(F32))
    in_specs.append(pl.BlockSpec((k, tn), lambda i, j: (0, j)))
    args.append(w)
    if has_group:
        gid = jnp.arange(tn) // group
        gm = jnp.where(gid[:, None] == gid[None, :], 1.0 / group, 0.0).astype(BF16)
        in_specs.append(pl.BlockSpec((tn, tn), lambda i, j: (0, 0)))
        args.append(gm)
        in_specs.append(pl.BlockSpec((1, tn), lambda i, j: (0, j)))
        args.append(group_gain.reshape(1, n).astype(F32))
    return pl.pallas_call(
        functools.partial(_norm_matmul_kernel, has_norm=has_norm, has_group=has_group),
        out_shape=jax.ShapeDtypeStruct((m, n), out_dtype),
        grid=(m // tm, n // tn),
        in_specs=in_specs,
        out_specs=pl.BlockSpec((tm, tn), lambda i, j: (i, j)),
        scratch_shapes=[pltpu.VMEM((tm, k), BF16)],
        compiler_params=_cparams(("parallel", "arbitrary")),
        name="norm_matmul",
    )(*args)


def _t5_bucket(n):
    max_exact = REL_BUCKETS // 2
    nf = jnp.maximum(n, 1).astype(F32)
    large = max_exact + (jnp.log(nf / max_exact) / math.log(REL_MAX_DIST / max_exact)
                         * (REL_BUCKETS - max_exact)).astype(jnp.int32)
    large = jnp.minimum(large, REL_BUCKETS - 1)
    return jnp.where(n < max_exact, n, large)


def _rel_bias_kernel(table_ref, o_ref, *, tq):
    hm = pl.program_id(0)
    c = lax.broadcasted_iota(jnp.int32, (2 * tq, tq), 0)
    r = lax.broadcasted_iota(jnp.int32, (2 * tq, tq), 1)
    bucket = _t5_bucket(jnp.maximum(r - c + tq, 0))
    acc = jnp.zeros((2 * tq, tq), F32)
    for b in range(REL_BUCKETS):
        acc = jnp.where(bucket == b, table_ref[b, hm], acc)
    o_ref[0] = acc * LOG2E


def rel_bias_tiles(rel_table, tq):
    nmaps = rel_table.shape[1]
    return pl.pallas_call(
        functools.partial(_rel_bias_kernel, tq=tq),
        out_shape=jax.ShapeDtypeStruct((nmaps, 2 * tq, tq), F32),
        grid=(nmaps,),
        in_specs=[pl.BlockSpec(memory_space=pltpu.SMEM)],
        out_specs=pl.BlockSpec((1, 2 * tq, tq), lambda i: (i, 0, 0)),
        compiler_params=_cparams(("arbitrary",)),
        name="rel_bias_tiles",
    )(rel_table.astype(F32))


def _diff_attn_kernel(scal_ref, q_ref, k_ref, vt_ref, bias_ref, subg_ref, o_ref, *, tq, hpb):
    hb = pl.program_id(1)
    i = pl.program_id(2)
    heads = range(hpb)
    lam = scal_ref[0]
    out_scale = scal_ref[1]
    w2 = 2 * tq

    sub = lax.broadcasted_iota(jnp.int32, (LANES, tq), 0)
    col2 = lax.broadcasted_iota(jnp.int32, (1, w2), 1)
    qt, cfar = [], []
    for hd in heads:
        q = q_ref[:, hd * LANES:(hd + 1) * LANES].astype(F32).T
        qt.append(jnp.concatenate([jnp.where(sub < DIFF_HEAD_DIM, q, 0.0),
                                   jnp.where(sub >= DIFF_HEAD_DIM, q, 0.0)], axis=1).astype(BF16))
        h = hb * hpb + hd
        cfar.append(jnp.where(col2 < tq, scal_ref[2 + 2 * h], scal_ref[3 + 2 * h]))

    def update(carry, s, vt, shift=None):
        m, l, acc = carry
        colmax = [jnp.max(x, axis=0, keepdims=True) for x in s]
        if shift is not None:
            colmax = [a + c for a, c in zip(colmax, shift)]
        m_new = [jnp.maximum(a, b_) for a, b_ in zip(m, colmax)]
        alpha = [jnp.exp2(a - b_) for a, b_ in zip(m, m_new)]
        sub_ = m_new if shift is None else [a - c for a, c in zip(m_new, shift)]
        p = [jnp.exp2(x - a) for x, a in zip(s, sub_)]
        l = [a * b_ + jnp.sum(x, axis=0, keepdims=True) for a, b_, x in zip(l, alpha, p)]
        acc = [a * b_ + jnp.dot(v_, x.astype(BF16), preferred_element_type=F32)
               for a, b_, x, v_ in zip(acc, alpha, p, vt)]
        return m_new, l, acc

    def scores(tile0, ntiles):
        rows = pl.ds(pl.multiple_of(tile0 * tq, tq), ntiles * tq)
        s = [jnp.dot(k_ref[rows, hd * LANES:(hd + 1) * LANES], qt[hd], preferred_element_type=F32) for hd in heads]
        vt = [jnp.concatenate([vt_ref[0, hd, tile0 + c] for c in range(ntiles)], axis=1) if ntiles > 1
              else vt_ref[0, hd, tile0] for hd in heads]
        return s, vt

    def pack(carry):
        m, l, acc = carry
        return tuple(m) + tuple(l) + tuple(acc)

    def unpack(flat):
        return list(flat[0:hpb]), list(flat[hpb:2 * hpb]), list(flat[2 * hpb:3 * hpb])

    def far_step(j, flat):
        s, vt = scores(2 * j, 2)
        return pack(update(unpack(flat), s, vt, cfar))

    def far_single(flat):
        s, vt = scores(i - 2, 1)
        return pack(update(unpack(flat), s, vt, cfar))

    n_far = jnp.maximum(i - 1, 0)
    init = ([jnp.full((1, w2), NEG, F32) for _ in heads], [jnp.zeros((1, w2), F32) for _ in heads],
            [jnp.zeros((LANES, w2), F32) for _ in heads])
    flat = lax.fori_loop(0, n_far // 2, far_step, pack(init))
    flat = lax.cond(n_far % 2 == 1, far_single, lambda c: c, flat)
    carry = unpack(flat)

    def near_bias(hd, lo):
        return jnp.concatenate([bias_ref[2 * hd, lo:lo + tq, :], bias_ref[2 * hd + 1, lo:lo + tq, :]], axis=1)

    s, vt = scores(jnp.maximum(i - 1, 0), 1)
    s = [jnp.where(i >= 1, x + near_bias(hd, 0), NEG) for hd, x in zip(heads, s)]
    carry = update(carry, s, vt)

    s, vt = scores(i, 1)
    kk = lax.broadcasted_iota(jnp.int32, (tq, w2), 0)
    qi = lax.broadcasted_iota(jnp.int32, (tq, w2), 1)
    qi = jnp.where(qi >= tq, qi - tq, qi)
    s = [jnp.where(kk <= qi, x + near_bias(hd, tq), NEG) for hd, x in zip(heads, s)]
    m, l, acc = update(carry, s, vt)

    outs = []
    for hd in heads:
        o = acc[hd] / l[hd]
        o = o[:, 0:tq] - lam * o[:, tq:w2]
        ms = jnp.mean(o * o, axis=0, keepdims=True)
        o = o * lax.rsqrt(ms + EPS) * subg_ref[...] * out_scale
        outs.append(o.T)
    o_ref[...] = jnp.concatenate(outs, axis=-1).astype(o_ref.dtype)


def diff_attention(qk, vt, bias_tiles, scal, subln_g, *, b, t, tq, hpb):
    m = b * t
    nq = t // tq
    nh = N_DIFF_HEADS
    wl = hpb * LANES
    assert nh % hpb == 0
    return pl.pallas_call(
        functools.partial(_diff_attn_kernel, tq=tq, hpb=hpb),
        out_shape=jax.ShapeDtypeStruct((m, nh * LANES), BF16),
        grid=(b, nh // hpb, nq),
        in_specs=[
            pl.BlockSpec(memory_space=pltpu.SMEM),
            pl.BlockSpec((tq, wl), lambda bi, h, i: (bi * nq + i, h)),
            pl.BlockSpec((t, wl), lambda bi, h, i: (bi, nh // hpb + h)),
            pl.BlockSpec((1, hpb, nq, LANES, tq), lambda bi, h, i: (bi, h, 0, 0, 0)),
            pl.BlockSpec((2 * hpb, 2 * tq, tq), lambda bi, h, i: (h, 0, 0)),
            pl.BlockSpec((LANES, 1), lambda bi, h, i: (0, 0)),
        ],
        out_specs=pl.BlockSpec((tq, wl), lambda bi, h, i: (bi * nq + i, h)),
        compiler_params=_cparams(("parallel", "parallel", "arbitrary")),
        name="diff_attention",
    )(scal, qk, qk, vt, bias_tiles, subln_g.reshape(LANES, 1).astype(F32))


def _sigmoid(x):
    return 1.0 / (1.0 + jnp.exp(-x))


def _softplus(x):
    return jnp.maximum(x, 0.0) + jnp.log(1.0 + jnp.exp(-jnp.abs(x)))


def _delta_kernel(hp_ref, xq_ref, xk_ref, xv_ref, pq_ref, pk_ref, pv_ref, wq_ref, wk_ref, wv_ref,
                  ab_ref, abt_ref, z_ref, og_ref, o_ref, s_ref, *, tt, nhb):
    hb = pl.program_id(1)
    i = pl.program_id(2)
    nc = tt // CHUNK
    dh = DELTA_HEAD_DIM
    heads = range(nhb)

    @pl.when(i == 0)
    def _():
        s_ref[...] = jnp.zeros_like(s_ref)

    def conv_silu(x_ref, p_ref, w_ref):
        prev = jnp.where(i > 0, p_ref[...].astype(F32), 0.0)
        xf = jnp.concatenate([prev, x_ref[...].astype(F32)], axis=0)
        w = w_ref[...]
        base = SUBLANES - (CONV_WIDTH - 1)
        y = xf[base:base + tt] * w[0:1]
        for c in range(1, CONV_WIDTH):
            y = y + xf[base + c:base + c + tt] * w[c:c + 1]
        return y * _sigmoid(y)

    def split(x):
        return [x[:, hd * dh:(hd + 1) * dh] for hd in heads]

    q = split(conv_silu(xq_ref, pq_ref, wq_ref))
    k = split(conv_silu(xk_ref, pk_ref, wk_ref))
    v = split(conv_silu(xv_ref, pv_ref, wv_ref))
    q = [x * lax.rsqrt(jnp.sum(x * x, axis=-1, keepdims=True) + EPS) * (dh ** -0.5) for x in q]
    k = [x * lax.rsqrt(jnp.sum(x * x, axis=-1, keepdims=True) + EPS) for x in k]

    ab = ab_ref[...]
    lane = lax.broadcasted_iota(jnp.int32, ab.shape, 1)
    beta_col, g_col, g_row = [], [], []
    for hd in heads:
        h = hb * nhb + hd
        neg_a = -jnp.exp(jnp.full((1, 1), hp_ref[0, h], F32))
        dt_bias = hp_ref[1, h]
        lb_col = jnp.sum(jnp.where(lane == h, ab, 0.0), axis=-1, keepdims=True)
        la_col = jnp.sum(jnp.where(lane == N_DELTA_HEADS + h, ab, 0.0), axis=-1, keepdims=True)
        la_row = abt_ref[pl.ds(N_DELTA_HEADS + h, 1), :]
        beta_col.append(_sigmoid(lb_col))
        g_col.append(neg_a * _softplus(la_col + dt_bias))
        g_row.append(neg_a * _softplus(la_row + dt_bias))

    ri = lax.broadcasted_iota(jnp.int32, (tt, tt), 0)
    ci = lax.broadcasted_iota(jnp.int32, (tt, tt), 1)
    same = (ri // CHUNK) == (ci // CHUNK)
    tril = same & (ri >= ci)
    strict = same & (ri > ci)
    triu = same & (ri <= ci)

    def mm(a, b):
        return jnp.dot(a.astype(BF16), b.astype(BF16), preferred_element_type=F32)

    def mm_nt(a, b):
        return lax.dot_general(a.astype(BF16), b.astype(BF16), (((1,), (1,)), ((), ())),
                               preferred_element_type=F32)

    def mm_tn(a, b):
        return lax.dot_general(a.astype(BF16), b.astype(BF16), (((0,), (0,)), ((), ())),
                               preferred_element_type=F32)

    gc_col = [jnp.sum(jnp.where(tril, g, 0.0), axis=-1, keepdims=True) for g in g_row]
    gc_row = [jnp.sum(jnp.where(triu, g, 0.0), axis=0, keepdims=True) for g in g_col]
    decay = [jnp.where(tril, jnp.exp(jnp.where(tril, a - b, 0.0)), 0.0) for a, b in zip(gc_col, gc_row)]
    e_gc = [jnp.exp(g) for g in gc_col]
    kk = [mm_nt(x, x) for x in k]
    qk = [mm_nt(a, b) for a, b in zip(q, k)]
    xm = [jnp.where(strict, -(b * kx * d), 0.0) for b, kx, d in zip(beta_col, kk, decay)]
    intra = [jnp.where(tril, a * d, 0.0) for a, d in zip(qk, decay)]
    r = [jnp.concatenate([b * vx, (b * e) * kx], axis=-1) for b, vx, e, kx in zip(beta_col, v, e_gc, k)]
    n_sq = CHUNK.bit_length() - 1
    for lvl in range(n_sq):
        r = [x + mm(m_, x) for m_, x in zip(xm, r)]
        if lvl + 1 < n_sq:
            xm = [mm(m_, m_) for m_ in xm]
    u = [x[:, 0:dh] for x in r]
    w = [x[:, dh:2 * dh] for x in r]
    qg = [a * e for a, e in zip(q, e_gc)]

    state = [s_ref[hd] for hd in heads]
    o_state = [[] for _ in heads]
    v_new = [[] for _ in heads]
    for c in range(nc):
        r0 = c * CHUNK
        sl = slice(r0, r0 + CHUNK)
        g_last = [g[r0 + CHUNK - 1:r0 + CHUNK] for g in gc_col]
        vn = [u[hd][sl] - mm(w[hd][sl], state[hd]) for hd in heads]
        os_ = [mm(qg[hd][sl], state[hd]) for hd in heads]
        k_dec = [k[hd][sl] * jnp.exp(g_last[hd] - gc_col[hd][sl]) for hd in heads]
        state = [state[hd] * jnp.exp(g_last[hd]) + mm_tn(k_dec[hd], vn[hd]) for hd in heads]
        for hd in heads:
            v_new[hd].append(vn[hd])
            o_state[hd].append(os_[hd])
    for hd in heads:
        s_ref[hd] = state[hd]

    outs = []
    for hd in heads:
        o = jnp.concatenate(o_state[hd], axis=0) + mm(intra[hd], jnp.concatenate(v_new[hd], axis=0))
        ms = jnp.mean(o * o, axis=-1, keepdims=True)
        outs.append(o * lax.rsqrt(ms + EPS) * og_ref[...])
    z = z_ref[...].astype(F32)
    o_ref[...] = (jnp.concatenate(outs, axis=-1) * (z * _sigmoid(z))).astype(o_ref.dtype)


def gated_deltanet(rest, ab, abt, conv_w, head_params, out_g, *, b, t, tt, nhb, qkv_col0, z_col0):
    m = b * t
    nt = t // tt
    nh = N_DELTA_HEADS
    sub = tt // SUBLANES
    wl = nhb * LANES
    assert nh % nhb == 0 and qkv_col0 % nhb == 0 and z_col0 % nhb == 0

    def cur(off):
        return pl.BlockSpec((tt, wl), lambda bi, h, i: (bi * nt + i, off // nhb + h))

    def prev(off):
        return pl.BlockSpec((SUBLANES, wl),
                            lambda bi, h, i: (jnp.maximum((bi * nt + i) * sub - 1, 0), off // nhb + h))

    def wspec(off):
        return pl.BlockSpec((CONV_WIDTH, wl), lambda bi, h, i: (0, off // nhb + h))

    return pl.pallas_call(
        functools.partial(_delta_kernel, tt=tt, nhb=nhb),
        out_shape=jax.ShapeDtypeStruct((m, nh * LANES), BF16),
        grid=(b, nh // nhb, nt),
        in_specs=[
            pl.BlockSpec(memory_space=pltpu.SMEM),
            cur(qkv_col0), cur(qkv_col0 + nh), cur(qkv_col0 + 2 * nh),
            prev(qkv_col0), prev(qkv_col0 + nh), prev(qkv_col0 + 2 * nh),
            wspec(0), wspec(nh), wspec(2 * nh),
            pl.BlockSpec((tt, LANES), lambda bi, h, i: (bi * nt + i, 0)),
            pl.BlockSpec((2 * nh, tt), lambda bi, h, i: (0, bi * nt + i)),
            cur(z_col0),
            pl.BlockSpec((1, LANES), lambda bi, h, i: (0, 0)),
        ],
        out_specs=pl.BlockSpec((tt, wl), lambda bi, h, i: (bi * nt + i, h)),
        scratch_shapes=[pltpu.VMEM((nhb, DELTA_HEAD_DIM, DELTA_HEAD_DIM), F32)],
        compiler_params=_cparams(("parallel", "parallel", "arbitrary")),
        name="gated_deltanet",
    )(head_params, rest, rest, rest, rest, rest, rest, conv_w, conv_w, conv_w, ab, abt, rest,
      out_g.reshape(1, LANES).astype(F32))


def _mem_attn_kernel(q_ref, k_ref, v_ref, o_ref):
    outs = []
    for h in range(N_MEM_HEADS):
        sl = slice(h * MEM_HEAD_DIM, (h + 1) * MEM_HEAD_DIM)
        s = lax.dot_general(q_ref[:, sl], k_ref[:, sl], (((1,), (1,)), ((), ())), preferred_element_type=F32)
        s = s - jnp.max(s, axis=-1, keepdims=True)
        p = jnp.exp(s)
        p = p / jnp.sum(p, axis=-1, keepdims=True)
        outs.append(jnp.dot(p.astype(BF16), v_ref[:, sl], preferred_element_type=F32))
    o_ref[...] = jnp.concatenate(outs, axis=-1).astype(o_ref.dtype)


def memory_attention(mq, mk, mv, *, b, t, mtok, tq):
    nq = t // tq
    w = N_MEM_HEADS * MEM_HEAD_DIM
    return pl.pallas_call(
        _mem_attn_kernel,
        out_shape=jax.ShapeDtypeStruct((b * t, w), BF16),
        grid=(b, nq),
        in_specs=[
            pl.BlockSpec((tq, w), lambda bi, i: (bi * nq + i, 0)),
            pl.BlockSpec((mtok, w), lambda bi, i: (bi, 0)),
            pl.BlockSpec((mtok, w), lambda bi, i: (bi, 0)),
        ],
        out_specs=pl.BlockSpec((tq, w), lambda bi, i: (bi * nq + i, 0)),
        compiler_params=_cparams(("parallel", "arbitrary")),
        name="memory_attention",
    )(mq, mk, mv)


def _merge_kernel(x_ref, ya_ref, yb_ref, yc_ref, ga_ref, gb_ref, gc_ref, bg_ref,
                  wa_ref, wb_ref, wc_ref, wo_ref, fg_ref, x1_ref, h2_ref):
    def branch(y_ref, g_ref, w_ref, idx):
        gate = _sigmoid(g_ref[...].astype(F32) + bg_ref[idx:idx + 1, :])
        return gate * jnp.dot(y_ref[...], w_ref[...], preferred_element_type=F32)

    merged = branch(ya_ref, ga_ref, wa_ref, 0) + branch(yb_ref, gb_ref, wb_ref, 1) + branch(yc_ref, gc_ref, wc_ref, 2)
    x1 = x_ref[...] + jnp.dot(merged.astype(BF16), wo_ref[...], preferred_element_type=F32)
    x1_ref[...] = x1
    ms = jnp.mean(x1 * x1, axis=-1, keepdims=True)
    h2_ref[...] = (x1 * lax.rsqrt(ms + EPS) * fg_ref[...]).astype(h2_ref.dtype)


def merge_project(x2d, y_diff, y_delta, y_mem, rest, b_gate, w_a, w_b, w_c, w_o, ffn_g, *, gate_col0, tm):
    m, d = x2d.shape
    row = lambda i: (i, 0)
    const = lambda i: (0, 0)
    assert gate_col0 % d == 0

    def gspec(j):
        return pl.BlockSpec((tm, d), lambda i: (i, gate_col0 // d + j))

    return pl.pallas_call(
        _merge_kernel,
        out_shape=(jax.ShapeDtypeStruct((m, d), F32), jax.ShapeDtypeStruct((m, d), BF16)),
        grid=(m // tm,),
        in_specs=[
            pl.BlockSpec((tm, d), row), pl.BlockSpec((tm, d), row), pl.BlockSpec((tm, d), row),
            pl.BlockSpec((tm, d), row), gspec(0), gspec(1), gspec(2),
            pl.BlockSpec((3, d), const),
            pl.BlockSpec((d, d), const), pl.BlockSpec((d, d), const), pl.BlockSpec((d, d), const),
            pl.BlockSpec((d, d), const), pl.BlockSpec((1, d), const),
        ],
        out_specs=(pl.BlockSpec((tm, d), row), pl.BlockSpec((tm, d), row)),
        compiler_params=_cparams(("parallel",)),
        name="merge_project",
    )(x2d, y_diff, y_delta, y_mem, rest, rest, rest, b_gate.reshape(3, d).astype(F32),
      w_a, w_b, w_c, w_o, ffn_g.reshape(1, d).astype(F32))


def _topk_rows(s, key, extra, k):
    big = jnp.iinfo(jnp.int32).max
    vals, keys, ext = [], [], []
    for _ in range(k):
        m = jnp.max(s, axis=0, keepdims=True)
        am = jnp.min(jnp.where(s == m, key, big), axis=0, keepdims=True)
        hit = key == am
        if extra is not None:
            ext.append(jnp.max(jnp.where(hit, extra, -1), axis=0, keepdims=True))
        s = jnp.where(hit, -jnp.inf, s)
        vals.append(m)
        keys.append(am)
    cat = lambda xs: jnp.concatenate(xs, axis=0)
    return cat(vals), cat(keys), (cat(ext) if extra is not None else None)


def _peer_route_kernel(q_ref, keys_ref, idx_ref, gate_ref, idx_acc, gate_acc):
    h = pl.program_id(1)
    k = PEER_TOPK
    q = q_ref[...]
    tt = q.shape[0]

    def half(p):
        qp = q[:, p * PEER_HALF:(p + 1) * PEER_HALF]
        st = lax.dot_general(keys_ref[0, p], qp, (((1,), (1,)), ((), ())), preferred_element_type=F32)
        v, r, _ = _topk_rows(st, lax.broadcasted_iota(jnp.int32, st.shape, 0), None, k)
        return v, r

    s1, i1 = half(0)
    s2, i2 = half(1)
    g = SUBLANES
    sub = lax.broadcasted_iota(jnp.int32, (g, tt), 0)
    e1 = i1 * PEER_KEYS
    pieces = []
    for b in range(g):
        lim = min(g, k // (b + 1))
        sc = s1[0:g] + s2[b:b + 1]
        if lim < g:
            sc = jnp.where(sub < lim, sc, -jnp.inf)
        pieces.append((sc, sub * k + b, e1[0:g] + i2[b:b + 1]))
    pieces.append((s1[g:k] + s2[0:1], (sub + g) * k, e1[g:k] + i2[0:1]))
    pieces.append((s1[0:1] + s2[g:k], sub + g, e1[0:1] + i2[g:k]))
    cand_s = jnp.concatenate([p[0] for p in pieces], axis=0)
    cand_k = jnp.concatenate([p[1] for p in pieces], axis=0)
    cand_e = jnp.concatenate([p[2] for p in pieces], axis=0)
    top_s, _, top_e = _topk_rows(cand_s, cand_k, cand_e, k)
    e = jnp.exp(top_s - top_s[0:1])
    rows = pl.ds(pl.multiple_of(h * k, k), k)
    gate_acc[rows, :] = e / jnp.sum(e, axis=0, keepdims=True)
    idx_acc[rows, :] = top_e.astype(F32)

    @pl.when(h == pl.num_programs(1) - 1)
    def _():
        gate_ref[...] = gate_acc[...].T
        idx_ref[...] = idx_acc[...].T.astype(jnp.int32)


def peer_route(qry, sub_keys, *, tt):
    m = qry.shape[0]
    nh = PEER_HEADS
    ne = nh * PEER_TOPK
    return pl.pallas_call(
        _peer_route_kernel,
        out_shape=(jax.ShapeDtypeStruct((m, ne), jnp.int32), jax.ShapeDtypeStruct((m, ne), F32)),
        grid=(m // tt, nh),
        in_specs=[
            pl.BlockSpec((tt, 2 * PEER_HALF), lambda i, h: (i, h)),
            pl.BlockSpec((1, 2, PEER_KEYS, PEER_HALF), lambda i, h: (h, 0, 0, 0)),
        ],
        out_specs=(pl.BlockSpec((tt, ne), lambda i, h: (i, 0)), pl.BlockSpec((tt, ne), lambda i, h: (i, 0))),
        scratch_shapes=[pltpu.VMEM((ne, tt), F32), pltpu.VMEM((ne, tt), F32)],
        compiler_params=_cparams(("parallel", "arbitrary")),
        name="peer_route",
    )(qry, sub_keys)


def _gelu_exact(x):
    return 0.5 * x * (1.0 + lax.erf(x * (2.0 ** -0.5)))


def _peer_expert_kernel(idx_ref, idxn_ref, gate_ref, h_ref, x_ref, pool_ref, poolt_ref, uv_hbm,
                        o_ref, *scratch, tb, ne, nbuf, dist):
    bufs, sem = scratch[:nbuf], scratch[nbuf]
    i = pl.program_id(0)
    n = pl.num_programs(0)
    rows = tb * ne
    d_sub = h_ref.shape[1] // LANES
    n_seg = 2 * tb
    seg = rows // n_seg

    def issue_rows(iref, row0, buf, which, r_lo, r_hi):
        for r in range(r_lo, r_hi):
            e = iref[row0 + r // ne, r % ne]
            pltpu.make_async_copy(uv_hbm.at[e], buf.at[r], sem.at[which, r % 2]).start(priority=r % 2)

    def wait_tile(buf, which):
        for k in range(2):
            pltpu.make_async_copy(uv_hbm.at[pl.ds(0, rows // 2)], buf.at[pl.ds(0, rows // 2)],
                                  sem.at[which, k]).wait()

    @pl.when(i == 0)
    def _():
        for a in range(dist):
            def tok(t, carry, a=a):
                for j in range(ne):
                    pltpu.make_async_copy(uv_hbm.at[idx_ref[a * tb + t, j]], bufs[a].at[t * ne + j],
                                          sem.at[a, j % 2]).start(priority=j % 2)
                return carry
            lax.fori_loop(0, tb, tok, 0)

    blk = 2 * d_sub
    sub_id = lax.broadcasted_iota(jnp.int32, (d_sub, ne * blk), 0)
    col_id = lax.broadcasted_iota(jnp.int32, (d_sub, ne * blk), 1)
    diag_u = (col_id % blk) == sub_id
    diag_v = (col_id % blk) == sub_id + d_sub

    def tile(row0, buf, nxt_iref, nxt_row0, nxt_buf, nxt_which):
        drows = []
        h8 = h_ref[pl.ds(row0, tb), :].astype(F32).reshape(tb, d_sub, LANES).astype(BF16)
        for t in range(tb):
            z = buf[pl.ds(t * ne, ne)].reshape(ne * blk, LANES)
            c = lax.dot_general(h8[t], z, (((1,), (1,)), ((), ())), preferred_element_type=F32)
            drows.append(jnp.sum(jnp.where(diag_u, c, 0.0), axis=0, keepdims=True))
            issue_rows(nxt_iref, nxt_row0, nxt_buf, nxt_which, t * seg, (t + 1) * seg)
        dall = jnp.concatenate(drows, axis=0)
        d_hi = dall.astype(BF16)
        d_lo = (dall - d_hi.astype(F32)).astype(BF16)
        a = (jnp.dot(d_hi, pool_ref[...], preferred_element_type=F32)
             + jnp.dot(d_lo, pool_ref[...], preferred_element_type=F32))
        w = gate_ref[pl.ds(row0, tb), :] * _gelu_exact(a)
        wrep = jnp.dot(w.astype(BF16), poolt_ref[...], preferred_element_type=F32)
        outs = []
        for t in range(tb):
            wexp = jnp.where(diag_v, wrep[t:t + 1, :], 0.0).astype(BF16)
            z = buf[pl.ds(t * ne, ne)].reshape(ne * blk, LANES)
            outs.append(jnp.dot(wexp, z, preferred_element_type=F32))
            issue_rows(nxt_iref, nxt_row0, nxt_buf, nxt_which, (tb + t) * seg, (tb + t + 1) * seg)
        o_ref[pl.ds(row0, tb), :] = x_ref[pl.ds(row0, tb), :] + jnp.stack(outs).reshape(tb, d_sub * LANES)

    for a in range(nbuf):
        wait_tile(bufs[a], a)
        nxt = a + dist
        if nxt < nbuf:
            tile(a * tb, bufs[a], idx_ref, nxt * tb, bufs[nxt], nxt)
        else:
            tile(a * tb, bufs[a], idxn_ref, (nxt - nbuf) * tb, bufs[nxt - nbuf], nxt - nbuf)

    @pl.when(i == n - 1)
    def _():
        for a in range(dist):
            wait_tile(bufs[a], a)


def peer_experts(idx, gates, h2d, x2d, uv3, *, tb):
    m, ne = idx.shape
    d_sub = h2d.shape[1] // LANES
    nbuf, dist = PEER_NBUF, PEER_DIST
    ts = nbuf * tb
    n = m // ts
    rows = tb * ne
    blk = 2 * d_sub
    cid = jnp.arange(ne * blk) // blk
    pool = (cid[:, None] == jnp.arange(ne)[None, :]).astype(BF16)
    tok = lambda i: (i, 0)
    return pl.pallas_call(
        functools.partial(_peer_expert_kernel, tb=tb, ne=ne, nbuf=nbuf, dist=dist),
        out_shape=jax.ShapeDtypeStruct(x2d.shape, F32),
        grid=(n,),
        in_specs=[
            pl.BlockSpec((ts, ne), tok, memory_space=pltpu.SMEM),
            pl.BlockSpec((ts, ne), lambda i: (jnp.minimum(i + 1, n - 1), 0), memory_space=pltpu.SMEM),
            pl.BlockSpec((ts, ne), tok),
            pl.BlockSpec((ts, d_sub * LANES), tok),
            pl.BlockSpec((ts, d_sub * LANES), tok),
            pl.BlockSpec((ne * blk, ne), lambda i: (0, 0)),
            pl.BlockSpec((ne, ne * blk), lambda i: (0, 0)),
            pl.BlockSpec(memory_space=pl.ANY),
        ],
        out_specs=pl.BlockSpec((ts, d_sub * LANES), tok),
        scratch_shapes=[pltpu.VMEM((rows, blk, LANES), BF16) for _ in range(nbuf)]
        + [pltpu.SemaphoreType.DMA((nbuf, 2))],
        compiler_params=_cparams(("arbitrary",)),
        name="peer_experts",
    )(idx, idx, gates, h2d, x2d, pool, pool.T, uv3)


def kernel(x, mem, positions, attn_norm_g, mem_norm_g, w_in, b_gate, diff_q_norm_g, diff_k_norm_g, lambda_q1, lambda_k1, lambda_q2, lambda_k2, diff_subln_g, rel_bias_table, conv_w, a_log, dt_bias, delta_out_norm_g, w_mem_kv, mem_q_norm_g, mem_k_norm_g, w_br_diff, w_br_delta, w_br_mem, w_out, ffn_norm_g, w_query, sub_keys, expert_u, expert_v):
    del positions
    b, t, d = x.shape
    mtok = mem.shape[1]
    m = b * t
    depth = w_in.shape[0]
    nh = N_DIFF_HEADS
    qk_w = nh * 2 * DIFF_HEAD_DIM
    dv_w = nh * 2 * DIFF_HEAD_DIM
    dl_w = N_DELTA_HEADS * DELTA_HEAD_DIM
    mem_w = N_MEM_HEADS * MEM_HEAD_DIM
    tq = min(ATTN_TQ, t)
    tt_delta = min(DELTA_TT, t)

    x2d = x.reshape(m, d)
    for l in range(depth):
        lam_init = 0.8 - 0.6 * math.exp(-0.3 * l)
        wl = w_in[l]
        o = 0
        w_qk = wl[:, o:o + 2 * qk_w]; o += 2 * qk_w
        w_dv = wl[:, o:o + dv_w]; o += dv_w
        w_lqkv = wl[:, o:o + 3 * dl_w]; o += 3 * dl_w
        w_lz = wl[:, o:o + dl_w]; o += dl_w
        w_ab = wl[:, o:o + 2 * N_DELTA_HEADS]; o += 2 * N_DELTA_HEADS
        w_mq = wl[:, o:o + mem_w]; o += mem_w
        w_gate = wl[:, o:o + 3 * d]

        qk_gain = jnp.concatenate([jnp.tile(diff_q_norm_g[l], 2 * nh) * (DIFF_HEAD_DIM ** -0.5 * LOG2E),
                                   jnp.tile(diff_k_norm_g[l], 2 * nh)])
        qk = norm_matmul(x2d, w_qk.astype(BF16), norm_g=attn_norm_g[l], group=DIFF_HEAD_DIM, group_gain=qk_gain)
        w_rest = jnp.concatenate([w_dv, w_lqkv, w_lz, w_gate], axis=1).astype(BF16)
        rest = norm_matmul(x2d, w_rest, norm_g=attn_norm_g[l])
        mq_gain = jnp.tile(mem_q_norm_g[l], N_MEM_HEADS) * (MEM_HEAD_DIM ** -0.5)
        mq = norm_matmul(x2d, w_mq.astype(BF16), norm_g=attn_norm_g[l], group=MEM_HEAD_DIM, group_gain=mq_gain)
        w_ab_pad = jnp.pad(w_ab, ((0, 0), (0, LANES - 2 * N_DELTA_HEADS))).astype(BF16)
        ab = norm_matmul(x2d, w_ab_pad, norm_g=attn_norm_g[l], out_dtype=F32, tn=LANES)
        abt = ab[:, :2 * N_DELTA_HEADS].T

        lam = (jnp.exp(jnp.sum(lambda_q1[l].astype(F32) * lambda_k1[l].astype(F32)))
               - jnp.exp(jnp.sum(lambda_q2[l].astype(F32) * lambda_k2[l].astype(F32))) + lam_init)
        bias_tiles = rel_bias_tiles(rel_bias_table, tq)
        scal = jnp.concatenate([jnp.stack([lam, jnp.asarray(1.0 - lam_init, F32)]),
                                rel_bias_table[REL_BUCKETS - 1].astype(F32) * LOG2E])
        vt = rest[:, :dv_w].reshape(b, t // tq, tq, nh, LANES).transpose(0, 3, 1, 4, 2)
        y_diff = diff_attention(qk, vt, bias_tiles, scal, diff_subln_g[l], b=b, t=t, tq=tq, hpb=ATTN_HEADS_PER_STEP)

        head_params = jnp.stack([a_log[l], dt_bias[l]]).astype(F32)
        y_delta = gated_deltanet(rest, ab, abt, conv_w[l].astype(F32), head_params, delta_out_norm_g[l],
                                 b=b, t=t, tt=tt_delta, nhb=DELTA_HEADS_PER_STEP, qkv_col0=dv_w // LANES,
                                 z_col0=(dv_w + 3 * dl_w) // LANES)

        mem2d = mem.reshape(b * mtok, d)
        wkv = w_mem_kv[l].astype(BF16)
        mk = norm_matmul(mem2d, wkv[:, :mem_w], norm_g=mem_norm_g[l], group=MEM_HEAD_DIM,
                         group_gain=jnp.tile(mem_k_norm_g[l], N_MEM_HEADS))
        mv = norm_matmul(mem2d, wkv[:, mem_w:], norm_g=mem_norm_g[l])
        y_mem = memory_attention(mq, mk, mv, b=b, t=t, mtok=mtok, tq=min(MEM_ATTN_TQ, t))

        x1, h2 = merge_project(x2d, y_diff, y_delta, y_mem, rest, b_gate[l],
                               w_br_diff[l].astype(BF16), w_br_delta[l].astype(BF16), w_br_mem[l].astype(BF16),
                               w_out[l].astype(BF16), ffn_norm_g[l],
                               gate_col0=dv_w + 4 * dl_w, tm=min(MERGE_TM, m))

        qry = norm_matmul(h2, w_query[l].astype(BF16))
        idx, gates = peer_route(qry, sub_keys[l].astype(BF16), tt=min(ROUTE_TT, m))
        sub = d // LANES
        uv3 = jnp.concatenate([expert_u[l].reshape(-1, sub, LANES), expert_v[l].reshape(-1, sub, LANES)],
                              axis=1).astype(BF16)
        x2d = peer_experts(idx, gates, h2, x1, uv3, tb=PEER_TB)
    return x2d.reshape(b, t, d)
```

```python
import functools
import math

import jax
import jax.numpy as jnp
from jax import lax
from jax.experimental import pallas as pl
from jax.experimental.pallas import tpu as pltpu

F32 = jnp.float32
BF16 = jnp.bfloat16
EPS = 1e-6
NEG = -1e30
LOG2E = math.log2(math.e)

N_DIFF_HEADS = 8
DIFF_HEAD_DIM = 64
N_DELTA_HEADS = 8
DELTA_HEAD_DIM = 128
CONV_WIDTH = 4
CHUNK = 64
N_MEM_HEADS = 4
MEM_HEAD_DIM = 256
REL_BUCKETS = 32
REL_MAX_DIST = 128
PEER_HEADS = 8
PEER_KEYS = 128
PEER_TOPK = 16
PEER_HALF = 128
LANES = 128
SUBLANES = 8
VMEM_LIMIT = 56 * 1024 * 1024

MATMUL_TM = 1024
MATMUL_TN = 2048
MATMUL_TN_GROUP = 512
ATTN_TQ = 256
ATTN_HEADS_PER_STEP = 8
DELTA_TT = 256
DELTA_HEADS_PER_STEP = 8
MEM_ATTN_TQ = 512
MERGE_TM = 512
ROUTE_TT = 1024
PEER_TB = 8
PEER_NBUF = 4
PEER_DIST = 3


def _cparams(sem):
    return pltpu.CompilerParams(dimension_semantics=sem, vmem_limit_bytes=VMEM_LIMIT)


def _norm_matmul_kernel(*refs, has_norm, has_group):
    it = iter(refs)
    x_ref = next(it)
    g_ref = next(it) if has_norm else None
    w_ref = next(it)
    gm_ref = next(it) if has_group else None
    gain_ref = next(it) if has_group else None
    o_ref = next(it)
    h_ref = next(it)

    @pl.when(pl.program_id(1) == 0)
    def _():
        x = x_ref[...].astype(F32)
        if has_norm:
            ms = jnp.mean(x * x, axis=-1, keepdims=True)
            x = x * lax.rsqrt(ms + EPS) * g_ref[...]
        h_ref[...] = x.astype(BF16)

    y = jnp.dot(h_ref[...], w_ref[...], preferred_element_type=F32)
    if has_group:
        ms = jnp.dot((y * y).astype(BF16), gm_ref[...], preferred_element_type=F32)
        y = y * lax.rsqrt(ms + EPS) * gain_ref[...]
    o_ref[...] = y.astype(o_ref.dtype)


def norm_matmul(x, w, *, norm_g=None, group=None, group_gain=None, out_dtype=BF16, tm=MATMUL_TM, tn=None):
    m, k = x.shape
    n = w.shape[1]
    has_norm = norm_g is not None
    has_group = group is not None
    if tn is None:
        tn = MATMUL_TN_GROUP if has_group else min(n, MATMUL_TN)
    tm = min(tm, m)
    assert m % tm == 0 and n % tn == 0
    in_specs = [pl.BlockSpec((tm, k), lambda i, j: (i, 0))]
    args = [x]
    if has_norm:
        in_specs.append(pl.BlockSpec((1, k), lambda i, j: (0, 0)))
        args.append(norm_g.reshape(1, k).astype(F32))
    in_specs.append(pl.BlockSpec((k, tn), lambda i, j: (0, j)))
    args.append(w)
    if has_group:
        gid = jnp.arange(tn) // group
        gm = jnp.where(gid[:, None] == gid[None, :], 1.0 / group, 0.0).astype(BF16)
        in_specs.append(pl.BlockSpec((tn, tn), lambda i, j: (0, 0)))
        args.append(gm)
        in_specs.append(pl.BlockSpec((1, tn), lambda i, j: (0, j)))
        args.append(group_gain.reshape(1, n).astype(F32))
    return pl.pallas_call(
        functools.partial(_norm_matmul_kernel, has_norm=has_norm, has_group=has_group),
        out_shape=jax.ShapeDtypeStruct((m, n), out_dtype),
        grid=(m // tm, n // tn),
        in_specs=in_specs,
        out_specs=pl.BlockSpec((tm, tn), lambda i, j: (i, j)),
        scratch_shapes=[pltpu.VMEM((tm, k), BF16)],
        compiler_params=_cparams(("parallel", "arbitrary")),
        name="norm_matmul",
    )(*args)


def _t5_bucket(n):
    max_exact = REL_BUCKETS // 2
    nf = jnp.maximum(n, 1).astype(F32)
    large = max_exact + (jnp.log(nf / max_exact) / math.log(REL_MAX_DIST / max_exact)
                         * (REL_BUCKETS - max_exact)).astype(jnp.int32)
    large = jnp.minimum(large, REL_BUCKETS - 1)
    return jnp.where(n < max_exact, n, large)


def _rel_bias_kernel(table_ref, o_ref, *, tq):
    hm = pl.program_id(0)
    c = lax.broadcasted_iota(jnp.int32, (2 * tq, tq), 0)
    r = lax.broadcasted_iota(jnp.int32, (2 * tq, tq), 1)
    bucket = _t5_bucket(jnp.maximum(r - c + tq, 0))
    acc = jnp.zeros((2 * tq, tq), F32)
    for b in range(REL_BUCKETS):
        acc = jnp.where(bucket == b, table_ref[b, hm], acc)
    o_ref[0] = acc * LOG2E


def rel_bias_tiles(rel_table, tq):
    nmaps = rel_table.shape[1]
    return pl.pallas_call(
        functools.partial(_rel_bias_kernel, tq=tq),
        out_shape=jax.ShapeDtypeStruct((nmaps, 2 * tq, tq), F32),
        grid=(nmaps,),
        in_specs=[pl.BlockSpec(memory_space=pltpu.SMEM)],
        out_specs=pl.BlockSpec((1, 2 * tq, tq), lambda i: (i, 0, 0)),
        compiler_params=_cparams(("arbitrary",)),
        name="rel_bias_tiles",
    )(rel_table.astype(F32))


def _diff_attn_kernel(scal_ref, q_ref, k_ref, vt_ref, bias_ref, subg_ref, o_ref, *, tq, hpb):
    hb = pl.program_id(1)
    i = pl.program_id(2)
    heads = range(hpb)
    lam = scal_ref[0]
    out_scale = scal_ref[1]
    w2 = 2 * tq

    sub = lax.broadcasted_iota(jnp.int32, (LANES, tq), 0)
    col2 = lax.broadcasted_iota(jnp.int32, (1, w2), 1)
    qt, cfar = [], []
    for hd in heads:
        q = q_ref[:, hd * LANES:(hd + 1) * LANES].astype(F32).T
        qt.append(jnp.concatenate([jnp.where(sub < DIFF_HEAD_DIM, q, 0.0),
                                   jnp.where(sub >= DIFF_HEAD_DIM, q, 0.0)], axis=1).astype(BF16))
        h = hb * hpb + hd
        cfar.append(jnp.where(col2 < tq, scal_ref[2 + 2 * h], scal_ref[3 + 2 * h]))

    def update(carry, s, vt, shift=None):
        m, l, acc = carry
        colmax = [jnp.max(x, axis=0, keepdims=True) for x in s]
        if shift is not None:
            colmax = [a + c for a, c in zip(colmax, shift)]
        m_new = [jnp.maximum(a, b_) for a, b_ in zip(m, colmax)]
        alpha = [jnp.exp2(a - b_) for a, b_ in zip(m, m_new)]
        sub_ = m_new if shift is None else [a - c for a, c in zip(m_new, shift)]
        p = [jnp.exp2(x - a) for x, a in zip(s, sub_)]
        l = [a * b_ + jnp.sum(x, axis=0, keepdims=True) for a, b_, x in zip(l, alpha, p)]
        acc = [a * b_ + jnp.dot(v_, x.astype(BF16), preferred_element_type=F32)
               for a, b_, x, v_ in zip(acc, alpha, p, vt)]
        return m_new, l, acc

    def scores(tile0, ntiles):
        rows = pl.ds(pl.multiple_of(tile0 * tq, tq), ntiles * tq)
        s = [jnp.dot(k_ref[rows, hd * LANES:(hd + 1) * LANES], qt[hd], preferred_element_type=F32) for hd in heads]
        vt = [jnp.concatenate([vt_ref[0, hd, tile0 + c] for c in range(ntiles)], axis=1) if ntiles > 1
              else vt_ref[0, hd, tile0] for hd in heads]
        return s, vt

    def pack(carry):
        m, l, acc = carry
        return tuple(m) + tuple(l) + tuple(acc)

    def unpack(flat):
        return list(flat[0:hpb]), list(flat[hpb:2 * hpb]), list(flat[2 * hpb:3 * hpb])

    def far_step(j, flat):
        s, vt = scores(2 * j, 2)
        return pack(update(unpack(flat), s, vt, cfar))

    def far_single(flat):
        s, vt = scores(i - 2, 1)
        return pack(update(unpack(flat), s, vt, cfar))

    n_far = jnp.maximum(i - 1, 0)
    init = ([jnp.full((1, w2), NEG, F32) for _ in heads], [jnp.zeros((1, w2), F32) for _ in heads],
            [jnp.zeros((LANES, w2), F32) for _ in heads])
    flat = lax.fori_loop(0, n_far // 2, far_step, pack(init))
    flat = lax.cond(n_far % 2 == 1, far_single, lambda c: c, flat)
    carry = unpack(flat)

    def near_bias(hd, lo):
        return jnp.concatenate([bias_ref[2 * hd, lo:lo + tq, :], bias_ref[2 * hd + 1, lo:lo + tq, :]], axis=1)

    s, vt = scores(jnp.maximum(i - 1, 0), 1)
    s = [jnp.where(i >= 1, x + near_bias(hd, 0), NEG) for hd, x in zip(heads, s)]
    carry = update(carry, s, vt)

    s, vt = scores(i, 1)
    kk = lax.broadcasted_iota(jnp.int32, (tq, w2), 0)
    qi = lax.broadcasted_iota(jnp.int32, (tq, w2), 1)
    qi = jnp.where(qi >= tq, qi - tq, qi)
    s = [jnp.where(kk <= qi, x + near_bias(hd, tq), NEG) for hd, x in zip(heads, s)]
    m, l, acc = update(carry, s, vt)

    outs = []
    for hd in heads:
        o = acc[hd] / l[hd]
        o = o[:, 0:tq] - lam * o[:, tq:w2]
        ms = jnp.mean(o * o, axis=0, keepdims=True)
        o = o * lax.rsqrt(ms + EPS) * subg_ref[...] * out_scale
        outs.append(o.T)
    o_ref[...] = jnp.concatenate(outs, axis=-1).astype(o_ref.dtype)


def diff_attention(qk, vt, bias_tiles, scal, subln_g, *, b, t, tq, hpb):
    m = b * t
    nq = t // tq
    nh = N_DIFF_HEADS
    wl = hpb * LANES
    assert nh % hpb == 0
    return pl.pallas_call(
        functools.partial(_diff_attn_kernel, tq=tq, hpb=hpb),
        out_shape=jax.ShapeDtypeStruct((m, nh * LANES), BF16),
        grid=(b, nh // hpb, nq),
        in_specs=[
            pl.BlockSpec(memory_space=pltpu.SMEM),
            pl.BlockSpec((tq, wl), lambda bi, h, i: (bi * nq + i, h)),
            pl.BlockSpec((t, wl), lambda bi, h, i: (bi, nh // hpb + h), pipeline_mode=pl.Buffered(1)),
            pl.BlockSpec((1, hpb, nq, LANES, tq), lambda bi, h, i: (bi, h, 0, 0, 0), pipeline_mode=pl.Buffered(1)),
            pl.BlockSpec((2 * hpb, 2 * tq, tq), lambda bi, h, i: (h, 0, 0), pipeline_mode=pl.Buffered(1)),
            pl.BlockSpec((LANES, 1), lambda bi, h, i: (0, 0)),
        ],
        out_specs=pl.BlockSpec((tq, wl), lambda bi, h, i: (bi * nq + i, h)),
        compiler_params=_cparams(("parallel", "parallel", "arbitrary")),
        name="diff_attention",
    )(scal, qk, qk, vt, bias_tiles, subln_g.reshape(LANES, 1).astype(F32))


def _sigmoid(x):
    return 1.0 / (1.0 + jnp.exp(-x))


def _softplus(x):
    return jnp.maximum(x, 0.0) + jnp.log(1.0 + jnp.exp(-jnp.abs(x)))


def _delta_kernel(hp_ref, xq_ref, xk_ref, xv_ref, pq_ref, pk_ref, pv_ref, wq_ref, wk_ref, wv_ref,
                  ab_ref, abt_ref, z_ref, og_ref, o_ref, s_ref, *, tt, nhb):
    hb = pl.program_id(1)
    i = pl.program_id(2)
    nc = tt // CHUNK
    dh = DELTA_HEAD_DIM
    heads = range(nhb)

    @pl.when(i == 0)
    def _():
        s_ref[...] = jnp.zeros_like(s_ref)

    def conv_silu(x_ref, p_ref, w_ref):
        prev = jnp.where(i > 0, p_ref[...].astype(F32), 0.0)
        xf = jnp.concatenate([prev, x_ref[...].astype(F32)], axis=0)
        w = w_ref[...]
        base = SUBLANES - (CONV_WIDTH - 1)
        y = xf[base:base + tt] * w[0:1]
        for c in range(1, CONV_WIDTH):
            y = y + xf[base + c:base + c + tt] * w[c:c + 1]
        return y * _sigmoid(y)

    def split(x):
        return [x[:, hd * dh:(hd + 1) * dh] for hd in heads]

    q = split(conv_silu(xq_ref, pq_ref, wq_ref))
    k = split(conv_silu(xk_ref, pk_ref, wk_ref))
    v = split(conv_silu(xv_ref, pv_ref, wv_ref))
    q = [x * lax.rsqrt(jnp.sum(x * x, axis=-1, keepdims=True) + EPS) * (dh ** -0.5) for x in q]
    k = [x * lax.rsqrt(jnp.sum(x * x, axis=-1, keepdims=True) + EPS) for x in k]

    ab = ab_ref[...]
    lane = lax.broadcasted_iota(jnp.int32, ab.shape, 1)
    beta_col, g_col, g_row = [], [], []
    for hd in heads:
        h = hb * nhb + hd
        neg_a = -jnp.exp(jnp.full((1, 1), hp_ref[0, h], F32))
        dt_bias = hp_ref[1, h]
        lb_col = jnp.sum(jnp.where(lane == h, ab, 0.0), axis=-1, keepdims=True)
        la_col = jnp.sum(jnp.where(lane == N_DELTA_HEADS + h, ab, 0.0), axis=-1, keepdims=True)
        la_row = abt_ref[pl.ds(N_DELTA_HEADS + h, 1), :]
        beta_col.append(_sigmoid(lb_col))
        g_col.append(neg_a * _softplus(la_col + dt_bias))
        g_row.append(neg_a * _softplus(la_row + dt_bias))

    ri = lax.broadcasted_iota(jnp.int32, (tt, tt), 0)
    ci = lax.broadcasted_iota(jnp.int32, (tt, tt), 1)
    same = (ri // CHUNK) == (ci // CHUNK)
    tril = same & (ri >= ci)
    strict = same & (ri > ci)
    triu = same & (ri <= ci)

    def mm(a, b):
        return jnp.dot(a.astype(BF16), b.astype(BF16), preferred_element_type=F32)

    def mm_nt(a, b):
        return lax.dot_general(a.astype(BF16), b.astype(BF16), (((1,), (1,)), ((), ())),
                               preferred_element_type=F32)

    def mm_tn(a, b):
        return lax.dot_general(a.astype(BF16), b.astype(BF16), (((0,), (0,)), ((), ())),
                               preferred_element_type=F32)

    gc_col = [jnp.sum(jnp.where(tril, g, 0.0), axis=-1, keepdims=True) for g in g_row]
    gc_row = [jnp.sum(jnp.where(triu, g, 0.0), axis=0, keepdims=True) for g in g_col]
    decay = [jnp.where(tril, jnp.exp(jnp.where(tril, a - b, 0.0)), 0.0) for a, b in zip(gc_col, gc_row)]
    e_gc = [jnp.exp(g) for g in gc_col]
    kk = [mm_nt(x, x) for x in k]
    qk = [mm_nt(a, b) for a, b in zip(q, k)]
    xm = [jnp.where(strict, -(b * kx * d), 0.0) for b, kx, d in zip(beta_col, kk, decay)]
    intra = [jnp.where(tril, a * d, 0.0) for a, d in zip(qk, decay)]
    r = [jnp.concatenate([b * vx, (b * e) * kx], axis=-1) for b, vx, e, kx in zip(beta_col, v, e_gc, k)]
    n_sq = CHUNK.bit_length() - 1
    for lvl in range(n_sq):
        r = [x + mm(m_, x) for m_, x in zip(xm, r)]
        if lvl + 1 < n_sq:
            xm = [mm(m_, m_) for m_ in xm]
    u = [x[:, 0:dh] for x in r]
    w = [x[:, dh:2 * dh] for x in r]
    qg = [a * e for a, e in zip(q, e_gc)]

    state = [s_ref[hd] for hd in heads]
    o_state = [[] for _ in heads]
    v_new = [[] for _ in heads]
    for c in range(nc):
        r0 = c * CHUNK
        sl = slice(r0, r0 + CHUNK)
        g_last = [g[r0 + CHUNK - 1:r0 + CHUNK] for g in gc_col]
        vn = [u[hd][sl] - mm(w[hd][sl], state[hd]) for hd in heads]
        os_ = [mm(qg[hd][sl], state[hd]) for hd in heads]
        k_dec = [k[hd][sl] * jnp.exp(g_last[hd] - gc_col[hd][sl]) for hd in heads]
        state = [state[hd] * jnp.exp(g_last[hd]) + mm_tn(k_dec[hd], vn[hd]) for hd in heads]
        for hd in heads:
            v_new[hd].append(vn[hd])
            o_state[hd].append(os_[hd])
    for hd in heads:
        s_ref[hd] = state[hd]

    outs = []
    for hd in heads:
        o = jnp.concatenate(o_state[hd], axis=0) + mm(intra[hd], jnp.concatenate(v_new[hd], axis=0))
        ms = jnp.mean(o * o, axis=-1, keepdims=True)
        outs.append(o * lax.rsqrt(ms + EPS) * og_ref[...])
    z = z_ref[...].astype(F32)
    o_ref[...] = (jnp.concatenate(outs, axis=-1) * (z * _sigmoid(z))).astype(o_ref.dtype)


def gated_deltanet(rest, ab, abt, conv_w, head_params, out_g, *, b, t, tt, nhb, qkv_col0, z_col0):
    m = b * t
    nt = t // tt
    nh = N_DELTA_HEADS
    sub = tt // SUBLANES
    wl = nhb * LANES
    assert nh % nhb == 0 and qkv_col0 % nhb == 0 and z_col0 % nhb == 0

    def cur(off):
        return pl.BlockSpec((tt, wl), lambda bi, h, i: (bi * nt + i, off // nhb + h))

    def prev(off):
        return pl.BlockSpec((SUBLANES, wl),
                            lambda bi, h, i: (jnp.maximum((bi * nt + i) * sub - 1, 0), off // nhb + h))

    def wspec(off):
        return pl.BlockSpec((CONV_WIDTH, wl), lambda bi, h, i: (0, off // nhb + h))

    return pl.pallas_call(
        functools.partial(_delta_kernel, tt=tt, nhb=nhb),
        out_shape=jax.ShapeDtypeStruct((m, nh * LANES), BF16),
        grid=(b, nh // nhb, nt),
        in_specs=[
            pl.BlockSpec(memory_space=pltpu.SMEM),
            cur(qkv_col0), cur(qkv_col0 + nh), cur(qkv_col0 + 2 * nh),
            prev(qkv_col0), prev(qkv_col0 + nh), prev(qkv_col0 + 2 * nh),
            wspec(0), wspec(nh), wspec(2 * nh),
            pl.BlockSpec((tt, LANES), lambda bi, h, i: (bi * nt + i, 0)),
            pl.BlockSpec((2 * nh, tt), lambda bi, h, i: (0, bi * nt + i)),
            cur(z_col0),
            pl.BlockSpec((1, LANES), lambda bi, h, i: (0, 0)),
        ],
        out_specs=pl.BlockSpec((tt, wl), lambda bi, h, i: (bi * nt + i, h)),
        scratch_shapes=[pltpu.VMEM((nhb, DELTA_HEAD_DIM, DELTA_HEAD_DIM), F32)],
        compiler_params=_cparams(("parallel", "parallel", "arbitrary")),
        name="gated_deltanet",
    )(head_params, rest, rest, rest, rest, rest, rest, conv_w, conv_w, conv_w, ab, abt, rest,
      out_g.reshape(1, LANES).astype(F32))


def _mem_attn_kernel(q_ref, k_ref, v_ref, o_ref):
    outs = []
    for h in range(N_MEM_HEADS):
        sl = slice(h * MEM_HEAD_DIM, (h + 1) * MEM_HEAD_DIM)
        s = lax.dot_general(q_ref[:, sl], k_ref[:, sl], (((1,), (1,)), ((), ())), preferred_element_type=F32)
        s = s - jnp.max(s, axis=-1, keepdims=True)
        p = jnp.exp(s)
        p = p / jnp.sum(p, axis=-1, keepdims=True)
        outs.append(jnp.dot(p.astype(BF16), v_ref[:, sl], preferred_element_type=F32))
    o_ref[...] = jnp.concatenate(outs, axis=-1).astype(o_ref.dtype)


def memory_attention(mq, mk, mv, *, b, t, mtok, tq):
    nq = t // tq
    w = N_MEM_HEADS * MEM_HEAD_DIM
    return pl.pallas_call(
        _mem_attn_kernel,
        out_shape=jax.ShapeDtypeStruct((b * t, w), BF16),
        grid=(b, nq),
        in_specs=[
            pl.BlockSpec((tq, w), lambda bi, i: (bi * nq + i, 0)),
            pl.BlockSpec((mtok, w), lambda bi, i: (bi, 0)),
            pl.BlockSpec((mtok, w), lambda bi, i: (bi, 0)),
        ],
        out_specs=pl.BlockSpec((tq, w), lambda bi, i: (bi * nq + i, 0)),
        compiler_params=_cparams(("parallel", "arbitrary")),
        name="memory_attention",
    )(mq, mk, mv)


def _merge_kernel(x_ref, ya_ref, yb_ref, yc_ref, ga_ref, gb_ref, gc_ref, bg_ref,
                  wa_ref, wb_ref, wc_ref, wo_ref, fg_ref, x1_ref, h2_ref):
    def branch(y_ref, g_ref, w_ref, idx):
        gate = _sigmoid(g_ref[...].astype(F32) + bg_ref[idx:idx + 1, :])
        return gate * jnp.dot(y_ref[...], w_ref[...], preferred_element_type=F32)

    merged = branch(ya_ref, ga_ref, wa_ref, 0) + branch(yb_ref, gb_ref, wb_ref, 1) + branch(yc_ref, gc_ref, wc_ref, 2)
    x1 = x_ref[...] + jnp.dot(merged.astype(BF16), wo_ref[...], preferred_element_type=F32)
    x1_ref[...] = x1
    ms = jnp.mean(x1 * x1, axis=-1, keepdims=True)
    h2_ref[...] = (x1 * lax.rsqrt(ms + EPS) * fg_ref[...]).astype(h2_ref.dtype)


def merge_project(x2d, y_diff, y_delta, y_mem, rest, b_gate, w_a, w_b, w_c, w_o, ffn_g, *, gate_col0, tm):
    m, d = x2d.shape
    row = lambda i: (i, 0)
    const = lambda i: (0, 0)
    assert gate_col0 % d == 0

    def gspec(j):
        return pl.BlockSpec((tm, d), lambda i: (i, gate_col0 // d + j))

    return pl.pallas_call(
        _merge_kernel,
        out_shape=(jax.ShapeDtypeStruct((m, d), F32), jax.ShapeDtypeStruct((m, d), BF16)),
        grid=(m // tm,),
        in_specs=[
            pl.BlockSpec((tm, d), row), pl.BlockSpec((tm, d), row), pl.BlockSpec((tm, d), row),
            pl.BlockSpec((tm, d), row), gspec(0), gspec(1), gspec(2),
            pl.BlockSpec((3, d), const),
            pl.BlockSpec((d, d), const), pl.BlockSpec((d, d), const), pl.BlockSpec((d, d), const),
            pl.BlockSpec((d, d), const), pl.BlockSpec((1, d), const),
        ],
        out_specs=(pl.BlockSpec((tm, d), row), pl.BlockSpec((tm, d), row)),
        compiler_params=_cparams(("parallel",)),
        name="merge_project",
    )(x2d, y_diff, y_delta, y_mem, rest, rest, rest, b_gate.reshape(3, d).astype(F32),
      w_a, w_b, w_c, w_o, ffn_g.reshape(1, d).astype(F32))


def _topk_rows(s, key, extra, k):
    big = jnp.iinfo(jnp.int32).max
    vals, keys, ext = [], [], []
    for _ in range(k):
        m = jnp.max(s, axis=0, keepdims=True)
        am = jnp.min(jnp.where(s == m, key, big), axis=0, keepdims=True)
        hit = key == am
        if extra is not None:
            ext.append(jnp.max(jnp.where(hit, extra, -1), axis=0, keepdims=True))
        s = jnp.where(hit, -jnp.inf, s)
        vals.append(m)
        keys.append(am)
    cat = lambda xs: jnp.concatenate(xs, axis=0)
    return cat(vals), cat(keys), (cat(ext) if extra is not None else None)


def _peer_route_kernel(q_ref, keys_ref, idx_ref, gate_ref, idx_acc, gate_acc):
    h = pl.program_id(1)
    k = PEER_TOPK
    q = q_ref[...]
    tt = q.shape[0]

    def half(p):
        qp = q[:, p * PEER_HALF:(p + 1) * PEER_HALF]
        st = lax.dot_general(keys_ref[0, p], qp, (((1,), (1,)), ((), ())), preferred_element_type=F32)
        v, r, _ = _topk_rows(st, lax.broadcasted_iota(jnp.int32, st.shape, 0), None, k)
        return v, r

    s1, i1 = half(0)
    s2, i2 = half(1)
    g = SUBLANES
    sub = lax.broadcasted_iota(jnp.int32, (g, tt), 0)
    e1 = i1 * PEER_KEYS
    pieces = []
    for b in range(g):
        lim = min(g, k // (b + 1))
        sc = s1[0:g] + s2[b:b + 1]
        if lim < g:
            sc = jnp.where(sub < lim, sc, -jnp.inf)
        pieces.append((sc, sub * k + b, e1[0:g] + i2[b:b + 1]))
    pieces.append((s1[g:k] + s2[0:1], (sub + g) * k, e1[g:k] + i2[0:1]))
    pieces.append((s1[0:1] + s2[g:k], sub + g, e1[0:1] + i2[g:k]))
    cand_s = jnp.concatenate([p[0] for p in pieces], axis=0)
    cand_k = jnp.concatenate([p[1] for p in pieces], axis=0)
    cand_e = jnp.concatenate([p[2] for p in pieces], axis=0)
    top_s, _, top_e = _topk_rows(cand_s, cand_k, cand_e, k)
    e = jnp.exp(top_s - top_s[0:1])
    rows = pl.ds(pl.multiple_of(h * k, k), k)
    gate_acc[rows, :] = e / jnp.sum(e, axis=0, keepdims=True)
    idx_acc[rows, :] = top_e.astype(F32)

    @pl.when(h == pl.num_programs(1) - 1)
    def _():
        gate_ref[...] = gate_acc[...].T
        idx_ref[...] = idx_acc[...].T.astype(jnp.int32)


def peer_route(qry, sub_keys, *, tt):
    m = qry.shape[0]
    nh = PEER_HEADS
    ne = nh * PEER_TOPK
    return pl.pallas_call(
        _peer_route_kernel,
        out_shape=(jax.ShapeDtypeStruct((m, ne), jnp.int32), jax.ShapeDtypeStruct((m, ne), F32)),
        grid=(m // tt, nh),
        in_specs=[
            pl.BlockSpec((tt, 2 * PEER_HALF), lambda i, h: (i, h)),
            pl.BlockSpec((1, 2, PEER_KEYS, PEER_HALF), lambda i, h: (h, 0, 0, 0)),
        ],
        out_specs=(pl.BlockSpec((tt, ne), lambda i, h: (i, 0)), pl.BlockSpec((tt, ne), lambda i, h: (i, 0))),
        scratch_shapes=[pltpu.VMEM((ne, tt), F32), pltpu.VMEM((ne, tt), F32)],
        compiler_params=_cparams(("parallel", "arbitrary")),
        name="peer_route",
    )(qry, sub_keys)


def _gelu_exact(x):
    return 0.5 * x * (1.0 + lax.erf(x * (2.0 ** -0.5)))


def _peer_expert_kernel(idx_ref, idxn_ref, gate_ref, h_ref, x_ref, pool_ref, poolt_ref, uv_hbm,
                        o_ref, *scratch, tb, ne, nbuf, dist):
    bufs, sem = scratch[:nbuf], scratch[nbuf]
    i = pl.program_id(0)
    n = pl.num_programs(0)
    rows = tb * ne
    d_sub = h_ref.shape[1] // LANES
    n_seg = 2 * tb
    seg = rows // n_seg

    def issue_rows(iref, row0, buf, which, r_lo, r_hi):
        for r in range(r_lo, r_hi):
            e = iref[row0 + r // ne, r % ne]
            pltpu.make_async_copy(uv_hbm.at[e], buf.at[r], sem.at[which, r % 2]).start(priority=r % 2)

    def wait_tile(buf, which):
        for k in range(2):
            pltpu.make_async_copy(uv_hbm.at[pl.ds(0, rows // 2)], buf.at[pl.ds(0, rows // 2)],
                                  sem.at[which, k]).wait()

    @pl.when(i == 0)
    def _():
        for a in range(dist):
            def tok(t, carry, a=a):
                for j in range(ne):
                    pltpu.make_async_copy(uv_hbm.at[idx_ref[a * tb + t, j]], bufs[a].at[t * ne + j],
                                          sem.at[a, j % 2]).start(priority=j % 2)
                return carry
            lax.fori_loop(0, tb, tok, 0)

    blk = 2 * d_sub
    sub_id = lax.broadcasted_iota(jnp.int32, (d_sub, ne * blk), 0)
    col_id = lax.broadcasted_iota(jnp.int32, (d_sub, ne * blk), 1)
    diag_u = (col_id % blk) == sub_id
    diag_v = (col_id % blk) == sub_id + d_sub

    def tile(row0, buf, nxt_iref, nxt_row0, nxt_buf, nxt_which):
        drows = []
        h8 = h_ref[pl.ds(row0, tb), :].astype(F32).reshape(tb, d_sub, LANES).astype(BF16)
        for t in range(tb):
            z = buf[pl.ds(t * ne, ne)].reshape(ne * blk, LANES)
            c = lax.dot_general(h8[t], z, (((1,), (1,)), ((), ())), preferred_element_type=F32)
            drows.append(jnp.sum(jnp.where(diag_u, c, 0.0), axis=0, keepdims=True))
            issue_rows(nxt_iref, nxt_row0, nxt_buf, nxt_which, t * seg, (t + 1) * seg)
        dall = jnp.concatenate(drows, axis=0)
        d_hi = dall.astype(BF16)
        d_lo = (dall - d_hi.astype(F32)).astype(BF16)
        a = (jnp.dot(d_hi, pool_ref[...], preferred_element_type=F32)
             + jnp.dot(d_lo, pool_ref[...], preferred_element_type=F32))
        w = gate_ref[pl.ds(row0, tb), :] * _gelu_exact(a)
        wrep = jnp.dot(w.astype(BF16), poolt_ref[...], preferred_element_type=F32)
        outs = []
        for t in range(tb):
            wexp = jnp.where(diag_v, wrep[t:t + 1, :], 0.0).astype(BF16)
            z = buf[pl.ds(t * ne, ne)].reshape(ne * blk, LANES)
            outs.append(jnp.dot(wexp, z, preferred_element_type=F32))
            issue_rows(nxt_iref, nxt_row0, nxt_buf, nxt_which, (tb + t) * seg, (tb + t + 1) * seg)
        o_ref[pl.ds(row0, tb), :] = x_ref[pl.ds(row0, tb), :] + jnp.stack(outs).reshape(tb, d_sub * LANES)

    for a in range(nbuf):
        wait_tile(bufs[a], a)
        nxt = a + dist
        if nxt < nbuf:
            tile(a * tb, bufs[a], idx_ref, nxt * tb, bufs[nxt], nxt)
        else:
            tile(a * tb, bufs[a], idxn_ref, (nxt - nbuf) * tb, bufs[nxt - nbuf], nxt - nbuf)

    @pl.when(i == n - 1)
    def _():
        for a in range(dist):
            wait_tile(bufs[a], a)


def peer_experts(idx, gates, h2d, x2d, uv3, *, tb):
    m, ne = idx.shape
    d_sub = h2d.shape[1] // LANES
    nbuf, dist = PEER_NBUF, PEER_DIST
    ts = nbuf * tb
    n = m // ts
    rows = tb * ne
    blk = 2 * d_sub
    cid = jnp.arange(ne * blk) // blk
    pool = (cid[:, None] == jnp.arange(ne)[None, :]).astype(BF16)
    tok = lambda i: (i, 0)
    return pl.pallas_call(
        functools.partial(_peer_expert_kernel, tb=tb, ne=ne, nbuf=nbuf, dist=dist),
        out_shape=jax.ShapeDtypeStruct(x2d.shape, F32),
        grid=(n,),
        in_specs=[
            pl.BlockSpec((ts, ne), tok, memory_space=pltpu.SMEM),
            pl.BlockSpec((ts, ne), lambda i: (jnp.minimum(i + 1, n - 1), 0), memory_space=pltpu.SMEM),
            pl.BlockSpec((ts, ne), tok),
            pl.BlockSpec((ts, d_sub * LANES), tok),
            pl.BlockSpec((ts, d_sub * LANES), tok),
            pl.BlockSpec((ne * blk, ne), lambda i: (0, 0)),
            pl.BlockSpec((ne, ne * blk), lambda i: (0, 0)),
            pl.BlockSpec(memory_space=pl.ANY),
        ],
        out_specs=pl.BlockSpec((ts, d_sub * LANES), tok),
        scratch_shapes=[pltpu.VMEM((rows, blk, LANES), BF16) for _ in range(nbuf)]
        + [pltpu.SemaphoreType.DMA((nbuf, 2))],
        compiler_params=_cparams(("arbitrary",)),
        name="peer_experts",
    )(idx, idx, gates, h2d, x2d, pool, pool.T, uv3)


def kernel(x, mem, positions, attn_norm_g, mem_norm_g, w_in, b_gate, diff_q_norm_g, diff_k_norm_g, lambda_q1, lambda_k1, lambda_q2, lambda_k2, diff_subln_g, rel_bias_table, conv_w, a_log, dt_bias, delta_out_norm_g, w_mem_kv, mem_q_norm_g, mem_k_norm_g, w_br_diff, w_br_delta, w_br_mem, w_out, ffn_norm_g, w_query, sub_keys, expert_u, expert_v):
    del positions
    b, t, d = x.shape
    mtok = mem.shape[1]
    m = b * t
    depth = w_in.shape[0]
    nh = N_DIFF_HEADS
    qk_w = nh * 2 * DIFF_HEAD_DIM
    dv_w = nh * 2 * DIFF_HEAD_DIM
    dl_w = N_DELTA_HEADS * DELTA_HEAD_DIM
    mem_w = N_MEM_HEADS * MEM_HEAD_DIM
    tq = min(ATTN_TQ, t)
    tt_delta = min(DELTA_TT, t)

    x2d = x.reshape(m, d)
    for l in range(depth):
        lam_init = 0.8 - 0.6 * math.exp(-0.3 * l)
        wl = w_in[l]
        o = 0
        w_qk = wl[:, o:o + 2 * qk_w]; o += 2 * qk_w
        w_dv = wl[:, o:o + dv_w]; o += dv_w
        w_lqkv = wl[:, o:o + 3 * dl_w]; o += 3 * dl_w
        w_lz = wl[:, o:o + dl_w]; o += dl_w
        w_ab = wl[:, o:o + 2 * N_DELTA_HEADS]; o += 2 * N_DELTA_HEADS
        w_mq = wl[:, o:o + mem_w]; o += mem_w
        w_gate = wl[:, o:o + 3 * d]

        qk_gain = jnp.concatenate([jnp.tile(diff_q_norm_g[l], 2 * nh) * (DIFF_HEAD_DIM ** -0.5 * LOG2E),
                                   jnp.tile(diff_k_norm_g[l], 2 * nh)])
        qk = norm_matmul(x2d, w_qk.astype(BF16), norm_g=attn_norm_g[l], group=DIFF_HEAD_DIM, group_gain=qk_gain)
        w_rest = jnp.concatenate([w_dv, w_lqkv, w_lz, w_gate], axis=1).astype(BF16)
        rest = norm_matmul(x2d, w_rest, norm_g=attn_norm_g[l])
        mq_gain = jnp.tile(mem_q_norm_g[l], N_MEM_HEADS) * (MEM_HEAD_DIM ** -0.5)
        mq = norm_matmul(x2d, w_mq.astype(BF16), norm_g=attn_norm_g[l], group=MEM_HEAD_DIM, group_gain=mq_gain)
        w_ab_pad = jnp.pad(w_ab, ((0, 0), (0, LANES - 2 * N_DELTA_HEADS))).astype(BF16)
        ab = norm_matmul(x2d, w_ab_pad, norm_g=attn_norm_g[l], out_dtype=F32, tn=LANES)
        abt = ab[:, :2 * N_DELTA_HEADS].T

        lam = (jnp.exp(jnp.sum(lambda_q1[l].astype(F32) * lambda_k1[l].astype(F32)))
               - jnp.exp(jnp.sum(lambda_q2[l].astype(F32) * lambda_k2[l].astype(F32))) + lam_init)
        bias_tiles = rel_bias_tiles(rel_bias_table, tq)
        scal = jnp.concatenate([jnp.stack([lam, jnp.asarray(1.0 - lam_init, F32)]),
                                rel_bias_table[REL_BUCKETS - 1].astype(F32) * LOG2E])
        vt = rest[:, :dv_w].reshape(b, t // tq, tq, nh, LANES).transpose(0, 3, 1, 4, 2)
        y_diff = diff_attention(qk, vt, bias_tiles, scal, diff_subln_g[l], b=b, t=t, tq=tq, hpb=ATTN_HEADS_PER_STEP)

        head_params = jnp.stack([a_log[l], dt_bias[l]]).astype(F32)
        y_delta = gated_deltanet(rest, ab, abt, conv_w[l].astype(F32), head_params, delta_out_norm_g[l],
                                 b=b, t=t, tt=tt_delta, nhb=DELTA_HEADS_PER_STEP, qkv_col0=dv_w // LANES,
                                 z_col0=(dv_w + 3 * dl_w) // LANES)

        mem2d = mem.reshape(b * mtok, d)
        wkv = w_mem_kv[l].astype(BF16)
        mk = norm_matmul(mem2d, wkv[:, :mem_w], norm_g=mem_norm_g[l], group=MEM_HEAD_DIM,
                         group_gain=jnp.tile(mem_k_norm_g[l], N_MEM_HEADS))
        mv = norm_matmul(mem2d, wkv[:, mem_w:], norm_g=mem_norm_g[l])
        y_mem = memory_attention(mq, mk, mv, b=b, t=t, mtok=mtok, tq=min(MEM_ATTN_TQ, t))

        x1, h2 = merge_project(x2d, y_diff, y_delta, y_mem, rest, b_gate[l],
                               w_br_diff[l].astype(BF16), w_br_delta[l].astype(BF16), w_br_mem[l].astype(BF16),
                               w_out[l].astype(BF16), ffn_norm_g[l],
                               gate_col0=dv_w + 4 * dl_w, tm=min(MERGE_TM, m))

        qry = norm_matmul(h2, w_query[l].astype(BF16))
        idx, gates = peer_route(qry, sub_keys[l].astype(BF16), tt=min(ROUTE_TT, m))
        sub = d // LANES
        uv3 = jnp.concatenate([expert_u[l].reshape(-1, sub, LANES), expert_v[l].reshape(-1, sub, LANES)],
                              axis=1).astype(BF16)
        x2d = peer_experts(idx, gates, h2, x1, uv3, tb=PEER_TB)
    return x2d.reshape(b, t, d)
```

```python
import functools
import math

import jax
import jax.numpy as jnp
from jax import lax
from jax.experimental import pallas as pl
from jax.experimental.pallas import tpu as pltpu

F32 = jnp.float32
BF16 = jnp.bfloat16
EPS = 1e-6
NEG = -1e30
LOG2E = math.log2(math.e)

N_DIFF_HEADS = 8
DIFF_HEAD_DIM = 64
N_DELTA_HEADS = 8
DELTA_HEAD_DIM = 128
CONV_WIDTH = 4
CHUNK = 64
N_MEM_HEADS = 4
MEM_HEAD_DIM = 256
REL_BUCKETS = 32
REL_MAX_DIST = 128
PEER_HEADS = 8
PEER_KEYS = 128
PEER_TOPK = 16
PEER_HALF = 128
LANES = 128
SUBLANES = 8
VMEM_LIMIT = 56 * 1024 * 1024

MATMUL_TM = 1024
MATMUL_TN = 2048
MATMUL_TN_GROUP = 512
ATTN_TQ = 256
ATTN_HEADS_PER_STEP = 8
DELTA_TT = 256
DELTA_HEADS_PER_STEP = 8
MEM_ATTN_TQ = 512
MERGE_TM = 512
ROUTE_TT = 1024
PEER_TB = 8
PEER_NBUF = 4
PEER_DIST = 3


def _cparams(sem):
    return pltpu.CompilerParams(dimension_semantics=sem, vmem_limit_bytes=VMEM_LIMIT)


def _norm_matmul_kernel(*refs, has_norm, has_group):
    it = iter(refs)
    x_ref = next(it)
    g_ref = next(it) if has_norm else None
    w_ref = next(it)
    gm_ref = next(it) if has_group else None
    gain_ref = next(it) if has_group else None
    o_ref = next(it)
    h_ref = next(it)

    @pl.when(pl.program_id(1) == 0)
    def _():
        x = x_ref[...].astype(F32)
        if has_norm:
            ms = jnp.mean(x * x, axis=-1, keepdims=True)
            x = x * lax.rsqrt(ms + EPS) * g_ref[...]
        h_ref[...] = x.astype(BF16)

    y = jnp.dot(h_ref[...], w_ref[...], preferred_element_type=F32)
    if has_group:
        ms = jnp.dot((y * y).astype(BF16), gm_ref[...], preferred_element_type=F32)
        y = y * lax.rsqrt(ms + EPS) * gain_ref[...]
    o_ref[...] = y.astype(o_ref.dtype)


def norm_matmul(x, w, *, norm_g=None, group=None, group_gain=None, out_dtype=BF16, tm=MATMUL_TM, tn=None):
    m, k = x.shape
    n = w.shape[1]
    has_norm = norm_g is not None
    has_group = group is not None
    if tn is None:
        tn = MATMUL_TN_GROUP if has_group else min(n, MATMUL_TN)
    tm = min(tm, m)
    assert m % tm == 0 and n % tn == 0
    in_specs = [pl.BlockSpec((tm, k), lambda i, j: (i, 0))]
    args = [x]
    if has_norm:
        in_specs.append(pl.BlockSpec((1, k), lambda i, j: (0, 0)))
        args.append(norm_g.reshape(1, k).astype(F32))
    in_specs.append(pl.BlockSpec((k, tn), lambda i, j: (0, j)))
    args.append(w)
    if has_group:
        gid = jnp.arange(tn) // group
        gm = jnp.where(gid[:, None] == gid[None, :], 1.0 / group, 0.0).astype(BF16)
        in_specs.append(pl.BlockSpec((tn, tn), lambda i, j: (0, 0)))
        args.append(gm)
        in_specs.append(pl.BlockSpec((1, tn), lambda i, j: (0, j)))
        args.append(group_gain.reshape(1, n).astype(F32))
    return pl.pallas_call(
        functools.partial(_norm_matmul_kernel, has_norm=has_norm, has_group=has_group),
        out_shape=jax.ShapeDtypeStruct((m, n), out_dtype),
        grid=(m // tm, n // tn),
        in_specs=in_specs,
        out_specs=pl.BlockSpec((tm, tn), lambda i, j: (i, j)),
        scratch_shapes=[pltpu.VMEM((tm, k), BF16)],
        compiler_params=_cparams(("parallel", "arbitrary")),
        name="norm_matmul",
    )(*args)


def _t5_bucket(n):
    max_exact = REL_BUCKETS // 2
    nf = jnp.maximum(n, 1).astype(F32)
    large = max_exact + (jnp.log(nf / max_exact) / math.log(REL_MAX_DIST / max_exact)
                         * (REL_BUCKETS - max_exact)).astype(jnp.int32)
    large = jnp.minimum(large, REL_BUCKETS - 1)
    return jnp.where(n < max_exact, n, large)


def _rel_bias_kernel(table_ref, o_ref, *, tq):
    hm = pl.program_id(0)
    c = lax.broadcasted_iota(jnp.int32, (2 * tq, tq), 0)
    r = lax.broadcasted_iota(jnp.int32, (2 * tq, tq), 1)
    bucket = _t5_bucket(jnp.maximum(r - c + tq, 0))
    acc = jnp.zeros((2 * tq, tq), F32)
    for b in range(REL_BUCKETS):
        acc = jnp.where(bucket == b, table_ref[b, hm], acc)
    o_ref[0] = acc * LOG2E


def rel_bias_tiles(rel_table, tq):
    nmaps = rel_table.shape[1]
    return pl.pallas_call(
        functools.partial(_rel_bias_kernel, tq=tq),
        out_shape=jax.ShapeDtypeStruct((nmaps, 2 * tq, tq), F32),
        grid=(nmaps,),
        in_specs=[pl.BlockSpec(memory_space=pltpu.SMEM)],
        out_specs=pl.BlockSpec((1, 2 * tq, tq), lambda i: (i, 0, 0)),
        compiler_params=_cparams(("arbitrary",)),
        name="rel_bias_tiles",
    )(rel_table.astype(F32))


def _diff_attn_kernel(scal_ref, q_ref, k_ref, vt_ref, bias_ref, subg_ref, o_ref, *, tq, hpb):
    hb = pl.program_id(1)
    i = pl.program_id(2)
    heads = range(hpb)
    lam = scal_ref[0]
    out_scale = scal_ref[1]
    w2 = 2 * tq

    sub = lax.broadcasted_iota(jnp.int32, (LANES, tq), 0)
    col2 = lax.broadcasted_iota(jnp.int32, (1, w2), 1)
    qt, cfar = [], []
    for hd in heads:
        q = q_ref[:, hd * LANES:(hd + 1) * LANES].astype(F32).T
        qt.append(jnp.concatenate([jnp.where(sub < DIFF_HEAD_DIM, q, 0.0),
                                   jnp.where(sub >= DIFF_HEAD_DIM, q, 0.0)], axis=1).astype(BF16))
        h = hb * hpb + hd
        cfar.append(jnp.where(col2 < tq, scal_ref[2 + 2 * h], scal_ref[3 + 2 * h]))

    def update(carry, s, vt, shift=None):
        m, l, acc = carry
        colmax = [jnp.max(x, axis=0, keepdims=True) for x in s]
        if shift is not None:
            colmax = [a + c for a, c in zip(colmax, shift)]
        m_new = [jnp.maximum(a, b_) for a, b_ in zip(m, colmax)]
        alpha = [jnp.exp2(a - b_) for a, b_ in zip(m, m_new)]
        sub_ = m_new if shift is None else [a - c for a, c in zip(m_new, shift)]
        p = [jnp.exp2(x - a) for x, a in zip(s, sub_)]
        l = [a * b_ + jnp.sum(x, axis=0, keepdims=True) for a, b_, x in zip(l, alpha, p)]
        acc = [a * b_ + jnp.dot(v_, x.astype(BF16), preferred_element_type=F32)
               for a, b_, x, v_ in zip(acc, alpha, p, vt)]
        return m_new, l, acc

    def scores(tile0, ntiles):
        rows = pl.ds(pl.multiple_of(tile0 * tq, tq), ntiles * tq)
        s = [jnp.dot(k_ref[rows, hd * LANES:(hd + 1) * LANES], qt[hd], preferred_element_type=F32) for hd in heads]
        vt = [jnp.concatenate([vt_ref[0, hd, tile0 + c] for c in range(ntiles)], axis=1) if ntiles > 1
              else vt_ref[0, hd, tile0] for hd in heads]
        return s, vt

    def pack(carry):
        m, l, acc = carry
        return tuple(m) + tuple(l) + tuple(acc)

    def unpack(flat):
        return list(flat[0:hpb]), list(flat[hpb:2 * hpb]), list(flat[2 * hpb:3 * hpb])

    def far_step(j, flat):
        s, vt = scores(2 * j, 2)
        return pack(update(unpack(flat), s, vt, cfar))

    def far_single(flat):
        s, vt = scores(i - 2, 1)
        return pack(update(unpack(flat), s, vt, cfar))

    n_far = jnp.maximum(i - 1, 0)
    init = ([jnp.full((1, w2), NEG, F32) for _ in heads], [jnp.zeros((1, w2), F32) for _ in heads],
            [jnp.zeros((LANES, w2), F32) for _ in heads])
    flat = lax.fori_loop(0, n_far // 2, far_step, pack(init))
    flat = lax.cond(n_far % 2 == 1, far_single, lambda c: c, flat)
    carry = unpack(flat)

    def near_bias(hd, lo):
        return jnp.concatenate([bias_ref[2 * hd, lo:lo + tq, :], bias_ref[2 * hd + 1, lo:lo + tq, :]], axis=1)

    s, vt = scores(jnp.maximum(i - 1, 0), 1)
    s = [jnp.where(i >= 1, x + near_bias(hd, 0), NEG) for hd, x in zip(heads, s)]
    carry = update(carry, s, vt)

    s, vt = scores(i, 1)
    kk = lax.broadcasted_iota(jnp.int32, (tq, w2), 0)
    qi = lax.broadcasted_iota(jnp.int32, (tq, w2), 1)
    qi = jnp.where(qi >= tq, qi - tq, qi)
    s = [jnp.where(kk <= qi, x + near_bias(hd, tq), NEG) for hd, x in zip(heads, s)]
    m, l, acc = update(carry, s, vt)

    outs = []
    for hd in heads:
        o = acc[hd] / l[hd]
        o = o[:, 0:tq] - lam * o[:, tq:w2]
        ms = jnp.mean(o * o, axis=0, keepdims=True)
        o = o * lax.rsqrt(ms + EPS) * subg_ref[...] * out_scale
        outs.append(o.T)
    o_ref[...] = jnp.concatenate(outs, axis=-1).astype(o_ref.dtype)


def diff_attention(qk, vt, bias_tiles, scal, subln_g, *, b, t, tq, hpb):
    m = b * t
    nq = t // tq
    nh = N_DIFF_HEADS
    wl = hpb * LANES
    assert nh % hpb == 0
    return pl.pallas_call(
        functools.partial(_diff_attn_kernel, tq=tq, hpb=hpb),
        out_shape=jax.ShapeDtypeStruct((m, nh * LANES), BF16),
        grid=(b, nh // hpb, nq),
        in_specs=[
            pl.BlockSpec(memory_space=pltpu.SMEM),
            pl.BlockSpec((tq, wl), lambda bi, h, i: (bi * nq + i, h)),
            pl.BlockSpec((t, wl), lambda bi, h, i: (bi, nh // hpb + h), pipeline_mode=pl.Buffered(1)),
            pl.BlockSpec((1, hpb, nq, LANES, tq), lambda bi, h, i: (bi, h, 0, 0, 0), pipeline_mode=pl.Buffered(1)),
            pl.BlockSpec((2 * hpb, 2 * tq, tq), lambda bi, h, i: (h, 0, 0), pipeline_mode=pl.Buffered(1)),
            pl.BlockSpec((LANES, 1), lambda bi, h, i: (0, 0)),
        ],
        out_specs=pl.BlockSpec((tq, wl), lambda bi, h, i: (bi * nq + i, h)),
        compiler_params=_cparams(("parallel", "parallel", "arbitrary")),
        name="diff_attention",
    )(scal, qk, qk, vt, bias_tiles, subln_g.reshape(LANES, 1).astype(F32))


def _sigmoid(x):
    return 1.0 / (1.0 + jnp.exp(-x))


def _softplus(x):
    return jnp.maximum(x, 0.0) + jnp.log(1.0 + jnp.exp(-jnp.abs(x)))


def _delta_kernel(hp_ref, xq_ref, xk_ref, xv_ref, pq_ref, pk_ref, pv_ref, wq_ref, wk_ref, wv_ref,
                  ab_ref, abt_ref, z_ref, og_ref, o_ref, s_ref, *, tt, nhb):
    hb = pl.program_id(1)
    i = pl.program_id(2)
    nc = tt // CHUNK
    dh = DELTA_HEAD_DIM
    heads = range(nhb)

    @pl.when(i == 0)
    def _():
        s_ref[...] = jnp.zeros_like(s_ref)

    def conv_silu(x_ref, p_ref, w_ref):
        prev = jnp.where(i > 0, p_ref[...].astype(F32), 0.0)
        xf = jnp.concatenate([prev, x_ref[...].astype(F32)], axis=0)
        w = w_ref[...]
        base = SUBLANES - (CONV_WIDTH - 1)
        y = xf[base:base + tt] * w[0:1]
        for c in range(1, CONV_WIDTH):
            y = y + xf[base + c:base + c + tt] * w[c:c + 1]
        return y * _sigmoid(y)

    def split(x):
        return [x[:, hd * dh:(hd + 1) * dh] for hd in heads]

    q = split(conv_silu(xq_ref, pq_ref, wq_ref))
    k = split(conv_silu(xk_ref, pk_ref, wk_ref))
    v = split(conv_silu(xv_ref, pv_ref, wv_ref))
    q = [x * lax.rsqrt(jnp.sum(x * x, axis=-1, keepdims=True) + EPS) * (dh ** -0.5) for x in q]
    k = [x * lax.rsqrt(jnp.sum(x * x, axis=-1, keepdims=True) + EPS) for x in k]

    ab = ab_ref[...]
    lane = lax.broadcasted_iota(jnp.int32, ab.shape, 1)
    beta_col, g_col, g_row = [], [], []
    for hd in heads:
        h = hb * nhb + hd
        neg_a = -jnp.exp(jnp.full((1, 1), hp_ref[0, h], F32))
        dt_bias = hp_ref[1, h]
        lb_col = jnp.sum(jnp.where(lane == h, ab, 0.0), axis=-1, keepdims=True)
        la_col = jnp.sum(jnp.where(lane == N_DELTA_HEADS + h, ab, 0.0), axis=-1, keepdims=True)
        la_row = abt_ref[pl.ds(N_DELTA_HEADS + h, 1), :]
        beta_col.append(_sigmoid(lb_col))
        g_col.append(neg_a * _softplus(la_col + dt_bias))
        g_row.append(neg_a * _softplus(la_row + dt_bias))

    ri = lax.broadcasted_iota(jnp.int32, (tt, tt), 0)
    ci = lax.broadcasted_iota(jnp.int32, (tt, tt), 1)
    same = (ri // CHUNK) == (ci // CHUNK)
    tril = same & (ri >= ci)
    strict = same & (ri > ci)
    triu = same & (ri <= ci)

    def mm(a, b):
        return jnp.dot(a.astype(BF16), b.astype(BF16), preferred_element_type=F32)

    def mm_nt(a, b):
        return lax.dot_general(a.astype(BF16), b.astype(BF16), (((1,), (1,)), ((), ())),
                               preferred_element_type=F32)

    def mm_tn(a, b):
        return lax.dot_general(a.astype(BF16), b.astype(BF16), (((0,), (0,)), ((), ())),
                               preferred_element_type=F32)

    gc_col = [jnp.sum(jnp.where(tril, g, 0.0), axis=-1, keepdims=True) for g in g_row]
    gc_row = [jnp.sum(jnp.where(triu, g, 0.0), axis=0, keepdims=True) for g in g_col]
    decay = [jnp.where(tril, jnp.exp(jnp.where(tril, a - b, 0.0)), 0.0) for a, b in zip(gc_col, gc_row)]
    e_gc = [jnp.exp(g) for g in gc_col]
    kk = [mm_nt(x, x) for x in k]
    qk = [mm_nt(a, b) for a, b in zip(q, k)]
    xm = [jnp.where(strict, -(b * kx * d), 0.0) for b, kx, d in zip(beta_col, kk, decay)]
    intra = [jnp.where(tril, a * d, 0.0) for a, d in zip(qk, decay)]
    r = [jnp.concatenate([b * vx, (b * e) * kx], axis=-1) for b, vx, e, kx in zip(beta_col, v, e_gc, k)]
    n_sq = CHUNK.bit_length() - 1
    for lvl in range(n_sq):
        r = [x + mm(m_, x) for m_, x in zip(xm, r)]
        if lvl + 1 < n_sq:
            xm = [mm(m_, m_) for m_ in xm]
    u = [x[:, 0:dh] for x in r]
    w = [x[:, dh:2 * dh] for x in r]
    qg = [a * e for a, e in zip(q, e_gc)]

    state = [s_ref[hd] for hd in heads]
    o_state = [[] for _ in heads]
    v_new = [[] for _ in heads]
    for c in range(nc):
        r0 = c * CHUNK
        sl = slice(r0, r0 + CHUNK)
        g_last = [g[r0 + CHUNK - 1:r0 + CHUNK] for g in gc_col]
        vn = [u[hd][sl] - mm(w[hd][sl], state[hd]) for hd in heads]
        os_ = [mm(qg[hd][sl], state[hd]) for hd in heads]
        k_dec = [k[hd][sl] * jnp.exp(g_last[hd] - gc_col[hd][sl]) for hd in heads]
        state = [state[hd] * jnp.exp(g_last[hd]) + mm_tn(k_dec[hd], vn[hd]) for hd in heads]
        for hd in heads:
            v_new[hd].append(vn[hd])
            o_state[hd].append(os_[hd])
    for hd in heads:
        s_ref[hd] = state[hd]

    outs = []
    for hd in heads:
        o = jnp.concatenate(o_state[hd], axis=0) + mm(intra[hd], jnp.concatenate(v_new[hd], axis=0))
        ms = jnp.mean(o * o, axis=-1, keepdims=True)
        outs.append(o * lax.rsqrt(ms + EPS) * og_ref[...])
    z = z_ref[...].astype(F32)
    o_ref[...] = (jnp.concatenate(outs, axis=-1) * (z * _sigmoid(z))).astype(o_ref.dtype)


def gated_deltanet(rest, ab, abt, conv_w, head_params, out_g, *, b, t, tt, nhb, qkv_col0, z_col0):
    m = b * t
    nt = t // tt
    nh = N_DELTA_HEADS
    sub = tt // SUBLANES
    wl = nhb * LANES
    assert nh % nhb == 0 and qkv_col0 % nhb == 0 and z_col0 % nhb == 0

    def cur(off):
        return pl.BlockSpec((tt, wl), lambda bi, h, i: (bi * nt + i, off // nhb + h))

    def prev(off):
        return pl.BlockSpec((SUBLANES, wl),
                            lambda bi, h, i: (jnp.maximum((bi * nt + i) * sub - 1, 0), off // nhb + h))

    def wspec(off):
        return pl.BlockSpec((CONV_WIDTH, wl), lambda bi, h, i: (0, off // nhb + h))

    return pl.pallas_call(
        functools.partial(_delta_kernel, tt=tt, nhb=nhb),
        out_shape=jax.ShapeDtypeStruct((m, nh * LANES), BF16),
        grid=(b, nh // nhb, nt),
        in_specs=[
            pl.BlockSpec(memory_space=pltpu.SMEM),
            cur(qkv_col0), cur(qkv_col0 + nh), cur(qkv_col0 + 2 * nh),
            prev(qkv_col0), prev(qkv_col0 + nh), prev(qkv_col0 + 2 * nh),
            wspec(0), wspec(nh), wspec(2 * nh),
            pl.BlockSpec((tt, LANES), lambda bi, h, i: (bi * nt + i, 0)),
            pl.BlockSpec((2 * nh, tt), lambda bi, h, i: (0, bi * nt + i)),
            cur(z_col0),
            pl.BlockSpec((1, LANES), lambda bi, h, i: (0, 0)),
        ],
        out_specs=pl.BlockSpec((tt, wl), lambda bi, h, i: (bi * nt + i, h)),
        scratch_shapes=[pltpu.VMEM((nhb, DELTA_HEAD_DIM, DELTA_HEAD_DIM), F32)],
        compiler_params=_cparams(("parallel", "parallel", "arbitrary")),
        name="gated_deltanet",
    )(head_params, rest, rest, rest, rest, rest, rest, conv_w, conv_w, conv_w, ab, abt, rest,
      out_g.reshape(1, LANES).astype(F32))


def _mem_attn_kernel(q_ref, k_ref, v_ref, o_ref):
    outs = []
    for h in range(N_MEM_HEADS):
        sl = slice(h * MEM_HEAD_DIM, (h + 1) * MEM_HEAD_DIM)
        s = lax.dot_general(q_ref[:, sl], k_ref[:, sl], (((1,), (1,)), ((), ())), preferred_element_type=F32)
        s = s - jnp.max(s, axis=-1, keepdims=True)
        p = jnp.exp(s)
        p = p / jnp.sum(p, axis=-1, keepdims=True)
        outs.append(jnp.dot(p.astype(BF16), v_ref[:, sl], preferred_element_type=F32))
    o_ref[...] = jnp.concatenate(outs, axis=-1).astype(o_ref.dtype)


def memory_attention(mq, mk, mv, *, b, t, mtok, tq):
    nq = t // tq
    w = N_MEM_HEADS * MEM_HEAD_DIM
    return pl.pallas_call(
        _mem_attn_kernel,
        out_shape=jax.ShapeDtypeStruct((b * t, w), BF16),
        grid=(b, nq),
        in_specs=[
            pl.BlockSpec((tq, w), lambda bi, i: (bi * nq + i, 0)),
            pl.BlockSpec((mtok, w), lambda bi, i: (bi, 0)),
            pl.BlockSpec((mtok, w), lambda bi, i: (bi, 0)),
        ],
        out_specs=pl.BlockSpec((tq, w), lambda bi, i: (bi * nq + i, 0)),
        compiler_params=_cparams(("parallel", "arbitrary")),
        name="memory_attention",
    )(mq, mk, mv)


def _merge_kernel(x_ref, ya_ref, yb_ref, yc_ref, ga_ref, gb_ref, gc_ref, bg_ref,
                  wa_ref, wb_ref, wc_ref, wo_ref, fg_ref, x1_ref, h2_ref):
    def branch(y_ref, g_ref, w_ref, idx):
        gate = _sigmoid(g_ref[...].astype(F32) + bg_ref[idx:idx + 1, :])
        return gate * jnp.dot(y_ref[...], w_ref[...], preferred_element_type=F32)

    merged = branch(ya_ref, ga_ref, wa_ref, 0) + branch(yb_ref, gb_ref, wb_ref, 1) + branch(yc_ref, gc_ref, wc_ref, 2)
    x1 = x_ref[...] + jnp.dot(merged.astype(BF16), wo_ref[...], preferred_element_type=F32)
    x1_ref[...] = x1
    ms = jnp.mean(x1 * x1, axis=-1, keepdims=True)
    h2_ref[...] = (x1 * lax.rsqrt(ms + EPS) * fg_ref[...]).astype(h2_ref.dtype)


def merge_project(x2d, y_diff, y_delta, y_mem, rest, b_gate, w_a, w_b, w_c, w_o, ffn_g, *, gate_col0, tm):
    m, d = x2d.shape
    row = lambda i: (i, 0)
    const = lambda i: (0, 0)
    assert gate_col0 % d == 0

    def gspec(j):
        return pl.BlockSpec((tm, d), lambda i: (i, gate_col0 // d + j))

    return pl.pallas_call(
        _merge_kernel,
        out_shape=(jax.ShapeDtypeStruct((m, d), F32), jax.ShapeDtypeStruct((m, d), BF16)),
        grid=(m // tm,),
        in_specs=[
            pl.BlockSpec((tm, d), row), pl.BlockSpec((tm, d), row), pl.BlockSpec((tm, d), row),
            pl.BlockSpec((tm, d), row), gspec(0), gspec(1), gspec(2),
            pl.BlockSpec((3, d), const),
            pl.BlockSpec((d, d), const), pl.BlockSpec((d, d), const), pl.BlockSpec((d, d), const),
            pl.BlockSpec((d, d), const), pl.BlockSpec((1, d), const),
        ],
        out_specs=(pl.BlockSpec((tm, d), row), pl.BlockSpec((tm, d), row)),
        compiler_params=_cparams(("parallel",)),
        name="merge_project",
    )(x2d, y_diff, y_delta, y_mem, rest, rest, rest, b_gate.reshape(3, d).astype(F32),
      w_a, w_b, w_c, w_o, ffn_g.reshape(1, d).astype(F32))


def _topk_rows(s, key, extra, k):
    big = jnp.iinfo(jnp.int32).max
    vals, keys, ext = [], [], []
    for _ in range(k):
        m = jnp.max(s, axis=0, keepdims=True)
        am = jnp.min(jnp.where(s == m, key, big), axis=0, keepdims=True)
        hit = key == am
        if extra is not None:
            ext.append(jnp.max(jnp.where(hit, extra, -1), axis=0, keepdims=True))
        s = jnp.where(hit, -jnp.inf, s)
        vals.append(m)
        keys.append(am)
    cat = lambda xs: jnp.concatenate(xs, axis=0)
    return cat(vals), cat(keys), (cat(ext) if extra is not None else None)


def _peer_route_kernel(q_ref, keys_ref, idx_ref, gate_ref, idx_acc, gate_acc):
    h = pl.program_id(1)
    k = PEER_TOPK
    q = q_ref[...]
    tt = q.shape[0]

    def half(p):
        qp = q[:, p * PEER_HALF:(p + 1) * PEER_HALF]
        st = lax.dot_general(keys_ref[0, p], qp, (((1,), (1,)), ((), ())), preferred_element_type=F32)
        v, r, _ = _topk_rows(st, lax.broadcasted_iota(jnp.int32, st.shape, 0), None, k)
        return v, r

    s1, i1 = half(0)
    s2, i2 = half(1)
    g = SUBLANES
    sub = lax.broadcasted_iota(jnp.int32, (g, tt), 0)
    e1 = i1 * PEER_KEYS
    pieces = []
    for b in range(g):
        lim = min(g, k // (b + 1))
        sc = s1[0:g] + s2[b:b + 1]
        if lim < g:
            sc = jnp.where(sub < lim, sc, -jnp.inf)
        pieces.append((sc, sub * k + b, e1[0:g] + i2[b:b + 1]))
    pieces.append((s1[g:k] + s2[0:1], (sub + g) * k, e1[g:k] + i2[0:1]))
    pieces.append((s1[0:1] + s2[g:k], sub + g, e1[0:1] + i2[g:k]))
    cand_s = jnp.concatenate([p[0] for p in pieces], axis=0)
    cand_k = jnp.concatenate([p[1] for p in pieces], axis=0)
    cand_e = jnp.concatenate([p[2] for p in pieces], axis=0)
    top_s, _, top_e = _topk_rows(cand_s, cand_k, cand_e, k)
    e = jnp.exp(top_s - top_s[0:1])
    rows = pl.ds(pl.multiple_of(h * k, k), k)
    gate_acc[rows, :] = e / jnp.sum(e, axis=0, keepdims=True)
    idx_acc[rows, :] = top_e.astype(F32)

    @pl.when(h == pl.num_programs(1) - 1)
    def _():
        gate_ref[...] = gate_acc[...].T
        idx_ref[...] = idx_acc[...].T.astype(jnp.int32)


def peer_route(qry, sub_keys, *, tt):
    m = qry.shape[0]
    nh = PEER_HEADS
    ne = nh * PEER_TOPK
    return pl.pallas_call(
        _peer_route_kernel,
        out_shape=(jax.ShapeDtypeStruct((m, ne), jnp.int32), jax.ShapeDtypeStruct((m, ne), F32)),
        grid=(m // tt, nh),
        in_specs=[
            pl.BlockSpec((tt, 2 * PEER_HALF), lambda i, h: (i, h)),
            pl.BlockSpec((1, 2, PEER_KEYS, PEER_HALF), lambda i, h: (h, 0, 0, 0)),
        ],
        out_specs=(pl.BlockSpec((tt, ne), lambda i, h: (i, 0)), pl.BlockSpec((tt, ne), lambda i, h: (i, 0))),
        scratch_shapes=[pltpu.VMEM((ne, tt), F32), pltpu.VMEM((ne, tt), F32)],
        compiler_params=_cparams(("parallel", "arbitrary")),
        name="peer_route",
    )(qry, sub_keys)


def _gelu_exact(x):
    return 0.5 * x * (1.0 + lax.erf(x * (2.0 ** -0.5)))


def _peer_expert_kernel(idx_ref, idxn_ref, gate_ref, h_ref, x_ref, pool_ref, poolt_ref, uv_hbm,
                        o_ref, *scratch, tb, ne, nbuf, dist):
    bufs, sem = scratch[:nbuf], scratch[nbuf]
    i = pl.program_id(0)
    n = pl.num_programs(0)
    rows = tb * ne
    d_sub = h_ref.shape[1] // LANES
    n_seg = 2 * tb
    seg = rows // n_seg

    def issue_rows(iref, row0, buf, which, r_lo, r_hi):
        for r in range(r_lo, r_hi):
            e = iref[row0 + r // ne, r % ne]
            pltpu.make_async_copy(uv_hbm.at[e], buf.at[r], sem.at[which, r % 2]).start(priority=r % 2)

    def wait_tile(buf, which):
        for k in range(2):
            pltpu.make_async_copy(uv_hbm.at[pl.ds(0, rows // 2)], buf.at[pl.ds(0, rows // 2)],
                                  sem.at[which, k]).wait()

    @pl.when(i == 0)
    def _():
        for a in range(dist):
            def tok(t, carry, a=a):
                for j in range(ne):
                    pltpu.make_async_copy(uv_hbm.at[idx_ref[a * tb + t, j]], bufs[a].at[t * ne + j],
                                          sem.at[a, j % 2]).start(priority=j % 2)
                return carry
            lax.fori_loop(0, tb, tok, 0)

    sub_id = lax.broadcasted_iota(jnp.int32, (d_sub, ne * d_sub), 0)
    col_id = lax.broadcasted_iota(jnp.int32, (d_sub, ne * d_sub), 1)
    diag = (col_id % d_sub) == sub_id

    def tile(row0, buf, nxt_iref, nxt_row0, nxt_buf, nxt_which):
        drows = []
        h8 = h_ref[pl.ds(row0, tb), :].astype(F32).reshape(tb, d_sub, LANES).astype(BF16)
        for t in range(tb):
            zu = buf[pl.ds(t * ne, ne), 0:d_sub, :].reshape(ne * d_sub, LANES).astype(BF16)
            c = lax.dot_general(h8[t], zu, (((1,), (1,)), ((), ())), preferred_element_type=F32)
            drows.append(jnp.sum(jnp.where(diag, c, 0.0), axis=0, keepdims=True))
            issue_rows(nxt_iref, nxt_row0, nxt_buf, nxt_which, t * seg, (t + 1) * seg)
        dall = jnp.concatenate(drows, axis=0)
        d_hi = dall.astype(BF16)
        d_lo = (dall - d_hi.astype(F32)).astype(BF16)
        a = (jnp.dot(d_hi, pool_ref[...], preferred_element_type=F32)
             + jnp.dot(d_lo, pool_ref[...], preferred_element_type=F32))
        w = gate_ref[pl.ds(row0, tb), :] * _gelu_exact(a)
        wrep = jnp.dot(w.astype(BF16), poolt_ref[...], preferred_element_type=F32)
        outs = []
        for t in range(tb):
            wexp = jnp.where(diag, wrep[t:t + 1, :], 0.0).astype(BF16)
            zv = buf[pl.ds(t * ne, ne), d_sub:2 * d_sub, :].reshape(ne * d_sub, LANES).astype(BF16)
            outs.append(jnp.dot(wexp, zv, preferred_element_type=F32))
            issue_rows(nxt_iref, nxt_row0, nxt_buf, nxt_which, (tb + t) * seg, (tb + t + 1) * seg)
        o_ref[pl.ds(row0, tb), :] = x_ref[pl.ds(row0, tb), :] + jnp.stack(outs).reshape(tb, d_sub * LANES)

    for a in range(nbuf):
        wait_tile(bufs[a], a)
        nxt = a + dist
        if nxt < nbuf:
            tile(a * tb, bufs[a], idx_ref, nxt * tb, bufs[nxt], nxt)
        else:
            tile(a * tb, bufs[a], idxn_ref, (nxt - nbuf) * tb, bufs[nxt - nbuf], nxt - nbuf)

    @pl.when(i == n - 1)
    def _():
        for a in range(dist):
            wait_tile(bufs[a], a)


def peer_experts(idx, gates, h2d, x2d, uv3, *, tb):
    m, ne = idx.shape
    d_sub = h2d.shape[1] // LANES
    nbuf, dist = PEER_NBUF, PEER_DIST
    ts = nbuf * tb
    n = m // ts
    rows = tb * ne
    cid = jnp.arange(ne * d_sub) // d_sub
    pool = (cid[:, None] == jnp.arange(ne)[None, :]).astype(BF16)
    tok = lambda i: (i, 0)
    return pl.pallas_call(
        functools.partial(_peer_expert_kernel, tb=tb, ne=ne, nbuf=nbuf, dist=dist),
        out_shape=jax.ShapeDtypeStruct(x2d.shape, F32),
        grid=(n,),
        in_specs=[
            pl.BlockSpec((ts, ne), tok, memory_space=pltpu.SMEM),
            pl.BlockSpec((ts, ne), lambda i: (jnp.minimum(i + 1, n - 1), 0), memory_space=pltpu.SMEM),
            pl.BlockSpec((ts, ne), tok),
            pl.BlockSpec((ts, d_sub * LANES), tok),
            pl.BlockSpec((ts, d_sub * LANES), tok),
            pl.BlockSpec((ne * d_sub, ne), lambda i: (0, 0)),
            pl.BlockSpec((ne, ne * d_sub), lambda i: (0, 0)),
            pl.BlockSpec(memory_space=pl.ANY),
        ],
        out_specs=pl.BlockSpec((ts, d_sub * LANES), tok),
        scratch_shapes=[pltpu.VMEM((rows, 2 * d_sub, LANES), F32) for _ in range(nbuf)]
        + [pltpu.SemaphoreType.DMA((nbuf, 2))],
        compiler_params=_cparams(("arbitrary",)),
        name="peer_experts",
    )(idx, idx, gates, h2d, x2d, pool, pool.T, uv3)


def kernel(x, mem, positions, attn_norm_g, mem_norm_g, w_in, b_gate, diff_q_norm_g, diff_k_norm_g, lambda_q1, lambda_k1, lambda_q2, lambda_k2, diff_subln_g, rel_bias_table, conv_w, a_log, dt_bias, delta_out_norm_g, w_mem_kv, mem_q_norm_g, mem_k_norm_g, w_br_diff, w_br_delta, w_br_mem, w_out, ffn_norm_g, w_query, sub_keys, expert_u, expert_v):
    del positions
    b, t, d = x.shape
    mtok = mem.shape[1]
    m = b * t
    depth = w_in.shape[0]
    nh = N_DIFF_HEADS
    qk_w = nh * 2 * DIFF_HEAD_DIM
    dv_w = nh * 2 * DIFF_HEAD_DIM
    dl_w = N_DELTA_HEADS * DELTA_HEAD_DIM
    mem_w = N_MEM_HEADS * MEM_HEAD_DIM
    tq = min(ATTN_TQ, t)
    tt_delta = min(DELTA_TT, t)

    x2d = x.reshape(m, d)
    for l in range(depth):
        lam_init = 0.8 - 0.6 * math.exp(-0.3 * l)
        wl = w_in[l]
        o = 0
        w_qk = wl[:, o:o + 2 * qk_w]; o += 2 * qk_w
        w_dv = wl[:, o:o + dv_w]; o += dv_w
        w_lqkv = wl[:, o:o + 3 * dl_w]; o += 3 * dl_w
        w_lz = wl[:, o:o + dl_w]; o += dl_w
        w_ab = wl[:, o:o + 2 * N_DELTA_HEADS]; o += 2 * N_DELTA_HEADS
        w_mq = wl[:, o:o + mem_w]; o += mem_w
        w_gate = wl[:, o:o + 3 * d]

        qk_gain = jnp.concatenate([jnp.tile(diff_q_norm_g[l], 2 * nh) * (DIFF_HEAD_DIM ** -0.5 * LOG2E),
                                   jnp.tile(diff_k_norm_g[l], 2 * nh)])
        qk = norm_matmul(x2d, w_qk.astype(BF16), norm_g=attn_norm_g[l], group=DIFF_HEAD_DIM, group_gain=qk_gain)
        w_rest = jnp.concatenate([w_dv, w_lqkv, w_lz, w_gate], axis=1).astype(BF16)
        rest = norm_matmul(x2d, w_rest, norm_g=attn_norm_g[l])
        mq_gain = jnp.tile(mem_q_norm_g[l], N_MEM_HEADS) * (MEM_HEAD_DIM ** -0.5)
        mq = norm_matmul(x2d, w_mq.astype(BF16), norm_g=attn_norm_g[l], group=MEM_HEAD_DIM, group_gain=mq_gain)
        w_ab_pad = jnp.pad(w_ab, ((0, 0), (0, LANES - 2 * N_DELTA_HEADS))).astype(BF16)
        ab = norm_matmul(x2d, w_ab_pad, norm_g=attn_norm_g[l], out_dtype=F32, tn=LANES)
        abt = ab[:, :2 * N_DELTA_HEADS].T

        lam = (jnp.exp(jnp.sum(lambda_q1[l].astype(F32) * lambda_k1[l].astype(F32)))
               - jnp.exp(jnp.sum(lambda_q2[l].astype(F32) * lambda_k2[l].astype(F32))) + lam_init)
        bias_tiles = rel_bias_tiles(rel_bias_table, tq)
        scal = jnp.concatenate([jnp.stack([lam, jnp.asarray(1.0 - lam_init, F32)]),
                                rel_bias_table[REL_BUCKETS - 1].astype(F32) * LOG2E])
        vt = rest[:, :dv_w].reshape(b, t // tq, tq, nh, LANES).transpose(0, 3, 1, 4, 2)
        y_diff = diff_attention(qk, vt, bias_tiles, scal, diff_subln_g[l], b=b, t=t, tq=tq, hpb=ATTN_HEADS_PER_STEP)

        head_params = jnp.stack([a_log[l], dt_bias[l]]).astype(F32)
        y_delta = gated_deltanet(rest, ab, abt, conv_w[l].astype(F32), head_params, delta_out_norm_g[l],
                                 b=b, t=t, tt=tt_delta, nhb=DELTA_HEADS_PER_STEP, qkv_col0=dv_w // LANES,
                                 z_col0=(dv_w + 3 * dl_w) // LANES)

        mem2d = mem.reshape(b * mtok, d)
        wkv = w_mem_kv[l].astype(BF16)
        mk = norm_matmul(mem2d, wkv[:, :mem_w], norm_g=mem_norm_g[l], group=MEM_HEAD_DIM,
                         group_gain=jnp.tile(mem_k_norm_g[l], N_MEM_HEADS))
        mv = norm_matmul(mem2d, wkv[:, mem_w:], norm_g=mem_norm_g[l])
        y_mem = memory_attention(mq, mk, mv, b=b, t=t, mtok=mtok, tq=min(MEM_ATTN_TQ, t))

        x1, h2 = merge_project(x2d, y_diff, y_delta, y_mem, rest, b_gate[l],
                               w_br_diff[l].astype(BF16), w_br_delta[l].astype(BF16), w_br_mem[l].astype(BF16),
                               w_out[l].astype(BF16), ffn_norm_g[l],
                               gate_col0=dv_w + 4 * dl_w, tm=min(MERGE_TM, m))

        qry = norm_matmul(h2, w_query[l].astype(BF16))
        idx, gates = peer_route(qry, sub_keys[l].astype(BF16), tt=min(ROUTE_TT, m))
        sub = d // LANES
        uv3 = jnp.concatenate([expert_u[l].reshape(-1, sub, LANES), expert_v[l].reshape(-1, sub, LANES)], axis=1)
        x2d = peer_experts(idx, gates, h2, x1, uv3, tb=PEER_TB)
    return x2d.reshape(b, t, d)
```
